```python
import math, functools
import jax, jax.numpy as jnp
from jax import lax
import numpy as np

D_MODEL = 2048
BATCH = 1
SEQ = 8192
DEPTH = 1
DEC_BATCH = 32
DEC_SEQ = 1
PAST_LEN = 8192
PAGE_SIZE = 128

N_HEADS = 8
HEAD_DIM = 64
QK_DIM = 2 * HEAD_DIM
V_DIM = 2 * HEAD_DIM
ROT_DIM = HEAD_DIM // 4
ROPE_THETA = 500000.0
ATTN_W = N_HEADS * V_DIM
CONV_W = D_MODEL // 2
CONV_K = 3
FFN_DIM = 5632
NORM_EPS = 1e-6
SUBLN_EPS = 1e-5
Q_BLOCK = 128
ATTN_SCALE = HEAD_DIM ** -0.5
IN_WIDTHS = (N_HEADS * QK_DIM, N_HEADS * QK_DIM, ATTN_W, CONV_W, CONV_W, CONV_W, D_MODEL, D_MODEL)
IN_W = N_HEADS * QK_DIM * 2 + ATTN_W + 3 * CONV_W + 2 * D_MODEL

kernel_name = "cond_hybrid_diffattn_shortconv_convffn_step"


def _rmsnorm(x, g, eps=NORM_EPS):
    x32 = x.astype(jnp.float32)
    y = x32 * lax.rsqrt(jnp.mean(x32 * x32, axis=-1, keepdims=True) + eps)
    return y.astype(x.dtype) * g


def _lambda_init(layer):
    return 0.8 - 0.6 * math.exp(-0.3 * layer)


def _rope(x, pos):
    half = ROT_DIM // 2
    inv_freq = ROPE_THETA ** (-jnp.arange(0, ROT_DIM, 2, dtype=jnp.float32) / ROT_DIM)
    ang = pos[:, None] * inv_freq[None, :]
    cos = jnp.cos(ang)[None, :, None, None, :].astype(x.dtype)
    sin = jnp.sin(ang)[None, :, None, None, :].astype(x.dtype)
    x1 = x[..., :half]
    x2 = x[..., half:ROT_DIM]
    return jnp.concatenate([x1 * cos - x2 * sin, x2 * cos + x1 * sin, x[..., ROT_DIM:]], axis=-1)


def _causal_dwconv(z, prev, w, b=None):
    t = z.shape[1]
    zp = jnp.concatenate([prev.astype(z.dtype), z], axis=1)
    y = zp[:, 0:t] * w[0]
    for j in range(1, CONV_K):
        y = y + zp[:, j:j + t] * w[j]
    if b is not None:
        y = y + b
    return y, zp[:, t:]


def _diff_combine(s, lam, v_dtype):
    p = jax.nn.softmax(s, axis=-1)
    return (p[:, :, 0] - lam * p[:, :, 1]).astype(v_dtype)


def _prompt_attention(q, k, v, lam):
    b, s = q.shape[0], q.shape[1]
    nb = s // Q_BLOCK
    qb = (q * ATTN_SCALE).reshape(b, nb, Q_BLOCK, N_HEADS, 2, HEAD_DIM).transpose(1, 0, 2, 3, 4, 5)
    starts = jnp.arange(nb, dtype=jnp.int32) * Q_BLOCK
    k_pos = jnp.arange(s, dtype=jnp.int32)

    def block(args):
        qi, start = args
        sc = jnp.einsum('bqhcd,bkhcd->bhcqk', qi, k).astype(jnp.float32)
        q_pos = start + jnp.arange(Q_BLOCK, dtype=jnp.int32)
        sc = jnp.where(k_pos[None, :] <= q_pos[:, None], sc, -jnp.inf)
        a = _diff_combine(sc, lam, v.dtype)
        return jnp.einsum('bhqk,bkhv->bqhv', a, v)

    out = lax.map(block, (qb, starts))
    return out.transpose(1, 0, 2, 3, 4).reshape(b, s, N_HEADS, V_DIM)


def _sample_attention(q, k_new, v_new, lam, k_past=None, v_past=None):
    t = q.shape[1]
    p_len = k_past.shape[1]
    qs = q * ATTN_SCALE
    s_past = jnp.einsum('bqhcd,bkhcd->bhcqk', qs, k_past).astype(jnp.float32)
    s_new = jnp.einsum('bqhcd,bkhcd->bhcqk', qs, k_new).astype(jnp.float32)
    causal = jnp.tril(jnp.ones((t, t), dtype=bool))
    s_new = jnp.where(causal, s_new, -jnp.inf)
    a = _diff_combine(jnp.concatenate([s_past, s_new], axis=-1), lam, v_new.dtype)
    return (jnp.einsum('bhqk,bkhv->bqhv', a[..., :p_len], v_past)
            + jnp.einsum('bhqk,bkhv->bqhv', a[..., p_len:], v_new))


def _layer(x, c, pos, attn_core, conv_prev, ffn_prev, layer_idx,
           w_ada, b_ada, norm1_g, w_in, lambda_q1, lambda_k1, lambda_q2, lambda_k2, subln_g,
           w_attn_out, conv_w, w_conv_out, w_o, norm2_g, w_up, ffn_conv_w, ffn_conv_b, w_down):
    b, t, _ = x.shape
    mod = (jax.nn.silu(c) @ w_ada + b_ada)[:, None, :]
    sh1, sc1, g1, sh2, sc2, g2 = jnp.split(mod, 6, axis=-1)

    h = _rmsnorm(x, norm1_g) * (1 + sc1) + sh1
    proj = h @ w_in
    split_at = np.cumsum(np.array(IN_WIDTHS))[:-1].tolist()
    q, k, v, c_b, c_c, c_x, g_attn, g_conv = jnp.split(proj, split_at, axis=-1)

    q = _rope(q.reshape(b, t, N_HEADS, 2, HEAD_DIM), pos)
    k = _rope(k.reshape(b, t, N_HEADS, 2, HEAD_DIM), pos)
    v = v.reshape(b, t, N_HEADS, V_DIM)
    lam_init = _lambda_init(layer_idx)
    f32 = jnp.float32
    lam = (jnp.exp(jnp.sum(lambda_q1.astype(f32) * lambda_k1.astype(f32)))
           - jnp.exp(jnp.sum(lambda_q2.astype(f32) * lambda_k2.astype(f32))) + lam_init)
    o = attn_core(q, k, v, lam)
    o = _rmsnorm(o, subln_g, SUBLN_EPS) * (1.0 - lam_init)
    y_attn = o.reshape(b, t, ATTN_W) @ w_attn_out

    z = c_c * c_x
    zc, conv_state = _causal_dwconv(z, conv_prev, conv_w)
    y_conv = (c_b * zc) @ w_conv_out

    merged = jax.nn.sigmoid(g_attn) * y_attn + jax.nn.sigmoid(g_conv) * y_conv
    x = x + g1 * (merged @ w_o)

    h2 = _rmsnorm(x, norm2_g) * (1 + sc2) + sh2
    up = h2 @ w_up
    uc, ffn_state = _causal_dwconv(up, ffn_prev, ffn_conv_w, ffn_conv_b)
    gate, val = jnp.split(uc, 2, axis=-1)
    x = x + g2 * ((jax.nn.silu(gate) * val) @ w_down)
    return x, k.reshape(b, t, N_HEADS, QK_DIM), v, conv_state, ffn_state


def setup_inputs(seed: int = 0) -> dict:
    key = jax.random.key(seed)
    ks = jax.random.split(key, 32)
    f32 = jnp.float32

    def nrm(k, shape, scale):
        return jax.random.normal(k, shape, f32) * scale

    n_pages = PAST_LEN // PAGE_SIZE
    n_used = DEC_BATCH * n_pages
    n_pool = (5 * n_used + 3) // 4
    perm = jax.random.permutation(ks[0], n_pool)
    page_table = perm[:n_used].reshape(DEC_BATCH, n_pages).astype(jnp.int32)

    return {
        "x_prompt": nrm(ks[1], (BATCH, SEQ, D_MODEL), 1.0),
        "x_sample": nrm(ks[2], (DEC_BATCH, DEC_SEQ, D_MODEL), 1.0),
        "cache_k": nrm(ks[3], (DEPTH, n_pool, PAGE_SIZE, N_HEADS, QK_DIM), 1.0),
        "cache_v": nrm(ks[4], (DEPTH, n_pool, PAGE_SIZE, N_HEADS, V_DIM), 1.0),
        "state_conv": nrm(ks[5], (DEPTH, DEC_BATCH, CONV_K - 1, CONV_W), 1.0),
        "state_ffn": nrm(ks[6], (DEPTH, DEC_BATCH, CONV_K - 1, 2 * FFN_DIM), 1.0),
        "page_table": page_table,
        "c_prompt": nrm(ks[7], (BATCH, D_MODEL), 1.0),
        "c_sample": nrm(ks[8], (DEC_BATCH, D_MODEL), 1.0),
        "w_ada": nrm(ks[9], (DEPTH, D_MODEL, 6 * D_MODEL), D_MODEL ** -0.5),
        "b_ada": nrm(ks[10], (DEPTH, 6 * D_MODEL), 0.02),
        "norm1_g": 1.0 + nrm(ks[11], (DEPTH, D_MODEL), 0.02),
        "w_in": nrm(ks[12], (DEPTH, D_MODEL, IN_W), D_MODEL ** -0.5),
        "lambda_q1": nrm(ks[13], (DEPTH, HEAD_DIM), 0.1),
        "lambda_k1": nrm(ks[14], (DEPTH, HEAD_DIM), 0.1),
        "lambda_q2": nrm(ks[15], (DEPTH, HEAD_DIM), 0.1),
        "lambda_k2": nrm(ks[16], (DEPTH, HEAD_DIM), 0.1),
        "subln_g": 1.0 + nrm(ks[17], (DEPTH, V_DIM), 0.02),
        "w_attn_out": nrm(ks[18], (DEPTH, ATTN_W, D_MODEL), ATTN_W ** -0.5),
        "conv_w": nrm(ks[19], (DEPTH, CONV_K, CONV_W), CONV_K ** -0.5),
        "w_conv_out": nrm(ks[20], (DEPTH, CONV_W, D_MODEL), CONV_W ** -0.5),
        "w_o": nrm(ks[21], (DEPTH, D_MODEL, D_MODEL), D_MODEL ** -0.5),
        "norm2_g": 1.0 + nrm(ks[22], (DEPTH, D_MODEL), 0.02),
        "w_up": nrm(ks[23], (DEPTH, D_MODEL, 2 * FFN_DIM), D_MODEL ** -0.5),
        "ffn_conv_w": nrm(ks[24], (DEPTH, CONV_K, 2 * FFN_DIM), CONV_K ** -0.5),
        "ffn_conv_b": nrm(ks[25], (DEPTH, 2 * FFN_DIM), 0.02),
        "w_down": nrm(ks[26], (DEPTH, FFN_DIM, D_MODEL), FFN_DIM ** -0.5),
        "final_g": 1.0 + nrm(ks[27], (D_MODEL,), 0.02),
    }


def reference(x_prompt, x_sample, cache_k, cache_v, state_conv, state_ffn, page_table,
              c_prompt, c_sample, w_ada, b_ada, norm1_g, w_in, lambda_q1, lambda_k1,
              lambda_q2, lambda_k2, subln_g, w_attn_out, conv_w, w_conv_out, w_o, norm2_g,
              w_up, ffn_conv_w, ffn_conv_b, w_down, final_g):
    xp, xs = x_prompt, x_sample
    bp, bs = xp.shape[0], xs.shape[0]
    pos_p = jnp.arange(xp.shape[1], dtype=jnp.float32)
    pos_s = PAST_LEN + jnp.arange(xs.shape[1], dtype=jnp.float32)
    kp_l, vp_l, cp_l, fp_l, ks_l, vs_l, cs_l, fs_l = [], [], [], [], [], [], [], []
    for li in range(DEPTH):
        wl = (w_ada[li], b_ada[li], norm1_g[li], w_in[li], lambda_q1[li], lambda_k1[li],
              lambda_q2[li], lambda_k2[li], subln_g[li], w_attn_out[li], conv_w[li],
              w_conv_out[li], w_o[li], norm2_g[li], w_up[li], ffn_conv_w[li], ffn_conv_b[li],
              w_down[li])
        conv0 = jnp.zeros((bp, CONV_K - 1, CONV_W), xp.dtype)
        ffn0 = jnp.zeros((bp, CONV_K - 1, 2 * FFN_DIM), xp.dtype)
        xp, kp, vp, cp, fp = _layer(xp, c_prompt, pos_p, _prompt_attention, conv0, ffn0, li, *wl)

        k_past = cache_k[li][page_table].reshape(bs, -1, N_HEADS, 2, HEAD_DIM)
        v_past = cache_v[li][page_table].reshape(bs, -1, N_HEADS, V_DIM)
        core = functools.partial(_sample_attention, k_past=k_past, v_past=v_past)
        xs, k_s, v_s, c_s, f_s = _layer(xs, c_sample, pos_s, core, state_conv[li], state_ffn[li], li, *wl)

        kp_l.append(kp); vp_l.append(vp); cp_l.append(cp); fp_l.append(fp)
        ks_l.append(k_s); vs_l.append(v_s); cs_l.append(c_s); fs_l.append(f_s)

    y_prompt = _rmsnorm(xp, final_g)
    y_sample = _rmsnorm(xs, final_g)
    return (y_prompt, y_sample,
            jnp.stack(kp_l), jnp.stack(vp_l), jnp.stack(cp_l), jnp.stack(fp_l),
            jnp.stack(ks_l), jnp.stack(vs_l), jnp.stack(cs_l), jnp.stack(fs_l))
```

```python
import functools
import math

import jax
import jax.numpy as jnp
from jax import lax
from jax.experimental import pallas as pl
from jax.experimental.pallas import tpu as pltpu

F32 = jnp.float32
BF16 = jnp.bfloat16

N_HEADS = 8
HEAD_DIM = 64
QK_DIM = 2 * HEAD_DIM
V_DIM = 2 * HEAD_DIM
ROT_DIM = HEAD_DIM // 4
ROT_HALF = ROT_DIM // 2
ROPE_THETA = 500000.0
ATTN_W = N_HEADS * V_DIM
CONV_K = 3
NORM_EPS = 1e-6
SUBLN_EPS = 1e-5
ATTN_SCALE = HEAD_DIM ** -0.5
PAGE_SIZE = 128

LANES = 128
SUBLANES = 8
VMEM_LIMIT_BYTES = 56 * 1024 * 1024
NEG_INF = float("-inf")


def _lambda_init(layer):
    return 0.8 - 0.6 * math.exp(-0.3 * layer)


def _params(n_axes):
    return pltpu.CompilerParams(dimension_semantics=("arbitrary",) * n_axes,
                                vmem_limit_bytes=VMEM_LIMIT_BYTES)


def _sigmoid(x):
    return 1.0 / (1.0 + jnp.exp(-x))


def _bdot(a, b):
    return jnp.dot(a, b, preferred_element_type=F32)


def _pick(n, candidates):
    for c in candidates:
        if n % c == 0:
            return c
    return n


def _ada_kernel(c_ref, w_ref, b_ref, o_ref):
    c = c_ref[...]
    s = (c * _sigmoid(c)).astype(BF16)
    o_ref[...] = _bdot(s, w_ref[...].astype(BF16)) + b_ref[...]


def _ada(c_rows, w_ada, b_ada):
    m, d = c_rows.shape
    n = w_ada.shape[1]
    tn = _pick(n, (1024, 512, 256, 128))
    return pl.pallas_call(
        _ada_kernel,
        grid=(n // tn,),
        in_specs=[pl.BlockSpec((m, d), lambda j: (0, 0)),
                  pl.BlockSpec((d, tn), lambda j: (0, j)),
                  pl.BlockSpec((1, tn), lambda j: (0, j))],
        out_specs=pl.BlockSpec((m, tn), lambda j: (0, j)),
        out_shape=jax.ShapeDtypeStruct((m, n), F32),
        compiler_params=_params(1),
        name="ada_mod",
    )(c_rows, w_ada, b_ada.reshape(1, n))


def _modulated_rmsnorm(x, g, sc, sh):
    ms = jnp.mean(x * x, axis=-1, keepdims=True)
    y = x * lax.rsqrt(ms + NORM_EPS)
    return y * g * (1.0 + sc) + sh


def _rope_tile(r, cos, sa, sb):
    outs = []
    for hh in range(r.shape[1] // LANES):
        xh = r[:, hh * LANES:(hh + 1) * LANES]
        outs.append(xh * cos + pltpu.roll(xh, ROT_HALF, 1) * sa
                    + pltpu.roll(xh, LANES - ROT_HALF, 1) * sb)
    return jnp.concatenate(outs, axis=1) if len(outs) > 1 else outs[0]


def _qkv_kernel(x_ref, g_ref, sc_ref, sh_ref, w_ref, cos_ref, sa_ref, sb_ref,
                h_ref, q_ref, kf_ref, kb_ref, vf_ref, vb_ref, *, nq):
    j = pl.program_id(1)

    @pl.when(j == 0)
    def _():
        h = _modulated_rmsnorm(x_ref[...], g_ref[...], sc_ref[...], sh_ref[...])
        h_ref[...] = h.astype(BF16)

    res = _bdot(h_ref[...], w_ref[...].astype(BF16))

    @pl.when(j < nq)
    def _():
        r = _rope_tile(res, cos_ref[...], sa_ref[...], sb_ref[...])
        q_ref[...] = (r * ATTN_SCALE).astype(BF16)

    @pl.when((j >= nq) & (j < 2 * nq))
    def _():
        r = _rope_tile(res, cos_ref[...], sa_ref[...], sb_ref[...])
        kf_ref[...] = r
        kb_ref[...] = r.astype(BF16)

    @pl.when(j >= 2 * nq)
    def _():
        vf_ref[...] = res
        vb_ref[...] = res.astype(BF16)


def _qkv(x, g, mod, w_in, cos, sa, sb, tm):
    m, d = x.shape
    mb = mod.shape[0]
    w = N_HEADS * QK_DIM
    tn = 512
    nq = w // tn
    row = lambda i, j: (i, 0)
    out_col = lambda lo: (lambda i, j: (i, jnp.clip(j - lo, 0, nq - 1)))
    return pl.pallas_call(
        functools.partial(_qkv_kernel, nq=nq),
        grid=(m // tm, 3 * nq),
        in_specs=[pl.BlockSpec((tm, d), row, pipeline_mode=pl.Buffered(1)),
                  pl.BlockSpec((1, d), lambda i, j: (0, 0)),
                  pl.BlockSpec((mb, d), lambda i, j: (0, 1)),
                  pl.BlockSpec((mb, d), lambda i, j: (0, 0)),
                  pl.BlockSpec((d, tn), lambda i, j: (0, j)),
                  pl.BlockSpec((tm, LANES), row),
                  pl.BlockSpec((tm, LANES), row),
                  pl.BlockSpec((tm, LANES), row)],
        out_specs=[pl.BlockSpec((tm, d), row),
                   pl.BlockSpec((tm, tn), out_col(0)),
                   pl.BlockSpec((tm, tn), out_col(nq)),
                   pl.BlockSpec((tm, tn), out_col(nq)),
                   pl.BlockSpec((tm, tn), out_col(2 * nq)),
                   pl.BlockSpec((tm, tn), out_col(2 * nq))],
        out_shape=[jax.ShapeDtypeStruct((m, d), BF16),
                   jax.ShapeDtypeStruct((m, w), BF16),
                   jax.ShapeDtypeStruct((m, w), F32),
                   jax.ShapeDtypeStruct((m, w), BF16),
                   jax.ShapeDtypeStruct((m, w), F32),
                   jax.ShapeDtypeStruct((m, w), BF16)],
        compiler_params=_params(2),
        name="qkv_proj",
    )(x, g, mod, mod, w_in, cos, sa, sb)


def _conv_rows(z, prev8, w_ref):
    tm = z.shape[0]
    zz = jnp.concatenate([prev8, z], axis=0)
    z1 = zz[SUBLANES - 1:SUBLANES - 1 + tm]
    z2 = zz[SUBLANES - 2:SUBLANES - 2 + tm]
    return z2 * w_ref[0:1, :] + z1 * w_ref[1:2, :] + z * w_ref[2:3, :]


def _conv_state(p0, p1, z, w_ref):
    return p0 * w_ref[0:1, :] + p1 * w_ref[1:2, :] + z * w_ref[2:3, :]


def _sconv_seq_kernel(h_ref, wb_ref, wc_ref, wx_ref, cw_ref, u_ref, tail_ref, carry_ref):
    i, j = pl.program_id(0), pl.program_id(1)
    h = h_ref[...]
    cb = _bdot(h, wb_ref[...].astype(BF16))
    z = _bdot(h, wc_ref[...].astype(BF16)) * _bdot(h, wx_ref[...].astype(BF16))

    @pl.when(i == 0)
    def _():
        carry_ref[j] = jnp.zeros(carry_ref.shape[1:], F32)

    u_ref[...] = (cb * _conv_rows(z, carry_ref[j], cw_ref)).astype(BF16)
    last8 = z[z.shape[0] - SUBLANES:]
    carry_ref[j] = last8
    tail_ref[0] = last8


def _sconv_state_kernel(h_ref, wb_ref, wc_ref, wx_ref, cw_ref, p0_ref, p1_ref, u_ref, z_ref):
    h = h_ref[...]
    cb = _bdot(h, wb_ref[...].astype(BF16))
    z = _bdot(h, wc_ref[...].astype(BF16)) * _bdot(h, wx_ref[...].astype(BF16))
    u_ref[...] = (cb * _conv_state(p0_ref[...], p1_ref[...], z, cw_ref)).astype(BF16)
    z_ref[...] = z


def _sconv(h, w_in, conv_w, col0, tm, state=None):
    m, d = h.shape
    cw = conv_w.shape[1]
    tn = 512
    nj = cw // tn
    wspec = lambda k: pl.BlockSpec((d, tn), lambda i, j: (0, (col0 + k * cw) // tn + j))
    tile = pl.BlockSpec((tm, tn), lambda i, j: (i, j))
    common = [pl.BlockSpec((tm, d), lambda i, j: (i, 0)), wspec(0), wspec(1), wspec(2),
              pl.BlockSpec((CONV_K, tn), lambda i, j: (0, j))]
    if state is None:
        return pl.pallas_call(
            _sconv_seq_kernel,
            grid=(m // tm, nj),
            in_specs=common,
            out_specs=[tile, pl.BlockSpec((1, SUBLANES, tn), lambda i, j: (i, 0, j))],
            out_shape=[jax.ShapeDtypeStruct((m, cw), BF16),
                       jax.ShapeDtypeStruct((m // tm, SUBLANES, cw), F32)],
            scratch_shapes=[pltpu.VMEM((nj, SUBLANES, tn), F32)],
            compiler_params=_params(2),
            name="sconv_seq",
        )(h, w_in, w_in, w_in, conv_w)
    p0, p1 = state
    return pl.pallas_call(
        _sconv_state_kernel,
        grid=(m // tm, nj),
        in_specs=common + [tile, tile],
        out_specs=[tile, tile],
        out_shape=[jax.ShapeDtypeStruct((m, cw), BF16), jax.ShapeDtypeStruct((m, cw), F32)],
        compiler_params=_params(2),
        name="sconv_state",
    )(h, w_in, w_in, w_in, conv_w, p0, p1)


def _lambda_full(lq1_ref, lk1_ref, lq2_ref, lk2_ref, lam_init):
    a = jnp.sum(lq1_ref[...] * lk1_ref[...], axis=-1, keepdims=True)
    b = jnp.sum(lq2_ref[...] * lk2_ref[...], axis=-1, keepdims=True)
    return jnp.exp(a) - jnp.exp(b) + lam_init


def _attn_kernel(lq1_ref, lk1_ref, lq2_ref, lk2_ref, g_ref, q_ref, k_ref, v_ref, o_ref,
                 qs_ref, m_ref, l_ref, acc_ref, *, bq, lam_init):
    qi = pl.program_id(1)
    q = q_ref[...]
    lane = lax.broadcasted_iota(jnp.int32, q.shape, 1)
    zero = jnp.zeros_like(q)
    qs_ref[0:bq] = jnp.where(lane < HEAD_DIM, q, zero)
    qs_ref[bq:2 * bq] = jnp.where(lane >= HEAD_DIM, q, zero)
    m_ref[...] = jnp.full(m_ref.shape, NEG_INF, F32)
    l_ref[...] = jnp.zeros(l_ref.shape, F32)
    acc_ref[...] = jnp.zeros(acc_ref.shape, F32)

    def chunk(j, masked):
        off = pl.multiple_of(j * bq, bq)
        kc = k_ref[pl.ds(off, bq), :]
        vc = v_ref[pl.ds(off, bq), :]
        s = lax.dot_general(qs_ref[...], kc, (((1,), (1,)), ((), ())),
                            preferred_element_type=F32)
        if masked:
            row = lax.broadcasted_iota(jnp.int32, s.shape, 0)
            col = lax.broadcasted_iota(jnp.int32, s.shape, 1)
            row = jnp.where(row >= bq, row - bq, row)
            s = jnp.where(col <= row, s, NEG_INF)
        m_prev = m_ref[...]
        m_new = jnp.maximum(m_prev, jnp.max(s, axis=-1, keepdims=True))
        alpha = jnp.exp(m_prev - m_new)
        p = jnp.exp(s - m_new)
        l_ref[...] = alpha * l_ref[...] + jnp.sum(p, axis=-1, keepdims=True)
        acc_ref[...] = alpha * acc_ref[...] + _bdot(p.astype(BF16), vc)
        m_ref[...] = m_new

    def body(j, carry):
        chunk(j, False)
        return carry

    lax.fori_loop(0, qi, body, 0)
    chunk(qi, True)

    lam = _lambda_full(lq1_ref, lk1_ref, lq2_ref, lk2_ref, lam_init)
    o = acc_ref[...] / l_ref[...]
    dlt = o[0:bq] - lam * o[bq:2 * bq]
    ms = jnp.mean(dlt * dlt, axis=-1, keepdims=True)
    y = dlt * lax.rsqrt(ms + SUBLN_EPS) * g_ref[...] * (1.0 - lam_init)
    o_ref[...] = y.astype(BF16)


def _prompt_attention(q, k, v, lams, subln_g, lam_init):
    t = q.shape[0]
    bq = _pick(t, (512, 256, 128))
    vec = lambda n: pl.BlockSpec((1, n), lambda h, i: (0, 0))
    return pl.pallas_call(
        functools.partial(_attn_kernel, bq=bq, lam_init=lam_init),
        grid=(N_HEADS, t // bq),
        in_specs=[vec(HEAD_DIM)] * 4 + [vec(V_DIM),
                  pl.BlockSpec((bq, QK_DIM), lambda h, i: (i, h)),
                  pl.BlockSpec((t, QK_DIM), lambda h, i: (0, h)),
                  pl.BlockSpec((t, V_DIM), lambda h, i: (0, h))],
        out_specs=pl.BlockSpec((bq, V_DIM), lambda h, i: (i, h)),
        out_shape=jax.ShapeDtypeStruct((t, ATTN_W), BF16),
        scratch_shapes=[pltpu.VMEM((2 * bq, QK_DIM), BF16),
                        pltpu.VMEM((2 * bq, 1), F32),
                        pltpu.VMEM((2 * bq, 1), F32),
                        pltpu.VMEM((2 * bq, V_DIM), F32)],
        compiler_params=_params(2),
        name="prompt_attn",
    )(*lams, subln_g, q, k, v)


N_MAPS = 2 * N_HEADS


def _sattn_kernel(pt_ref, lq1_ref, lk1_ref, lq2_ref, lk2_ref, g_ref, q_ref, kn_ref, vn_ref, *rest,
                  pg, lam_init):
    k_refs, v_refs = rest[:pg], rest[pg:2 * pg]
    o_ref, wq_ref, m_ref, l_ref, acc_ref = rest[2 * pg:]
    step = pl.program_id(1)
    width = wq_ref.shape[1]
    row = lax.broadcasted_iota(jnp.int32, (N_MAPS, width), 0)
    col = lax.broadcasted_iota(jnp.int32, (N_MAPS, width), 1)

    @pl.when(step == 0)
    def _():
        qrow = jnp.broadcast_to(q_ref[0].astype(F32), (N_MAPS, width))
        wq_ref[...] = jnp.where(row == col // HEAD_DIM, qrow, 0.0).astype(BF16)
        m_ref[...] = jnp.full(m_ref.shape, NEG_INF, F32)
        l_ref[...] = jnp.zeros(l_ref.shape, F32)
        acc_ref[...] = jnp.zeros(acc_ref.shape, F32)

    wq = wq_ref[...]
    s = jnp.concatenate(
        [lax.dot_general(wq, k_refs[p][0].astype(BF16), (((1,), (1,)), ((), ())),
                         preferred_element_type=F32) for p in range(pg)], axis=1)
    m_prev = m_ref[...]
    m_new = jnp.maximum(m_prev, jnp.max(s, axis=-1, keepdims=True))
    alpha = jnp.exp(m_prev - m_new)
    p_all = jnp.exp(s - m_new)
    l_ref[...] = alpha * l_ref[...] + jnp.sum(p_all, axis=-1, keepdims=True)
    pb = p_all.astype(BF16)
    pv = _bdot(pb[:, 0:PAGE_SIZE], v_refs[0][0].astype(BF16))
    for p in range(1, pg):
        pv = pv + _bdot(pb[:, p * PAGE_SIZE:(p + 1) * PAGE_SIZE], v_refs[p][0].astype(BF16))
    acc_ref[...] = alpha * acc_ref[...] + pv
    m_ref[...] = m_new

    @pl.when(step == pl.num_programs(1) - 1)
    def _():
        kn = kn_ref[0].astype(F32)
        vn = vn_ref[0].astype(F32)
        s_new = jnp.sum(wq.astype(F32) * kn, axis=-1, keepdims=True)
        m_old = m_ref[...]
        m_fin = jnp.maximum(m_old, s_new)
        a = jnp.exp(m_old - m_fin)
        p_new = jnp.exp(s_new - m_fin)
        l_fin = a * l_ref[...] + p_new
        acc = a * acc_ref[...] + p_new * vn
        o = acc / l_fin
        lam = _lambda_full(lq1_ref, lk1_ref, lq2_ref, lk2_ref, lam_init)
        dlt = o - lam * pltpu.roll(o, N_MAPS - 1, 0)
        keep = (row % 2 == 0) & (col // V_DIM == row // 2)
        dlt = jnp.where(keep, dlt, 0.0)
        ms = jnp.sum(dlt * dlt, axis=-1, keepdims=True) / V_DIM
        y = dlt * lax.rsqrt(ms + SUBLN_EPS)
        y = jnp.sum(y, axis=0, keepdims=True) * g_ref[...] * (1.0 - lam_init)
        o_ref[0] = y


def _sample_attention(q, k_new, v_new, cache_k, cache_v, page_table, lams, subln_g, lam_init):
    b, n_pages = page_table.shape
    width = q.shape[-1]
    pg = _pick(n_pages, (8, 4, 2, 1))
    g_wide = jnp.tile(subln_g, (1, N_HEADS))
    vec = lambda n: pl.BlockSpec((1, n), lambda i, s, pt: (0, 0))
    rowspec = pl.BlockSpec((1, 1, width), lambda i, s, pt: (i, 0, 0))
    page = lambda p: pl.BlockSpec((1, PAGE_SIZE, width), lambda i, s, pt: (pt[i, s * pg + p], 0, 0))
    grid_spec = pltpu.PrefetchScalarGridSpec(
        num_scalar_prefetch=1,
        grid=(b, n_pages // pg),
        in_specs=[vec(HEAD_DIM)] * 4 + [vec(width), rowspec, rowspec, rowspec]
                 + [page(p) for p in range(pg)] + [page(p) for p in range(pg)],
        out_specs=rowspec,
        scratch_shapes=[pltpu.VMEM((N_MAPS, width), BF16),
                        pltpu.VMEM((N_MAPS, 1), F32),
                        pltpu.VMEM((N_MAPS, 1), F32),
                        pltpu.VMEM((N_MAPS, width), F32)],
    )
    return pl.pallas_call(
        functools.partial(_sattn_kernel, pg=pg, lam_init=lam_init),
        grid_spec=grid_spec,
        out_shape=jax.ShapeDtypeStruct((b, 1, width), F32),
        compiler_params=_params(2),
        name="sample_attn",
    )(page_table, *lams, g_wide, q, k_new, v_new, *([cache_k] * pg), *([cache_v] * pg))


def _merge_kernel(h_ref, o_ref, u_ref, wga_ref, wgc_ref, wao_ref, wco_ref, out_ref):
    h = h_ref[...]
    ga = _bdot(h, wga_ref[...].astype(BF16))
    ya = _bdot(o_ref[...].astype(BF16), wao_ref[...].astype(BF16))
    acc = _sigmoid(ga) * ya
    gc = _bdot(h, wgc_ref[...].astype(BF16))
    yc = _bdot(u_ref[...], wco_ref[...].astype(BF16))
    out_ref[...] = (acc + _sigmoid(gc) * yc).astype(BF16)


def _merge(h, o, u, w_in, w_attn_out, w_conv_out, col_ga, col_gc, tm):
    m, d = h.shape
    tn = 256
    row = lambda w: pl.BlockSpec((tm, w), lambda i, j: (i, 0))
    return pl.pallas_call(
        _merge_kernel,
        grid=(m // tm, d // tn),
        in_specs=[row(d), row(o.shape[1]), row(u.shape[1]),
                  pl.BlockSpec((d, tn), lambda i, j: (0, col_ga // tn + j)),
                  pl.BlockSpec((d, tn), lambda i, j: (0, col_gc // tn + j)),
                  pl.BlockSpec((w_attn_out.shape[0], tn), lambda i, j: (0, j)),
                  pl.BlockSpec((w_conv_out.shape[0], tn), lambda i, j: (0, j))],
        out_specs=pl.BlockSpec((tm, tn), lambda i, j: (i, j)),
        out_shape=jax.ShapeDtypeStruct((m, d), BF16),
        compiler_params=_params(2),
        name="merge",
    )(h, o, u, w_in, w_in, w_attn_out, w_conv_out)


def _resid_kernel(a_ref, w_ref, x_ref, g_ref, o_ref):
    o_ref[...] = x_ref[...] + g_ref[...] * _bdot(a_ref[...], w_ref[...].astype(BF16))


def _resid_proj(a, w, x, mod, gate_col, tm):
    m, d = x.shape
    k = a.shape[1]
    mb = mod.shape[0]
    tn = 512
    per = d // tn
    return pl.pallas_call(
        _resid_kernel,
        grid=(m // tm, d // tn),
        in_specs=[pl.BlockSpec((tm, k), lambda i, j: (i, 0)),
                  pl.BlockSpec((k, tn), lambda i, j: (0, j)),
                  pl.BlockSpec((tm, tn), lambda i, j: (i, j)),
                  pl.BlockSpec((mb, tn), lambda i, j: (0, gate_col * per + j))],
        out_specs=pl.BlockSpec((tm, tn), lambda i, j: (i, j)),
        out_shape=jax.ShapeDtypeStruct((m, d), F32),
        compiler_params=_params(2),
        name="resid_proj",
    )(a, w, x, mod)


def _ffn_up_common(x_ref, g_ref, sc_ref, sh_ref, wg_ref, wv_ref, h_ref):
    @pl.when(pl.program_id(1) == 0)
    def _():
        h = _modulated_rmsnorm(x_ref[...], g_ref[...], sc_ref[...], sh_ref[...])
        h_ref[...] = h.astype(BF16)

    h = h_ref[...]
    return _bdot(h, wg_ref[...].astype(BF16)), _bdot(h, wv_ref[...].astype(BF16))


def _ffn_up_seq_kernel(x_ref, g_ref, sc_ref, sh_ref, wg_ref, wv_ref, cwg_ref, cwv_ref,
                       bg_ref, bv_ref, act_ref, tg_ref, tv_ref, h_ref, carry_ref):
    i, j = pl.program_id(0), pl.program_id(1)
    ug, uv = _ffn_up_common(x_ref, g_ref, sc_ref, sh_ref, wg_ref, wv_ref, h_ref)

    @pl.when(i == 0)
    def _():
        carry_ref[j] = jnp.zeros(carry_ref.shape[1:], F32)

    gate = _conv_rows(ug, carry_ref[j, 0], cwg_ref) + bg_ref[...]
    val = _conv_rows(uv, carry_ref[j, 1], cwv_ref) + bv_ref[...]
    act_ref[...] = (gate * _sigmoid(gate) * val).astype(BF16)
    tm = ug.shape[0]
    carry_ref[j, 0] = ug[tm - SUBLANES:]
    carry_ref[j, 1] = uv[tm - SUBLANES:]
    tg_ref[0] = ug[tm - SUBLANES:]
    tv_ref[0] = uv[tm - SUBLANES:]


def _ffn_up_state_kernel(x_ref, g_ref, sc_ref, sh_ref, wg_ref, wv_ref, cwg_ref, cwv_ref,
                         bg_ref, bv_ref, pg0_ref, pg1_ref, pv0_ref, pv1_ref,
                         act_ref, ug_ref, uv_ref, h_ref):
    ug, uv = _ffn_up_common(x_ref, g_ref, sc_ref, sh_ref, wg_ref, wv_ref, h_ref)
    gate = _conv_state(pg0_ref[...], pg1_ref[...], ug, cwg_ref) + bg_ref[...]
    val = _conv_state(pv0_ref[...], pv1_ref[...], uv, cwv_ref) + bv_ref[...]
    act_ref[...] = (gate * _sigmoid(gate) * val).astype(BF16)
    ug_ref[...] = ug
    uv_ref[...] = uv


def _ffn_up(x, g, mod, w_up, conv_w, conv_b, tm, state=None):
    m, d = x.shape
    mb = mod.shape[0]
    f = w_up.shape[1] // 2
    tn = 256
    nj = f // tn
    row = lambda i, j: (i, 0)
    lo = lambda i, j: (0, j)
    hi = lambda i, j: (0, nj + j)
    tile = pl.BlockSpec((tm, tn), lambda i, j: (i, j))
    tile_hi = pl.BlockSpec((tm, tn), lambda i, j: (i, nj + j))
    conv_b = conv_b.reshape(1, 2 * f)
    common = [pl.BlockSpec((tm, d), row, pipeline_mode=pl.Buffered(1)),
              pl.BlockSpec((1, d), lambda i, j: (0, 0)),
              pl.BlockSpec((mb, d), lambda i, j: (0, 4)),
              pl.BlockSpec((mb, d), lambda i, j: (0, 3)),
              pl.BlockSpec((d, tn), lo), pl.BlockSpec((d, tn), hi),
              pl.BlockSpec((CONV_K, tn), lo), pl.BlockSpec((CONV_K, tn), hi),
              pl.BlockSpec((1, tn), lo), pl.BlockSpec((1, tn), hi)]
    args = (x, g, mod, mod, w_up, w_up, conv_w, conv_w, conv_b, conv_b)
    if state is None:
        tail = pl.BlockSpec((1, SUBLANES, tn), lambda i, j: (i, 0, j))
        return pl.pallas_call(
            _ffn_up_seq_kernel,
            grid=(m // tm, nj),
            in_specs=common,
            out_specs=[tile, tail, tail],
            out_shape=[jax.ShapeDtypeStruct((m, f), BF16),
                       jax.ShapeDtypeStruct((m // tm, SUBLANES, f), F32),
                       jax.ShapeDtypeStruct((m // tm, SUBLANES, f), F32)],
            scratch_shapes=[pltpu.VMEM((tm, d), BF16),
                            pltpu.VMEM((nj, 2, SUBLANES, tn), F32)],
            compiler_params=_params(2),
            name="ffn_up_seq",
        )(*args)
    p0, p1 = state
    return pl.pallas_call(
        _ffn_up_state_kernel,
        grid=(m // tm, nj),
        in_specs=common + [tile, tile, tile_hi, tile_hi],
        out_specs=[tile, tile, tile],
        out_shape=[jax.ShapeDtypeStruct((m, f), BF16),
                   jax.ShapeDtypeStruct((m, f), F32),
                   jax.ShapeDtypeStruct((m, f), F32)],
        scratch_shapes=[pltpu.VMEM((tm, d), BF16)],
        compiler_params=_params(2),
        name="ffn_up_state",
    )(*args, p0, p1, p0, p1)


def _ffn_down_kernel(a_ref, w_ref, x_ref, g2_ref, fg_ref, y_ref, acc_ref):
    kk = pl.program_id(1)

    @pl.when(kk == 0)
    def _():
        acc_ref[...] = jnp.zeros(acc_ref.shape, F32)

    acc_ref[...] += _bdot(a_ref[...], w_ref[...].astype(BF16))

    @pl.when(kk == pl.num_programs(1) - 1)
    def _():
        x = x_ref[...] + g2_ref[...] * acc_ref[...]
        ms = jnp.mean(x * x, axis=-1, keepdims=True)
        y_ref[...] = x * lax.rsqrt(ms + NORM_EPS) * fg_ref[...]


def _ffn_down(act, w_down, x, mod, final_g, tm):
    m, d = x.shape
    f = act.shape[1]
    mb = mod.shape[0]
    tk = _pick(f, (512, 256, 128))
    return pl.pallas_call(
        _ffn_down_kernel,
        grid=(m // tm, f // tk),
        in_specs=[pl.BlockSpec((tm, tk), lambda i, k: (i, k)),
                  pl.BlockSpec((tk, d), lambda i, k: (k, 0)),
                  pl.BlockSpec((tm, d), lambda i, k: (i, 0)),
                  pl.BlockSpec((mb, d), lambda i, k: (0, 5)),
                  pl.BlockSpec((1, d), lambda i, k: (0, 0))],
        out_specs=pl.BlockSpec((tm, d), lambda i, k: (i, 0)),
        out_shape=jax.ShapeDtypeStruct((m, d), F32),
        scratch_shapes=[pltpu.VMEM((tm, d), F32)],
        compiler_params=_params(2),
        name="ffn_down",
    )(act, w_down, x, mod, final_g)


def _rope_tables(pos):
    inv_freq = ROPE_THETA ** (-jnp.arange(0, ROT_DIM, 2, dtype=F32) / ROT_DIM)
    ang = pos[:, None] * inv_freq[None, :]
    cos, sin = jnp.cos(ang), jnp.sin(ang)
    d = jnp.arange(LANES) % HEAD_DIM
    idx = d % ROT_HALF
    cos_t = jnp.where(d[None, :] < ROT_DIM, cos[:, idx], 1.0)
    sa_t = jnp.where((d[None, :] >= ROT_HALF) & (d[None, :] < ROT_DIM), sin[:, idx], 0.0)
    sb_t = jnp.where(d[None, :] < ROT_HALF, -sin[:, idx], 0.0)
    return cos_t.astype(F32), sa_t.astype(F32), sb_t.astype(F32)


def kernel(x_prompt, x_sample, cache_k, cache_v, state_conv, state_ffn, page_table, c_prompt, c_sample, w_ada, b_ada, norm1_g, w_in, lambda_q1, lambda_k1, lambda_q2, lambda_k2, subln_g, w_attn_out, conv_w, w_conv_out, w_o, norm2_g, w_up, ffn_conv_w, ffn_conv_b, w_down, final_g):
    depth = w_ada.shape[0]
    bp, seq, d = x_prompt.shape
    bs, dec_seq, _ = x_sample.shape
    assert bp == 1 and dec_seq == 1, "one prompt sequence and one new token per sample sequence"
    assert depth == 1, "the final norm is fused into the ConvFFN-down kernel of the only layer"
    n_pool = cache_k.shape[1]
    past_len = page_table.shape[1] * PAGE_SIZE
    qk_w = N_HEADS * QK_DIM
    cw = conv_w.shape[-1]
    f = w_down.shape[1]
    col_conv = 2 * qk_w + ATTN_W
    col_ga = col_conv + 3 * cw
    col_gc = col_ga + d
    tm_p = _pick(seq, (1024, 512, 256, 128))
    tm_s = bs

    xp = x_prompt.reshape(seq, d)
    xs = x_sample.reshape(bs, d)
    pad = (-(bs + bp)) % 16
    c_rows = jnp.concatenate([c_sample, c_prompt, jnp.zeros((pad, d), F32)], axis=0)
    rope_p = _rope_tables(jnp.arange(seq, dtype=F32))
    rope_s = _rope_tables(jnp.full((bs,), past_len, dtype=F32))
    fin_g = final_g.reshape(1, d)

    outs = [[] for _ in range(8)]
    for li in range(depth):
        lam_init = _lambda_init(li)
        mod = _ada(c_rows, w_ada[li], b_ada[li])
        mod_s, mod_p = mod[:bs], mod[bs:bs + bp]
        lams = tuple(a[li].reshape(1, HEAD_DIM) for a in (lambda_q1, lambda_k1, lambda_q2, lambda_k2))
        g1, g2 = norm1_g[li].reshape(1, d), norm2_g[li].reshape(1, d)
        sub_g = subln_g[li].reshape(1, V_DIM)

        h, q, kf, kb, vf, vb = _qkv(xp, g1, mod_p, w_in[li], *rope_p, tm_p)
        u, tail = _sconv(h, w_in[li], conv_w[li], col_conv, tm_p)
        o = _prompt_attention(q, kb, vb, lams, sub_g, lam_init)
        merged = _merge(h, o, u, w_in[li], w_attn_out[li], w_conv_out[li], col_ga, col_gc, tm_p)
        x1 = _resid_proj(merged, w_o[li], xp, mod_p, 2, tm_p)
        act, tg, tv = _ffn_up(x1, g2, mod_p, w_up[li], ffn_conv_w[li], ffn_conv_b[li], tm_p)
        xp = _ffn_down(act, w_down[li], x1, mod_p, fin_g, min(tm_p, 512))
        outs[0].append(kf.reshape(bp, seq, N_HEADS, QK_DIM))
        outs[1].append(vf.reshape(bp, seq, N_HEADS, V_DIM))
        outs[2].append(tail[-1, SUBLANES - (CONV_K - 1):].reshape(bp, CONV_K - 1, cw))
        outs[3].append(jnp.concatenate([tg[-1, SUBLANES - (CONV_K - 1):],
                                        tv[-1, SUBLANES - (CONV_K - 1):]], axis=-1).reshape(bp, CONV_K - 1, 2 * f))

        sc_prev, sf_prev = state_conv[li], state_ffn[li]
        h, q, kf, kb, vf, vb = _qkv(xs, g1, mod_s, w_in[li], *rope_s, tm_s)
        u, z = _sconv(h, w_in[li], conv_w[li], col_conv, tm_s, state=(sc_prev[:, 0], sc_prev[:, 1]))
        r3 = lambda a: a.reshape(bs, 1, a.shape[-1])
        o = _sample_attention(r3(q), r3(kb), r3(vb),
                              cache_k[li].reshape(n_pool, PAGE_SIZE, qk_w),
                              cache_v[li].reshape(n_pool, PAGE_SIZE, ATTN_W),
                              page_table, lams, sub_g, lam_init).reshape(bs, ATTN_W)
        merged = _merge(h, o, u, w_in[li], w_attn_out[li], w_conv_out[li], col_ga, col_gc, tm_s)
        x1 = _resid_proj(merged, w_o[li], xs, mod_s, 2, tm_s)
        act, ug, uv = _ffn_up(x1, g2, mod_s, w_up[li], ffn_conv_w[li], ffn_conv_b[li], tm_s,
                              state=(sf_prev[:, 0], sf_prev[:, 1]))
        xs = _ffn_down(act, w_down[li], x1, mod_s, fin_g, tm_s)
        outs[4].append(kf.reshape(bs, dec_seq, N_HEADS, QK_DIM))
        outs[5].append(vf.reshape(bs, dec_seq, N_HEADS, V_DIM))
        outs[6].append(jnp.stack([sc_prev[:, 1], z], axis=1))
        outs[7].append(jnp.stack([sf_prev[:, 1], jnp.concatenate([ug, uv], axis=-1)], axis=1))

    return (xp.reshape(bp, seq, d), xs.reshape(bs, dec_seq, d),
            *(jnp.stack(o) for o in outs))
```

```python
import functools
import math

import jax
import jax.numpy as jnp
from jax import lax
from jax.experimental import pallas as pl
from jax.experimental.pallas import tpu as pltpu

F32 = jnp.float32
BF16 = jnp.bfloat16

N_HEADS = 8
HEAD_DIM = 64
QK_DIM = 2 * HEAD_DIM
V_DIM = 2 * HEAD_DIM
ROT_DIM = HEAD_DIM // 4
ROT_HALF = ROT_DIM // 2
ROPE_THETA = 500000.0
ATTN_W = N_HEADS * V_DIM
CONV_K = 3
NORM_EPS = 1e-6
SUBLN_EPS = 1e-5
ATTN_SCALE = HEAD_DIM ** -0.5
PAGE_SIZE = 128
N_MAPS = 2 * N_HEADS

LANES = 128
SUBLANES = 8
MXU_DIM = 256
VMEM_LIMIT_BYTES = 56 * 1024 * 1024
ROW_SUB = MXU_DIM
CAST_ROWS = 256
NEG_INF = float("-inf")


def _lambda_init(layer):
    return 0.8 - 0.6 * math.exp(-0.3 * layer)


def _params(n_axes):
    return pltpu.CompilerParams(dimension_semantics=("arbitrary",) * n_axes,
                                vmem_limit_bytes=VMEM_LIMIT_BYTES)


def _sigmoid(x):
    return 1.0 / (1.0 + jnp.exp(-x))


def _bdot(a, b):
    return jnp.dot(a, b, preferred_element_type=F32)


def _pick(n, candidates):
    for c in candidates:
        if n % c == 0:
            return c
    return n


def _row_blocks(tm):
    return [slice(r, min(r + ROW_SUB, tm)) for r in range(0, tm, ROW_SUB)]


def _cast_weight(w_ref, wb_ref):
    k = w_ref.shape[0]
    for r in range(0, k, CAST_ROWS):
        rows = slice(r, min(r + CAST_ROWS, k))
        wb_ref[rows, :] = w_ref[rows, :].astype(BF16)


def _mod_rows(ref, rows):
    return ref[...] if ref.shape[0] == 1 else ref[rows, :]


def _ada_kernel(c_ref, w_ref, b_ref, o_ref, wb_ref):
    c = c_ref[...]
    s = (c * _sigmoid(c)).astype(BF16)
    _cast_weight(w_ref, wb_ref)
    o_ref[...] = _bdot(s, wb_ref[...]) + b_ref[...]


def _ada(c_rows, w_ada, b_ada):
    m, d = c_rows.shape
    n = w_ada.shape[1]
    tn = _pick(n, (1024, 512, 256, 128))
    return pl.pallas_call(
        _ada_kernel,
        grid=(n // tn,),
        in_specs=[pl.BlockSpec((m, d), lambda j: (0, 0)),
                  pl.BlockSpec((d, tn), lambda j: (0, j)),
                  pl.BlockSpec((1, tn), lambda j: (0, j))],
        out_specs=pl.BlockSpec((m, tn), lambda j: (0, j)),
        out_shape=jax.ShapeDtypeStruct((m, n), F32),
        scratch_shapes=[pltpu.VMEM((d, tn), BF16)],
        compiler_params=_params(1),
        name="ada_mod",
    )(c_rows, w_ada, b_ada.reshape(1, n))


def _modulated_rmsnorm(x, g, sc, sh):
    ms = jnp.mean(x * x, axis=-1, keepdims=True)
    y = x * lax.rsqrt(ms + NORM_EPS)
    return y * g * (1.0 + sc) + sh


def _norm_prologue(x_ref, g_ref, sc_ref, sh_ref, h_ref):
    for rows in _row_blocks(x_ref.shape[0]):
        h = _modulated_rmsnorm(x_ref[rows, :], g_ref[...], _mod_rows(sc_ref, rows), _mod_rows(sh_ref, rows))
        h_ref[rows, :] = h.astype(BF16)


def _rope_tile(r, cos, sa, sb):
    outs = []
    for hh in range(r.shape[1] // LANES):
        xh = r[:, hh * LANES:(hh + 1) * LANES]
        outs.append(xh * cos + pltpu.roll(xh, ROT_HALF, 1) * sa
                    + pltpu.roll(xh, LANES - ROT_HALF, 1) * sb)
    return jnp.concatenate(outs, axis=1) if len(outs) > 1 else outs[0]


def _qkv_kernel(x_ref, g_ref, sc_ref, sh_ref, w_ref, cos_ref, sa_ref, sb_ref,
                h_ref, q_ref, kf_ref, kb_ref, vf_ref, vb_ref, wb_ref, *, nq):
    j = pl.program_id(1)

    @pl.when(j == 0)
    def _():
        _norm_prologue(x_ref, g_ref, sc_ref, sh_ref, h_ref)

    _cast_weight(w_ref, wb_ref)

    def run(epilogue):
        for rows in _row_blocks(h_ref.shape[0]):
            epilogue(rows, _bdot(h_ref[rows, :], wb_ref[...]))

    def rope(rows, res):
        return _rope_tile(res, cos_ref[rows, :], sa_ref[rows, :], sb_ref[rows, :])

    @pl.when(j < nq)
    def _():
        def epilogue(rows, res):
            q_ref[rows, :] = (rope(rows, res) * ATTN_SCALE).astype(BF16)
        run(epilogue)

    @pl.when((j >= nq) & (j < 2 * nq))
    def _():
        def epilogue(rows, res):
            r = rope(rows, res)
            kf_ref[rows, :] = r
            kb_ref[rows, :] = r.astype(BF16)
        run(epilogue)

    @pl.when(j >= 2 * nq)
    def _():
        def epilogue(rows, res):
            vf_ref[rows, :] = res
            vb_ref[rows, :] = res.astype(BF16)
        run(epilogue)


def _qkv(x, g, mod, w_in, cos, sa, sb, tm):
    m, d = x.shape
    mb = mod.shape[0]
    w = N_HEADS * QK_DIM
    tn = 512
    nq = w // tn
    row = lambda i, j: (i, 0)
    out_col = lambda lo: (lambda i, j: (i, jnp.clip(j - lo, 0, nq - 1)))
    return pl.pallas_call(
        functools.partial(_qkv_kernel, nq=nq),
        grid=(m // tm, 3 * nq),
        in_specs=[pl.BlockSpec((tm, d), row, pipeline_mode=pl.Buffered(1)),
                  pl.BlockSpec((1, d), lambda i, j: (0, 0)),
                  pl.BlockSpec((mb, d), lambda i, j: (0, 1)),
                  pl.BlockSpec((mb, d), lambda i, j: (0, 0)),
                  pl.BlockSpec((d, tn), lambda i, j: (0, j)),
                  pl.BlockSpec((tm, LANES), row),
                  pl.BlockSpec((tm, LANES), row),
                  pl.BlockSpec((tm, LANES), row)],
        out_specs=[pl.BlockSpec((tm, d), row),
                   pl.BlockSpec((tm, tn), out_col(0)),
                   pl.BlockSpec((tm, tn), out_col(nq)),
                   pl.BlockSpec((tm, tn), out_col(nq)),
                   pl.BlockSpec((tm, tn), out_col(2 * nq)),
                   pl.BlockSpec((tm, tn), out_col(2 * nq))],
        out_shape=[jax.ShapeDtypeStruct((m, d), BF16),
                   jax.ShapeDtypeStruct((m, w), BF16),
                   jax.ShapeDtypeStruct((m, w), F32),
                   jax.ShapeDtypeStruct((m, w), BF16),
                   jax.ShapeDtypeStruct((m, w), F32),
                   jax.ShapeDtypeStruct((m, w), BF16)],
        scratch_shapes=[pltpu.VMEM((d, tn), BF16)],
        compiler_params=_params(2),
        name="qkv_proj",
    )(x, g, mod, mod, w_in, cos, sa, sb)


def _conv_rows(z, prev8, w_ref):
    n = z.shape[0]
    zz = jnp.concatenate([prev8, z], axis=0)
    z1 = zz[SUBLANES - 1:SUBLANES - 1 + n]
    z2 = zz[SUBLANES - 2:SUBLANES - 2 + n]
    return z2 * w_ref[0:1, :] + z1 * w_ref[1:2, :] + z * w_ref[2:3, :]


def _conv_state(p0, p1, z, w_ref):
    return p0 * w_ref[0:1, :] + p1 * w_ref[1:2, :] + z * w_ref[2:3, :]


def _last8(z):
    return z[z.shape[0] - SUBLANES:]


def _sconv_seq_kernel(h_ref, wb_ref, wc_ref, wx_ref, cw_ref, u_ref, tail_ref,
                      wbb_ref, wcb_ref, wxb_ref, carry_ref):
    i, j = pl.program_id(0), pl.program_id(1)
    _cast_weight(wb_ref, wbb_ref)
    _cast_weight(wc_ref, wcb_ref)
    _cast_weight(wx_ref, wxb_ref)

    @pl.when(i == 0)
    def _():
        carry_ref[j] = jnp.zeros(carry_ref.shape[1:], F32)

    prev = carry_ref[j]
    for rows in _row_blocks(h_ref.shape[0]):
        h = h_ref[rows, :]
        z = _bdot(h, wcb_ref[...]) * _bdot(h, wxb_ref[...])
        u_ref[rows, :] = (_bdot(h, wbb_ref[...]) * _conv_rows(z, prev, cw_ref)).astype(BF16)
        prev = _last8(z)
    carry_ref[j] = prev
    tail_ref[0] = prev


def _sconv_state_kernel(h_ref, wb_ref, wc_ref, wx_ref, cw_ref, p0_ref, p1_ref, u_ref, z_ref,
                        wbb_ref, wcb_ref, wxb_ref):
    _cast_weight(wb_ref, wbb_ref)
    _cast_weight(wc_ref, wcb_ref)
    _cast_weight(wx_ref, wxb_ref)
    h = h_ref[...]
    z = _bdot(h, wcb_ref[...]) * _bdot(h, wxb_ref[...])
    conv = _conv_state(p0_ref[...], p1_ref[...], z, cw_ref)
    u_ref[...] = (_bdot(h, wbb_ref[...]) * conv).astype(BF16)
    z_ref[...] = z


def _sconv(h, w_in, conv_w, col0, tm, state=None):
    m, d = h.shape
    cw = conv_w.shape[1]
    tn = 512
    nj = cw // tn
    wspec = lambda k: pl.BlockSpec((d, tn), lambda i, j: (0, (col0 + k * cw) // tn + j))
    tile = pl.BlockSpec((tm, tn), lambda i, j: (i, j))
    common = [pl.BlockSpec((tm, d), lambda i, j: (i, 0)), wspec(0), wspec(1), wspec(2),
              pl.BlockSpec((CONV_K, tn), lambda i, j: (0, j))]
    wscratch = [pltpu.VMEM((d, tn), BF16)] * 3
    if state is None:
        return pl.pallas_call(
            _sconv_seq_kernel,
            grid=(m // tm, nj),
            in_specs=common,
            out_specs=[tile, pl.BlockSpec((1, SUBLANES, tn), lambda i, j: (i, 0, j))],
            out_shape=[jax.ShapeDtypeStruct((m, cw), BF16),
                       jax.ShapeDtypeStruct((m // tm, SUBLANES, cw), F32)],
            scratch_shapes=wscratch + [pltpu.VMEM((nj, SUBLANES, tn), F32)],
            compiler_params=_params(2),
            name="sconv_seq",
        )(h, w_in, w_in, w_in, conv_w)
    p0, p1 = state
    return pl.pallas_call(
        _sconv_state_kernel,
        grid=(m // tm, nj),
        in_specs=common + [tile, tile],
        out_specs=[tile, tile],
        out_shape=[jax.ShapeDtypeStruct((m, cw), BF16), jax.ShapeDtypeStruct((m, cw), F32)],
        scratch_shapes=wscratch,
        compiler_params=_params(2),
        name="sconv_state",
    )(h, w_in, w_in, w_in, conv_w, p0, p1)


def _lambda_full(lq1_ref, lk1_ref, lq2_ref, lk2_ref, lam_init):
    a = jnp.sum(lq1_ref[...] * lk1_ref[...], axis=-1, keepdims=True)
    b = jnp.sum(lq2_ref[...] * lk2_ref[...], axis=-1, keepdims=True)
    return jnp.exp(a) - jnp.exp(b) + lam_init


def _attn_kernel(lq1_ref, lk1_ref, lq2_ref, lk2_ref, g_ref, q_ref, k_ref, v_ref, o_ref,
                 qst_ref, vt_ref, m_ref, l_ref, acc_ref, *, bq, lam_init):
    qi = pl.program_id(1)
    t = k_ref.shape[0]

    @pl.when(qi == 0)
    def _():
        for c in range(t // bq):
            blk = v_ref[c * bq:(c + 1) * bq, :].astype(F32)
            vt_ref[:, c * bq:(c + 1) * bq] = blk.T.astype(BF16)

    qt = q_ref[...].astype(F32).T
    dim = lax.broadcasted_iota(jnp.int32, qt.shape, 0)
    qst_ref[:, 0:bq] = jnp.where(dim < HEAD_DIM, qt, 0.0).astype(BF16)
    qst_ref[:, bq:2 * bq] = jnp.where(dim >= HEAD_DIM, qt, 0.0).astype(BF16)
    m_ref[...] = jnp.full(m_ref.shape, NEG_INF, F32)
    l_ref[...] = jnp.zeros(l_ref.shape, F32)
    acc_ref[...] = jnp.zeros(acc_ref.shape, F32)

    def chunk(j, masked):
        off = pl.multiple_of(j * bq, bq)
        st = _bdot(k_ref[pl.ds(off, bq), :], qst_ref[...])
        if masked:
            key = lax.broadcasted_iota(jnp.int32, st.shape, 0)
            qry = lax.broadcasted_iota(jnp.int32, st.shape, 1)
            qry = jnp.where(qry >= bq, qry - bq, qry)
            st = jnp.where(key <= qry, st, NEG_INF)
        m_prev = m_ref[...]
        m_new = jnp.maximum(m_prev, jnp.max(st, axis=0, keepdims=True))
        alpha = jnp.exp(m_prev - m_new)
        pt = jnp.exp(st - m_new)
        l_ref[...] = alpha * l_ref[...] + jnp.sum(pt, axis=0, keepdims=True)
        acc_ref[...] = alpha * acc_ref[...] + _bdot(vt_ref[:, pl.ds(off, bq)], pt.astype(BF16))
        m_ref[...] = m_new

    def body(j, carry):
        chunk(j, False)
        return carry

    lax.fori_loop(0, qi, body, 0)
    chunk(qi, True)

    lam = _lambda_full(lq1_ref, lk1_ref, lq2_ref, lk2_ref, lam_init)
    ot = acc_ref[...] / l_ref[...]
    dlt = ot[:, 0:bq] - lam * ot[:, bq:2 * bq]
    ms = jnp.mean(dlt * dlt, axis=0, keepdims=True)
    y = (dlt * lax.rsqrt(ms + SUBLN_EPS)).T * g_ref[...] * (1.0 - lam_init)
    o_ref[...] = y.astype(BF16)


def _prompt_attention(q, k, v, lams, subln_g, lam_init):
    t = q.shape[0]
    bq = _pick(t, (512, 256))
    vec = lambda n: pl.BlockSpec((1, n), lambda h, i: (0, 0))
    return pl.pallas_call(
        functools.partial(_attn_kernel, bq=bq, lam_init=lam_init),
        grid=(N_HEADS, t // bq),
        in_specs=[vec(HEAD_DIM)] * 4 + [vec(V_DIM),
                  pl.BlockSpec((bq, QK_DIM), lambda h, i: (i, h)),
                  pl.BlockSpec((t, QK_DIM), lambda h, i: (0, h)),
                  pl.BlockSpec((t, V_DIM), lambda h, i: (0, h))],
        out_specs=pl.BlockSpec((bq, V_DIM), lambda h, i: (i, h)),
        out_shape=jax.ShapeDtypeStruct((t, ATTN_W), BF16),
        scratch_shapes=[pltpu.VMEM((QK_DIM, 2 * bq), BF16),
                        pltpu.VMEM((V_DIM, t), BF16),
                        pltpu.VMEM((1, 2 * bq), F32),
                        pltpu.VMEM((1, 2 * bq), F32),
                        pltpu.VMEM((V_DIM, 2 * bq), F32)],
        compiler_params=_params(2),
        name="prompt_attn",
    )(*lams, subln_g, q, k, v)


def _sattn_kernel(pt_ref, lq1_ref, lk1_ref, lq2_ref, lk2_ref, g_ref, q_ref, kn_ref, vn_ref, *rest,
                  pg, lam_init):
    k_refs, v_refs = rest[:pg], rest[pg:2 * pg]
    o_ref, wq_ref, m_ref, l_ref, acc_ref = rest[2 * pg:]
    step = pl.program_id(1)
    width = wq_ref.shape[1]
    row = lax.broadcasted_iota(jnp.int32, (N_MAPS, width), 0)
    col = lax.broadcasted_iota(jnp.int32, (N_MAPS, width), 1)

    @pl.when(step == 0)
    def _():
        qrow = jnp.broadcast_to(q_ref[0].astype(F32), (N_MAPS, width))
        wq_ref[...] = jnp.where(row == col // HEAD_DIM, qrow, 0.0).astype(BF16)
        m_ref[...] = jnp.full(m_ref.shape, NEG_INF, F32)
        l_ref[...] = jnp.zeros(l_ref.shape, F32)
        acc_ref[...] = jnp.zeros(acc_ref.shape, F32)

    wq = wq_ref[...]
    s = jnp.concatenate(
        [lax.dot_general(wq, k_refs[p][0].astype(BF16), (((1,), (1,)), ((), ())),
                         preferred_element_type=F32) for p in range(pg)], axis=1)
    m_prev = m_ref[...]
    m_new = jnp.maximum(m_prev, jnp.max(s, axis=-1, keepdims=True))
    alpha = jnp.exp(m_prev - m_new)
    p_all = jnp.exp(s - m_new)
    l_ref[...] = alpha * l_ref[...] + jnp.sum(p_all, axis=-1, keepdims=True)
    pb = p_all.astype(BF16)
    pv = _bdot(pb[:, 0:PAGE_SIZE], v_refs[0][0].astype(BF16))
    for p in range(1, pg):
        pv = pv + _bdot(pb[:, p * PAGE_SIZE:(p + 1) * PAGE_SIZE], v_refs[p][0].astype(BF16))
    acc_ref[...] = alpha * acc_ref[...] + pv
    m_ref[...] = m_new

    @pl.when(step == pl.num_programs(1) - 1)
    def _():
        kn = kn_ref[0].astype(F32)
        vn = vn_ref[0].astype(F32)
        s_new = jnp.sum(wq.astype(F32) * kn, axis=-1, keepdims=True)
        m_old = m_ref[...]
        m_fin = jnp.maximum(m_old, s_new)
        a = jnp.exp(m_old - m_fin)
        p_new = jnp.exp(s_new - m_fin)
        l_fin = a * l_ref[...] + p_new
        acc = a * acc_ref[...] + p_new * vn
        o = acc / l_fin
        lam = _lambda_full(lq1_ref, lk1_ref, lq2_ref, lk2_ref, lam_init)
        dlt = o - lam * pltpu.roll(o, N_MAPS - 1, 0)
        keep = (row % 2 == 0) & (col // V_DIM == row // 2)
        dlt = jnp.where(keep, dlt, 0.0)
        ms = jnp.sum(dlt * dlt, axis=-1, keepdims=True) / V_DIM
        y = dlt * lax.rsqrt(ms + SUBLN_EPS)
        y = jnp.sum(y, axis=0, keepdims=True) * g_ref[...] * (1.0 - lam_init)
        o_ref[0] = y


def _sample_attention(q, k_new, v_new, cache_k, cache_v, page0, page_table, lams, subln_g, lam_init):
    b, n_pages = page_table.shape
    width = q.shape[-1]
    pg = _pick(n_pages, (8, 4, 2, 1))
    g_wide = jnp.tile(subln_g, (1, N_HEADS))
    vec = lambda n: pl.BlockSpec((1, n), lambda i, s, pt: (0, 0))
    rowspec = pl.BlockSpec((1, 1, width), lambda i, s, pt: (i, 0, 0))
    page = lambda p: pl.BlockSpec((1, PAGE_SIZE, width), lambda i, s, pt: (page0 + pt[i, s * pg + p], 0, 0))
    grid_spec = pltpu.PrefetchScalarGridSpec(
        num_scalar_prefetch=1,
        grid=(b, n_pages // pg),
        in_specs=[vec(HEAD_DIM)] * 4 + [vec(width), rowspec, rowspec, rowspec]
                 + [page(p) for p in range(pg)] + [page(p) for p in range(pg)],
        out_specs=rowspec,
        scratch_shapes=[pltpu.VMEM((N_MAPS, width), BF16),
                        pltpu.VMEM((N_MAPS, 1), F32),
                        pltpu.VMEM((N_MAPS, 1), F32),
                        pltpu.VMEM((N_MAPS, width), F32)],
    )
    return pl.pallas_call(
        functools.partial(_sattn_kernel, pg=pg, lam_init=lam_init),
        grid_spec=grid_spec,
        out_shape=jax.ShapeDtypeStruct((b, 1, width), F32),
        compiler_params=_params(2),
        name="sample_attn",
    )(page_table, *lams, g_wide, q, k_new, v_new, *([cache_k] * pg), *([cache_v] * pg))


def _merge_kernel(h_ref, o_ref, u_ref, wga_ref, wgc_ref, wao_ref, wco_ref, out_ref,
                  wgab_ref, wgcb_ref, waob_ref, wcob_ref):
    _cast_weight(wga_ref, wgab_ref)
    _cast_weight(wgc_ref, wgcb_ref)
    _cast_weight(wao_ref, waob_ref)
    _cast_weight(wco_ref, wcob_ref)
    for rows in _row_blocks(h_ref.shape[0]):
        h = h_ref[rows, :]
        att = _sigmoid(_bdot(h, wgab_ref[...])) * _bdot(o_ref[rows, :].astype(BF16), waob_ref[...])
        conv = _sigmoid(_bdot(h, wgcb_ref[...])) * _bdot(u_ref[rows, :], wcob_ref[...])
        out_ref[rows, :] = (att + conv).astype(BF16)


def _merge(h, o, u, w_in, w_attn_out, w_conv_out, col_ga, col_gc, tm):
    m, d = h.shape
    tn = 256
    ka, kc = w_attn_out.shape[0], w_conv_out.shape[0]
    row = lambda w: pl.BlockSpec((tm, w), lambda i, j: (i, 0))
    return pl.pallas_call(
        _merge_kernel,
        grid=(m // tm, d // tn),
        in_specs=[row(d), row(o.shape[1]), row(u.shape[1]),
                  pl.BlockSpec((d, tn), lambda i, j: (0, col_ga // tn + j)),
                  pl.BlockSpec((d, tn), lambda i, j: (0, col_gc // tn + j)),
                  pl.BlockSpec((ka, tn), lambda i, j: (0, j)),
                  pl.BlockSpec((kc, tn), lambda i, j: (0, j))],
        out_specs=pl.BlockSpec((tm, tn), lambda i, j: (i, j)),
        out_shape=jax.ShapeDtypeStruct((m, d), BF16),
        scratch_shapes=[pltpu.VMEM((d, tn), BF16), pltpu.VMEM((d, tn), BF16),
                        pltpu.VMEM((ka, tn), BF16), pltpu.VMEM((kc, tn), BF16)],
        compiler_params=_params(2),
        name="merge",
    )(h, o, u, w_in, w_in, w_attn_out, w_conv_out)


def _resid_kernel(a_ref, w_ref, x_ref, g_ref, o_ref, wb_ref):
    _cast_weight(w_ref, wb_ref)
    for rows in _row_blocks(a_ref.shape[0]):
        o_ref[rows, :] = x_ref[rows, :] + _mod_rows(g_ref, rows) * _bdot(a_ref[rows, :], wb_ref[...])


def _resid_proj(a, w, x, mod, gate_col, tm):
    m, d = x.shape
    k = a.shape[1]
    mb = mod.shape[0]
    tn = 512
    per = d // tn
    return pl.pallas_call(
        _resid_kernel,
        grid=(m // tm, d // tn),
        in_specs=[pl.BlockSpec((tm, k), lambda i, j: (i, 0)),
                  pl.BlockSpec((k, tn), lambda i, j: (0, j)),
                  pl.BlockSpec((tm, tn), lambda i, j: (i, j)),
                  pl.BlockSpec((mb, tn), lambda i, j: (0, gate_col * per + j))],
        out_specs=pl.BlockSpec((tm, tn), lambda i, j: (i, j)),
        out_shape=jax.ShapeDtypeStruct((m, d), F32),
        scratch_shapes=[pltpu.VMEM((k, tn), BF16)],
        compiler_params=_params(2),
        name="resid_proj",
    )(a, w, x, mod)


def _ffn_up_seq_kernel(x_ref, g_ref, sc_ref, sh_ref, wg_ref, wv_ref, cwg_ref, cwv_ref,
                       bg_ref, bv_ref, act_ref, tg_ref, tv_ref, h_ref, wgb_ref, wvb_ref, carry_ref):
    i, j = pl.program_id(0), pl.program_id(1)

    @pl.when(j == 0)
    def _():
        _norm_prologue(x_ref, g_ref, sc_ref, sh_ref, h_ref)

    _cast_weight(wg_ref, wgb_ref)
    _cast_weight(wv_ref, wvb_ref)

    @pl.when(i == 0)
    def _():
        carry_ref[j] = jnp.zeros(carry_ref.shape[1:], F32)

    prev_g, prev_v = carry_ref[j, 0], carry_ref[j, 1]
    for rows in _row_blocks(h_ref.shape[0]):
        h = h_ref[rows, :]
        ug, uv = _bdot(h, wgb_ref[...]), _bdot(h, wvb_ref[...])
        gate = _conv_rows(ug, prev_g, cwg_ref) + bg_ref[...]
        val = _conv_rows(uv, prev_v, cwv_ref) + bv_ref[...]
        act_ref[rows, :] = (gate * _sigmoid(gate) * val).astype(BF16)
        prev_g, prev_v = _last8(ug), _last8(uv)
    carry_ref[j, 0] = prev_g
    carry_ref[j, 1] = prev_v
    tg_ref[0] = prev_g
    tv_ref[0] = prev_v


def _ffn_up_state_kernel(x_ref, g_ref, sc_ref, sh_ref, wg_ref, wv_ref, cwg_ref, cwv_ref,
                         bg_ref, bv_ref, pg0_ref, pg1_ref, pv0_ref, pv1_ref,
                         act_ref, ug_ref, uv_ref, h_ref, wgb_ref, wvb_ref):
    @pl.when(pl.program_id(1) == 0)
    def _():
        _norm_prologue(x_ref, g_ref, sc_ref, sh_ref, h_ref)

    _cast_weight(wg_ref, wgb_ref)
    _cast_weight(wv_ref, wvb_ref)
    h = h_ref[...]
    ug, uv = _bdot(h, wgb_ref[...]), _bdot(h, wvb_ref[...])
    gate = _conv_state(pg0_ref[...], pg1_ref[...], ug, cwg_ref) + bg_ref[...]
    val = _conv_state(pv0_ref[...], pv1_ref[...], uv, cwv_ref) + bv_ref[...]
    act_ref[...] = (gate * _sigmoid(gate) * val).astype(BF16)
    ug_ref[...] = ug
    uv_ref[...] = uv


def _ffn_up(x, g, mod, w_up, conv_w, conv_b, tm, state=None):
    m, d = x.shape
    mb = mod.shape[0]
    f = w_up.shape[1] // 2
    tn = _pick(f, (512, 256))
    nj = f // tn
    row = lambda i, j: (i, 0)
    lo = lambda i, j: (0, j)
    hi = lambda i, j: (0, nj + j)
    tile = pl.BlockSpec((tm, tn), lambda i, j: (i, j))
    tile_hi = pl.BlockSpec((tm, tn), lambda i, j: (i, nj + j))
    conv_b = conv_b.reshape(1, 2 * f)
    common = [pl.BlockSpec((tm, d), row, pipeline_mode=pl.Buffered(1)),
              pl.BlockSpec((1, d), lambda i, j: (0, 0)),
              pl.BlockSpec((mb, d), lambda i, j: (0, 4)),
              pl.BlockSpec((mb, d), lambda i, j: (0, 3)),
              pl.BlockSpec((d, tn), lo), pl.BlockSpec((d, tn), hi),
              pl.BlockSpec((CONV_K, tn), lo), pl.BlockSpec((CONV_K, tn), hi),
              pl.BlockSpec((1, tn), lo), pl.BlockSpec((1, tn), hi)]
    args = (x, g, mod, mod, w_up, w_up, conv_w, conv_w, conv_b, conv_b)
    scratch = [pltpu.VMEM((tm, d), BF16), pltpu.VMEM((d, tn), BF16), pltpu.VMEM((d, tn), BF16)]
    if state is None:
        tail = pl.BlockSpec((1, SUBLANES, tn), lambda i, j: (i, 0, j))
        return pl.pallas_call(
            _ffn_up_seq_kernel,
            grid=(m // tm, nj),
            in_specs=common,
            out_specs=[tile, tail, tail],
            out_shape=[jax.ShapeDtypeStruct((m, f), BF16),
                       jax.ShapeDtypeStruct((m // tm, SUBLANES, f), F32),
                       jax.ShapeDtypeStruct((m // tm, SUBLANES, f), F32)],
            scratch_shapes=scratch + [pltpu.VMEM((nj, 2, SUBLANES, tn), F32)],
            compiler_params=_params(2),
            name="ffn_up_seq",
        )(*args)
    p0, p1 = state
    return pl.pallas_call(
        _ffn_up_state_kernel,
        grid=(m // tm, nj),
        in_specs=common + [tile, tile, tile_hi, tile_hi],
        out_specs=[tile, tile, tile],
        out_shape=[jax.ShapeDtypeStruct((m, f), BF16),
                   jax.ShapeDtypeStruct((m, f), F32),
                   jax.ShapeDtypeStruct((m, f), F32)],
        scratch_shapes=scratch,
        compiler_params=_params(2),
        name="ffn_up_state",
    )(*args, p0, p1, p0, p1)


def _ffn_down_kernel(a_ref, w_ref, x_ref, g2_ref, fg_ref, y_ref, wb_ref):
    kk = pl.program_id(1)
    last = pl.num_programs(1) - 1
    _cast_weight(w_ref, wb_ref)
    blocks = _row_blocks(a_ref.shape[0])

    @pl.when(kk == 0)
    def _():
        for rows in blocks:
            y_ref[rows, :] = _bdot(a_ref[rows, :], wb_ref[...])

    @pl.when((kk > 0) & (kk < last))
    def _():
        for rows in blocks:
            y_ref[rows, :] += _bdot(a_ref[rows, :], wb_ref[...])

    @pl.when(kk == last)
    def _():
        for rows in blocks:
            acc = y_ref[rows, :] + _bdot(a_ref[rows, :], wb_ref[...])
            x = x_ref[rows, :] + _mod_rows(g2_ref, rows) * acc
            ms = jnp.mean(x * x, axis=-1, keepdims=True)
            y_ref[rows, :] = x * lax.rsqrt(ms + NORM_EPS) * fg_ref[...]


def _ffn_down(act, w_down, x, mod, final_g, tm):
    m, d = x.shape
    f = act.shape[1]
    mb = mod.shape[0]
    tk = _pick(f, (512, 256, 128))
    assert f // tk >= 2, "the accumulate-in-output schedule needs at least two K steps"
    return pl.pallas_call(
        _ffn_down_kernel,
        grid=(m // tm, f // tk),
        in_specs=[pl.BlockSpec((tm, tk), lambda i, k: (i, k)),
                  pl.BlockSpec((tk, d), lambda i, k: (k, 0)),
                  pl.BlockSpec((tm, d), lambda i, k: (i, 0), pipeline_mode=pl.Buffered(1)),
                  pl.BlockSpec((mb, d), lambda i, k: (0, 5)),
                  pl.BlockSpec((1, d), lambda i, k: (0, 0))],
        out_specs=pl.BlockSpec((tm, d), lambda i, k: (i, 0)),
        out_shape=jax.ShapeDtypeStruct((m, d), F32),
        scratch_shapes=[pltpu.VMEM((tk, d), BF16)],
        compiler_params=_params(2),
        name="ffn_down",
    )(act, w_down, x, mod, final_g)


def _rope_tables(pos):
    inv_freq = ROPE_THETA ** (-jnp.arange(0, ROT_DIM, 2, dtype=F32) / ROT_DIM)
    ang = pos[:, None] * inv_freq[None, :]
    cos, sin = jnp.cos(ang), jnp.sin(ang)
    d = jnp.arange(LANES) % HEAD_DIM
    idx = d % ROT_HALF
    cos_t = jnp.where(d[None, :] < ROT_DIM, cos[:, idx], 1.0)
    sa_t = jnp.where((d[None, :] >= ROT_HALF) & (d[None, :] < ROT_DIM), sin[:, idx], 0.0)
    sb_t = jnp.where(d[None, :] < ROT_HALF, -sin[:, idx], 0.0)
    return cos_t.astype(F32), sa_t.astype(F32), sb_t.astype(F32)


def kernel(x_prompt, x_sample, cache_k, cache_v, state_conv, state_ffn, page_table, c_prompt, c_sample, w_ada, b_ada, norm1_g, w_in, lambda_q1, lambda_k1, lambda_q2, lambda_k2, subln_g, w_attn_out, conv_w, w_conv_out, w_o, norm2_g, w_up, ffn_conv_w, ffn_conv_b, w_down, final_g):
    depth = w_ada.shape[0]
    bp, seq, d = x_prompt.shape
    bs, dec_seq, _ = x_sample.shape
    assert bp == 1 and dec_seq == 1, "one prompt sequence and one new token per sample sequence"
    assert depth == 1, "the final norm is fused into the ConvFFN-down kernel of the only layer"
    n_pool = cache_k.shape[1]
    past_len = page_table.shape[1] * PAGE_SIZE
    qk_w = N_HEADS * QK_DIM
    cw = conv_w.shape[-1]
    f = w_down.shape[1]
    col_conv = 2 * qk_w + ATTN_W
    col_ga = col_conv + 3 * cw
    col_gc = col_ga + d
    tm_p = _pick(seq, (1024, 512, 256))
    tm_s = bs

    xp = x_prompt.reshape(seq, d)
    xs = x_sample.reshape(bs, d)
    pad = (-(bs + bp)) % 16
    c_rows = jnp.concatenate([c_sample, c_prompt, jnp.zeros((pad, d), F32)], axis=0)
    rope_p = _rope_tables(jnp.arange(seq, dtype=F32))
    rope_s = _rope_tables(jnp.full((bs,), past_len, dtype=F32))
    fin_g = final_g.reshape(1, d)

    outs = [[] for _ in range(8)]
    for li in range(depth):
        lam_init = _lambda_init(li)
        mod = _ada(c_rows, w_ada[li], b_ada[li])
        mod_s, mod_p = mod[:bs], mod[bs:bs + bp]
        lams = tuple(a[li].reshape(1, HEAD_DIM) for a in (lambda_q1, lambda_k1, lambda_q2, lambda_k2))
        g1, g2 = norm1_g[li].reshape(1, d), norm2_g[li].reshape(1, d)
        sub_g = subln_g[li].reshape(1, V_DIM)

        h, q, kf, kb, vf, vb = _qkv(xp, g1, mod_p, w_in[li], *rope_p, tm_p)
        u, tail = _sconv(h, w_in[li], conv_w[li], col_conv, tm_p)
        o = _prompt_attention(q, kb, vb, lams, sub_g, lam_init)
        merged = _merge(h, o, u, w_in[li], w_attn_out[li], w_conv_out[li], col_ga, col_gc, tm_p)
        x1 = _resid_proj(merged, w_o[li], xp, mod_p, 2, tm_p)
        act, tg, tv = _ffn_up(x1, g2, mod_p, w_up[li], ffn_conv_w[li], ffn_conv_b[li], tm_p)
        xp = _ffn_down(act, w_down[li], x1, mod_p, fin_g, tm_p)
        keep = SUBLANES - (CONV_K - 1)
        outs[0].append(kf.reshape(bp, seq, N_HEADS, QK_DIM))
        outs[1].append(vf.reshape(bp, seq, N_HEADS, V_DIM))
        outs[2].append(tail[-1, keep:].reshape(bp, CONV_K - 1, cw))
        outs[3].append(jnp.concatenate([tg[-1, keep:], tv[-1, keep:]], axis=-1).reshape(bp, CONV_K - 1, 2 * f))

        sc_prev, sf_prev = state_conv[li], state_ffn[li]
        h, q, kf, kb, vf, vb = _qkv(xs, g1, mod_s, w_in[li], *rope_s, tm_s)
        u, z = _sconv(h, w_in[li], conv_w[li], col_conv, tm_s, state=(sc_prev[:, 0], sc_prev[:, 1]))
        r3 = lambda a: a.reshape(bs, 1, a.shape[-1])
        o = _sample_attention(r3(q), r3(kb), r3(vb),
                              cache_k.reshape(depth * n_pool, PAGE_SIZE, qk_w),
                              cache_v.reshape(depth * n_pool, PAGE_SIZE, ATTN_W),
                              li * n_pool, page_table, lams, sub_g, lam_init).reshape(bs, ATTN_W)
        merged = _merge(h, o, u, w_in[li], w_attn_out[li], w_conv_out[li], col_ga, col_gc, tm_s)
        x1 = _resid_proj(merged, w_o[li], xs, mod_s, 2, tm_s)
        act, ug, uv = _ffn_up(x1, g2, mod_s, w_up[li], ffn_conv_w[li], ffn_conv_b[li], tm_s,
                              state=(sf_prev[:, 0], sf_prev[:, 1]))
        xs = _ffn_down(act, w_down[li], x1, mod_s, fin_g, tm_s)
        outs[4].append(kf.reshape(bs, dec_seq, N_HEADS, QK_DIM))
        outs[5].append(vf.reshape(bs, dec_seq, N_HEADS, V_DIM))
        outs[6].append(jnp.stack([sc_prev[:, 1], z], axis=1))
        outs[7].append(jnp.stack([sf_prev[:, 1], jnp.concatenate([ug, uv], axis=-1)], axis=1))

    return (xp.reshape(bp, seq, d), xs.reshape(bs, dec_seq, d),
            *(jnp.stack(o) for o in outs))
```

```python
import functools
import math

import jax
import jax.numpy as jnp
from jax import lax
from jax.experimental import pallas as pl
from jax.experimental.pallas import tpu as pltpu

F32 = jnp.float32
BF16 = jnp.bfloat16

N_HEADS = 8
HEAD_DIM = 64
QK_DIM = 2 * HEAD_DIM
V_DIM = 2 * HEAD_DIM
ROT_DIM = HEAD_DIM // 4
ROT_HALF = ROT_DIM // 2
ROPE_THETA = 500000.0
ATTN_W = N_HEADS * V_DIM
CONV_K = 3
NORM_EPS = 1e-6
SUBLN_EPS = 1e-5
ATTN_SCALE = HEAD_DIM ** -0.5
Q_SCALE = ATTN_SCALE * math.log2(math.e)
PAGE_SIZE = 128
N_MAPS = 2 * N_HEADS

LANES = 128
SUBLANES = 8
BF16_ROWS = 2 * SUBLANES
MXU_DIM = 256
VMEM_LIMIT_BYTES = 56 * 1024 * 1024
ROW_SUB = MXU_DIM
CAST_ROWS = 256
NEG_INF = float("-inf")


def _lambda_init(layer):
    return 0.8 - 0.6 * math.exp(-0.3 * layer)


def _params(n_axes):
    return pltpu.CompilerParams(dimension_semantics=("arbitrary",) * n_axes,
                                vmem_limit_bytes=VMEM_LIMIT_BYTES)


def _sigmoid(x):
    return 1.0 / (1.0 + jnp.exp(-x))


def _bdot(a, b):
    return jnp.dot(a, b, preferred_element_type=F32)


def _pick(n, candidates):
    for c in candidates:
        if n % c == 0:
            return c
    return n


def _row_blocks(tm):
    return [slice(r, min(r + ROW_SUB, tm)) for r in range(0, tm, ROW_SUB)]


def _cast_weight(w_ref, wb_ref):
    k = w_ref.shape[0]
    for r in range(0, k, CAST_ROWS):
        rows = slice(r, min(r + CAST_ROWS, k))
        wb_ref[rows, :] = w_ref[rows, :].astype(BF16)


def _mod_rows(ref, rows):
    return ref[...] if ref.shape[0] == 1 else ref[rows, :]


def _ada_kernel(c_ref, w_ref, b_ref, o_ref, wb_ref):
    c = c_ref[...]
    s = (c * _sigmoid(c)).astype(BF16)
    _cast_weight(w_ref, wb_ref)
    o_ref[...] = _bdot(s, wb_ref[...]) + b_ref[...]


def _ada(c_rows, w_ada, b_ada):
    m, d = c_rows.shape
    n = w_ada.shape[1]
    tn = _pick(n, (1024, 512, 256, 128))
    return pl.pallas_call(
        _ada_kernel,
        grid=(n // tn,),
        in_specs=[pl.BlockSpec((m, d), lambda j: (0, 0)),
                  pl.BlockSpec((d, tn), lambda j: (0, j)),
                  pl.BlockSpec((1, tn), lambda j: (0, j))],
        out_specs=pl.BlockSpec((m, tn), lambda j: (0, j)),
        out_shape=jax.ShapeDtypeStruct((m, n), F32),
        scratch_shapes=[pltpu.VMEM((d, tn), BF16)],
        compiler_params=_params(1),
        name="ada_mod",
    )(c_rows, w_ada, b_ada.reshape(1, n))


def _modulated_rmsnorm(x, g, sc, sh):
    ms = jnp.mean(x * x, axis=-1, keepdims=True)
    y = x * lax.rsqrt(ms + NORM_EPS)
    return y * g * (1.0 + sc) + sh


def _norm_prologue(x_ref, g_ref, sc_ref, sh_ref, h_ref):
    for rows in _row_blocks(x_ref.shape[0]):
        h = _modulated_rmsnorm(x_ref[rows, :], g_ref[...], _mod_rows(sc_ref, rows), _mod_rows(sh_ref, rows))
        h_ref[rows, :] = h.astype(BF16)


def _rope_tile(r, cos, sa, sb):
    outs = []
    for hh in range(r.shape[1] // LANES):
        xh = r[:, hh * LANES:(hh + 1) * LANES]
        outs.append(xh * cos + pltpu.roll(xh, ROT_HALF, 1) * sa
                    + pltpu.roll(xh, LANES - ROT_HALF, 1) * sb)
    return jnp.concatenate(outs, axis=1) if len(outs) > 1 else outs[0]


def _qkv_kernel(x_ref, g_ref, sc_ref, sh_ref, w_ref, cos_ref, sa_ref, sb_ref,
                h_ref, q_ref, kf_ref, kb_ref, vf_ref, vb_ref, wb_ref, *, nq):
    j = pl.program_id(1)

    @pl.when(j == 0)
    def _():
        _norm_prologue(x_ref, g_ref, sc_ref, sh_ref, h_ref)

    _cast_weight(w_ref, wb_ref)

    def run(epilogue):
        for rows in _row_blocks(h_ref.shape[0]):
            epilogue(rows, _bdot(h_ref[rows, :], wb_ref[...]))

    def rope(rows, res):
        return _rope_tile(res, cos_ref[rows, :], sa_ref[rows, :], sb_ref[rows, :])

    @pl.when(j < nq)
    def _():
        def epilogue(rows, res):
            q_ref[rows, :] = (rope(rows, res) * Q_SCALE).astype(BF16)
        run(epilogue)

    @pl.when((j >= nq) & (j < 2 * nq))
    def _():
        def epilogue(rows, res):
            r = rope(rows, res)
            kf_ref[rows, :] = r
            kb_ref[rows, :] = r.astype(BF16)
        run(epilogue)

    @pl.when(j >= 2 * nq)
    def _():
        def epilogue(rows, res):
            vf_ref[rows, :] = res
            vb_ref[rows, :] = res.astype(BF16)
        run(epilogue)


def _qkv(x, g, mod, w_in, cos, sa, sb, tm):
    m, d = x.shape
    mb = mod.shape[0]
    w = N_HEADS * QK_DIM
    tn = 512
    nq = w // tn
    row = lambda i, j: (i, 0)
    out_col = lambda lo: (lambda i, j: (i, jnp.clip(j - lo, 0, nq - 1)))
    return pl.pallas_call(
        functools.partial(_qkv_kernel, nq=nq),
        grid=(m // tm, 3 * nq),
        in_specs=[pl.BlockSpec((tm, d), row, pipeline_mode=pl.Buffered(1)),
                  pl.BlockSpec((1, d), lambda i, j: (0, 0)),
                  pl.BlockSpec((mb, d), lambda i, j: (0, 1)),
                  pl.BlockSpec((mb, d), lambda i, j: (0, 0)),
                  pl.BlockSpec((d, tn), lambda i, j: (0, j)),
                  pl.BlockSpec((tm, LANES), row),
                  pl.BlockSpec((tm, LANES), row),
                  pl.BlockSpec((tm, LANES), row)],
        out_specs=[pl.BlockSpec((tm, d), row),
                   pl.BlockSpec((tm, tn), out_col(0)),
                   pl.BlockSpec((tm, tn), out_col(nq)),
                   pl.BlockSpec((tm, tn), out_col(nq)),
                   pl.BlockSpec((tm, tn), out_col(2 * nq)),
                   pl.BlockSpec((tm, tn), out_col(2 * nq))],
        out_shape=[jax.ShapeDtypeStruct((m, d), BF16),
                   jax.ShapeDtypeStruct((m, w), BF16),
                   jax.ShapeDtypeStruct((m, w), F32),
                   jax.ShapeDtypeStruct((m, w), BF16),
                   jax.ShapeDtypeStruct((m, w), F32),
                   jax.ShapeDtypeStruct((m, w), BF16)],
        scratch_shapes=[pltpu.VMEM((d, tn), BF16)],
        compiler_params=_params(2),
        name="qkv_proj",
    )(x, g, mod, mod, w_in, cos, sa, sb)


def _conv_rows(z, prev8, w_ref):
    n = z.shape[0]
    zz = jnp.concatenate([prev8, z], axis=0)
    z1 = zz[SUBLANES - 1:SUBLANES - 1 + n]
    z2 = zz[SUBLANES - 2:SUBLANES - 2 + n]
    return z2 * w_ref[0:1, :] + z1 * w_ref[1:2, :] + z * w_ref[2:3, :]


def _conv_state(p0, p1, z, w_ref):
    return p0 * w_ref[0:1, :] + p1 * w_ref[1:2, :] + z * w_ref[2:3, :]


def _last8(z):
    return z[z.shape[0] - SUBLANES:]


def _sconv_seq_kernel(h_ref, wb_ref, wc_ref, wx_ref, cw_ref, u_ref, tail_ref,
                      wbb_ref, wcb_ref, wxb_ref, carry_ref):
    i, j = pl.program_id(0), pl.program_id(1)
    _cast_weight(wb_ref, wbb_ref)
    _cast_weight(wc_ref, wcb_ref)
    _cast_weight(wx_ref, wxb_ref)

    @pl.when(i == 0)
    def _():
        carry_ref[j] = jnp.zeros(carry_ref.shape[1:], F32)

    prev = carry_ref[j]
    for rows in _row_blocks(h_ref.shape[0]):
        h = h_ref[rows, :]
        z = _bdot(h, wcb_ref[...]) * _bdot(h, wxb_ref[...])
        u_ref[rows, :] = (_bdot(h, wbb_ref[...]) * _conv_rows(z, prev, cw_ref)).astype(BF16)
        prev = _last8(z)
    carry_ref[j] = prev
    tail_ref[0] = prev


def _sconv_state_kernel(h_ref, wb_ref, wc_ref, wx_ref, cw_ref, p0_ref, p1_ref, u_ref, z_ref,
                        wbb_ref, wcb_ref, wxb_ref):
    _cast_weight(wb_ref, wbb_ref)
    _cast_weight(wc_ref, wcb_ref)
    _cast_weight(wx_ref, wxb_ref)
    h = h_ref[...]
    z = _bdot(h, wcb_ref[...]) * _bdot(h, wxb_ref[...])
    conv = _conv_state(p0_ref[...], p1_ref[...], z, cw_ref)
    u_ref[...] = (_bdot(h, wbb_ref[...]) * conv).astype(BF16)
    z_ref[...] = z


def _sconv(h, w_in, conv_w, col0, tm, state=None):
    m, d = h.shape
    cw = conv_w.shape[1]
    tn = 512
    nj = cw // tn
    wspec = lambda k: pl.BlockSpec((d, tn), lambda i, j: (0, (col0 + k * cw) // tn + j))
    tile = pl.BlockSpec((tm, tn), lambda i, j: (i, j))
    common = [pl.BlockSpec((tm, d), lambda i, j: (i, 0)), wspec(0), wspec(1), wspec(2),
              pl.BlockSpec((CONV_K, tn), lambda i, j: (0, j))]
    wscratch = [pltpu.VMEM((d, tn), BF16)] * 3
    if state is None:
        return pl.pallas_call(
            _sconv_seq_kernel,
            grid=(m // tm, nj),
            in_specs=common,
            out_specs=[tile, pl.BlockSpec((1, SUBLANES, tn), lambda i, j: (i, 0, j))],
            out_shape=[jax.ShapeDtypeStruct((m, cw), BF16),
                       jax.ShapeDtypeStruct((m // tm, SUBLANES, cw), F32)],
            scratch_shapes=wscratch + [pltpu.VMEM((nj, SUBLANES, tn), F32)],
            compiler_params=_params(2),
            name="sconv_seq",
        )(h, w_in, w_in, w_in, conv_w)
    p0, p1 = state
    return pl.pallas_call(
        _sconv_state_kernel,
        grid=(m // tm, nj),
        in_specs=common + [tile, tile],
        out_specs=[tile, tile],
        out_shape=[jax.ShapeDtypeStruct((m, cw), BF16), jax.ShapeDtypeStruct((m, cw), F32)],
        scratch_shapes=wscratch,
        compiler_params=_params(2),
        name="sconv_state",
    )(h, w_in, w_in, w_in, conv_w, p0, p1)


def _lambda_full(lq1_ref, lk1_ref, lq2_ref, lk2_ref, lam_init):
    a = jnp.sum(lq1_ref[...] * lk1_ref[...], axis=-1, keepdims=True)
    b = jnp.sum(lq2_ref[...] * lk2_ref[...], axis=-1, keepdims=True)
    return jnp.exp(a) - jnp.exp(b) + lam_init


def _attn_kernel(lq1_ref, lk1_ref, lq2_ref, lk2_ref, g_ref, q_ref, k_ref, v_ref, o_ref,
                 qst_ref, vt_ref, m_ref, acc_ref, s0_ref, s1_ref, *, bq, lam_init):
    qi = pl.program_id(1)
    t = k_ref.shape[0]
    bk = s0_ref.shape[0]

    @pl.when(qi == 0)
    def _():
        for c in range(t // bq):
            blk = v_ref[c * bq:(c + 1) * bq, :].astype(F32)
            vt_ref[0:V_DIM, c * bq:(c + 1) * bq] = blk.T.astype(BF16)
        vt_ref[V_DIM:, :] = jnp.ones((vt_ref.shape[0] - V_DIM, t), BF16)

    qt = q_ref[...].astype(F32).T
    dim = lax.broadcasted_iota(jnp.int32, qt.shape, 0)
    qst_ref[:, 0:bq] = jnp.where(dim < HEAD_DIM, qt, 0.0).astype(BF16)
    qst_ref[:, bq:2 * bq] = jnp.where(dim >= HEAD_DIM, qt, 0.0).astype(BF16)
    m_ref[...] = jnp.full(m_ref.shape, NEG_INF, F32)
    acc_ref[...] = jnp.zeros(acc_ref.shape, F32)

    def scores(j, s_ref):
        off = pl.multiple_of(j * bk, bk)
        s_ref[...] = _bdot(k_ref[pl.ds(off, bk), :], qst_ref[...])

    def softmax_pv(j, s_ref, diag_key0=None):
        off = pl.multiple_of(j * bk, bk)
        st = s_ref[...]
        if diag_key0 is not None:
            key = lax.broadcasted_iota(jnp.int32, st.shape, 0) + diag_key0
            qry = lax.broadcasted_iota(jnp.int32, st.shape, 1)
            qry = jnp.where(qry >= bq, qry - bq, qry)
            st = jnp.where(key <= qry, st, NEG_INF)
        m_prev = m_ref[...]
        m_new = jnp.maximum(m_prev, jnp.max(st, axis=0, keepdims=True))
        alpha = jnp.exp2(m_prev - m_new)
        pt = jnp.exp2(st - m_new)
        acc_ref[...] = alpha * acc_ref[...] + _bdot(vt_ref[:, pl.ds(off, bk)], pt.astype(BF16))
        m_ref[...] = m_new

    def body(jj, carry):
        j = 2 * jj
        scores(j + 1, s1_ref)
        softmax_pv(j, s0_ref)
        scores(j + 2, s0_ref)
        softmax_pv(j + 1, s1_ref)
        return carry

    scores(0, s0_ref)
    lax.fori_loop(0, qi, body, 0)
    scores(2 * qi + 1, s1_ref)
    softmax_pv(2 * qi, s0_ref, diag_key0=0)
    softmax_pv(2 * qi + 1, s1_ref, diag_key0=bk)

    lam = _lambda_full(lq1_ref, lk1_ref, lq2_ref, lk2_ref, lam_init)
    ot = acc_ref[0:V_DIM, :] / acc_ref[V_DIM:V_DIM + 1, :]
    dlt = ot[:, 0:bq] - lam * ot[:, bq:2 * bq]
    ms = jnp.mean(dlt * dlt, axis=0, keepdims=True)
    y = (dlt * lax.rsqrt(ms + SUBLN_EPS)).T * g_ref[...] * (1.0 - lam_init)
    o_ref[...] = y.astype(BF16)


def _prompt_attention(q, k, v, lams, subln_g, lam_init):
    t = q.shape[0]
    bq = _pick(t, (512, 256))
    vec = lambda n: pl.BlockSpec((1, n), lambda h, i: (0, 0))
    return pl.pallas_call(
        functools.partial(_attn_kernel, bq=bq, lam_init=lam_init),
        grid=(N_HEADS, t // bq),
        in_specs=[vec(HEAD_DIM)] * 4 + [vec(V_DIM),
                  pl.BlockSpec((bq, QK_DIM), lambda h, i: (i, h)),
                  pl.BlockSpec((t, QK_DIM), lambda h, i: (0, h)),
                  pl.BlockSpec((t, V_DIM), lambda h, i: (0, h))],
        out_specs=pl.BlockSpec((bq, V_DIM), lambda h, i: (i, h)),
        out_shape=jax.ShapeDtypeStruct((t, ATTN_W), BF16),
        scratch_shapes=[pltpu.VMEM((QK_DIM, 2 * bq), BF16),
                        pltpu.VMEM((V_DIM + BF16_ROWS, t), BF16),
                        pltpu.VMEM((1, 2 * bq), F32),
                        pltpu.VMEM((V_DIM + BF16_ROWS, 2 * bq), F32),
                        pltpu.VMEM((bq // 2, 2 * bq), F32),
                        pltpu.VMEM((bq // 2, 2 * bq), F32)],
        compiler_params=_params(2),
        name="prompt_attn",
    )(*lams, subln_g, q, k, v)


def _sattn_kernel(pt_ref, lq1_ref, lk1_ref, lq2_ref, lk2_ref, g_ref, q_ref, kn_ref, vn_ref, *rest,
                  pg, lam_init):
    k_refs, v_refs = rest[:pg], rest[pg:2 * pg]
    o_ref, qm_ref, m_ref, l_ref, acc_ref = rest[2 * pg:]
    step = pl.program_id(1)
    page_rows = PAGE_SIZE * N_HEADS
    lane = lax.broadcasted_iota(jnp.int32, (N_HEADS, QK_DIM), 1)

    @pl.when(step == 0)
    def _():
        q = q_ref[0].astype(F32)
        qm_ref[0:N_HEADS] = jnp.where(lane < HEAD_DIM, q, 0.0)
        qm_ref[N_HEADS:N_MAPS] = jnp.where(lane >= HEAD_DIM, q, 0.0)
        m_ref[...] = jnp.full(m_ref.shape, NEG_INF, F32)
        l_ref[...] = jnp.zeros(l_ref.shape, F32)
        acc_ref[...] = jnp.zeros(acc_ref.shape, F32)

    qm = qm_ref[...]
    qmb = qm.astype(BF16)
    s = jnp.concatenate(
        [lax.dot_general(qmb, k_refs[p][0, 0].reshape(page_rows, QK_DIM).astype(BF16),
                         (((1,), (1,)), ((), ())), preferred_element_type=F32)
         for p in range(pg)], axis=1)
    row = lax.broadcasted_iota(jnp.int32, s.shape, 0)
    col = lax.broadcasted_iota(jnp.int32, s.shape, 1)
    s = jnp.where(col % N_HEADS == row % N_HEADS, s, NEG_INF)
    m_prev = m_ref[...]
    m_new = jnp.maximum(m_prev, jnp.max(s, axis=-1, keepdims=True))
    alpha = jnp.exp2(m_prev - m_new)
    p_all = jnp.exp2(s - m_new)
    l_ref[...] = alpha * l_ref[...] + jnp.sum(p_all, axis=-1, keepdims=True)
    pb = p_all.astype(BF16)
    pv = None
    for p in range(pg):
        part = _bdot(pb[:, p * page_rows:(p + 1) * page_rows],
                     v_refs[p][0, 0].reshape(page_rows, V_DIM).astype(BF16))
        pv = part if pv is None else pv + part
    acc_ref[...] = alpha * acc_ref[...] + pv
    m_ref[...] = m_new

    @pl.when(step == pl.num_programs(1) - 1)
    def _():
        kn = kn_ref[0].astype(F32)
        vn = vn_ref[0].astype(F32)
        kn2 = jnp.concatenate([kn, kn], axis=0)
        vn2 = jnp.concatenate([vn, vn], axis=0)
        s_new = jnp.sum(qm * kn2, axis=-1, keepdims=True)
        m_old = m_ref[...]
        m_fin = jnp.maximum(m_old, s_new)
        a = jnp.exp2(m_old - m_fin)
        p_new = jnp.exp2(s_new - m_fin)
        l_fin = a * l_ref[...] + p_new
        o = (a * acc_ref[...] + p_new * vn2) / l_fin
        lam = _lambda_full(lq1_ref, lk1_ref, lq2_ref, lk2_ref, lam_init)
        dlt = o[0:N_HEADS] - lam * o[N_HEADS:N_MAPS]
        ms = jnp.mean(dlt * dlt, axis=-1, keepdims=True)
        o_ref[0] = dlt * lax.rsqrt(ms + SUBLN_EPS) * g_ref[...] * (1.0 - lam_init)


def _sample_attention(q, k_new, v_new, cache_k, cache_v, layer, page_table, lams, subln_g, lam_init):
    b, n_pages = page_table.shape
    pg = _pick(n_pages, (8, 4, 2, 1))
    vec = lambda n: pl.BlockSpec((1, n), lambda i, s, pt: (0, 0))
    rowspec = pl.BlockSpec((1, N_HEADS, QK_DIM), lambda i, s, pt: (i, 0, 0))
    page = lambda p: pl.BlockSpec((1, 1, PAGE_SIZE, N_HEADS, QK_DIM),
                                  lambda i, s, pt: (layer, pt[i, s * pg + p], 0, 0, 0))
    grid_spec = pltpu.PrefetchScalarGridSpec(
        num_scalar_prefetch=1,
        grid=(b, n_pages // pg),
        in_specs=[vec(HEAD_DIM)] * 4 + [vec(V_DIM), rowspec, rowspec, rowspec]
                 + [page(p) for p in range(pg)] + [page(p) for p in range(pg)],
        out_specs=rowspec,
        scratch_shapes=[pltpu.VMEM((N_MAPS, QK_DIM), F32),
                        pltpu.VMEM((N_MAPS, 1), F32),
                        pltpu.VMEM((N_MAPS, 1), F32),
                        pltpu.VMEM((N_MAPS, V_DIM), F32)],
    )
    return pl.pallas_call(
        functools.partial(_sattn_kernel, pg=pg, lam_init=lam_init),
        grid_spec=grid_spec,
        out_shape=jax.ShapeDtypeStruct((b, N_HEADS, V_DIM), F32),
        compiler_params=_params(2),
        name="sample_attn",
    )(page_table, *lams, subln_g, q, k_new, v_new, *([cache_k] * pg), *([cache_v] * pg))


def _merge_kernel(h_ref, o_ref, u_ref, wga_ref, wgc_ref, wao_ref, wco_ref, out_ref,
                  wgab_ref, wgcb_ref, waob_ref, wcob_ref):
    _cast_weight(wga_ref, wgab_ref)
    _cast_weight(wgc_ref, wgcb_ref)
    _cast_weight(wao_ref, waob_ref)
    _cast_weight(wco_ref, wcob_ref)
    for rows in _row_blocks(h_ref.shape[0]):
        h = h_ref[rows, :]
        att = _sigmoid(_bdot(h, wgab_ref[...])) * _bdot(o_ref[rows, :].astype(BF16), waob_ref[...])
        conv = _sigmoid(_bdot(h, wgcb_ref[...])) * _bdot(u_ref[rows, :], wcob_ref[...])
        out_ref[rows, :] = (att + conv).astype(BF16)


def _merge(h, o, u, w_in, w_attn_out, w_conv_out, col_ga, col_gc, tm):
    m, d = h.shape
    tn = 256
    ka, kc = w_attn_out.shape[0], w_conv_out.shape[0]
    row = lambda w: pl.BlockSpec((tm, w), lambda i, j: (i, 0))
    return pl.pallas_call(
        _merge_kernel,
        grid=(m // tm, d // tn),
        in_specs=[row(d), row(o.shape[1]), row(u.shape[1]),
                  pl.BlockSpec((d, tn), lambda i, j: (0, col_ga // tn + j)),
                  pl.BlockSpec((d, tn), lambda i, j: (0, col_gc // tn + j)),
                  pl.BlockSpec((ka, tn), lambda i, j: (0, j)),
                  pl.BlockSpec((kc, tn), lambda i, j: (0, j))],
        out_specs=pl.BlockSpec((tm, tn), lambda i, j: (i, j)),
        out_shape=jax.ShapeDtypeStruct((m, d), BF16),
        scratch_shapes=[pltpu.VMEM((d, tn), BF16), pltpu.VMEM((d, tn), BF16),
                        pltpu.VMEM((ka, tn), BF16), pltpu.VMEM((kc, tn), BF16)],
        compiler_params=_params(2),
        name="merge",
    )(h, o, u, w_in, w_in, w_attn_out, w_conv_out)


def _resid_kernel(a_ref, w_ref, x_ref, g_ref, o_ref, wb_ref):
    _cast_weight(w_ref, wb_ref)
    for rows in _row_blocks(a_ref.shape[0]):
        o_ref[rows, :] = x_ref[rows, :] + _mod_rows(g_ref, rows) * _bdot(a_ref[rows, :], wb_ref[...])


def _resid_proj(a, w, x, mod, gate_col, tm):
    m, d = x.shape
    k = a.shape[1]
    mb = mod.shape[0]
    tn = 512
    per = d // tn
    return pl.pallas_call(
        _resid_kernel,
        grid=(m // tm, d // tn),
        in_specs=[pl.BlockSpec((tm, k), lambda i, j: (i, 0)),
                  pl.BlockSpec((k, tn), lambda i, j: (0, j)),
                  pl.BlockSpec((tm, tn), lambda i, j: (i, j)),
                  pl.BlockSpec((mb, tn), lambda i, j: (0, gate_col * per + j))],
        out_specs=pl.BlockSpec((tm, tn), lambda i, j: (i, j)),
        out_shape=jax.ShapeDtypeStruct((m, d), F32),
        scratch_shapes=[pltpu.VMEM((k, tn), BF16)],
        compiler_params=_params(2),
        name="resid_proj",
    )(a, w, x, mod)


def _ffn_up_seq_kernel(x_ref, g_ref, sc_ref, sh_ref, wg_ref, wv_ref, cwg_ref, cwv_ref,
                       bg_ref, bv_ref, act_ref, tg_ref, tv_ref, h_ref, wgb_ref, wvb_ref, carry_ref):
    i, j = pl.program_id(0), pl.program_id(1)

    @pl.when(j == 0)
    def _():
        _norm_prologue(x_ref, g_ref, sc_ref, sh_ref, h_ref)

    _cast_weight(wg_ref, wgb_ref)
    _cast_weight(wv_ref, wvb_ref)

    @pl.when(i == 0)
    def _():
        carry_ref[j] = jnp.zeros(carry_ref.shape[1:], F32)

    prev_g, prev_v = carry_ref[j, 0], carry_ref[j, 1]
    for rows in _row_blocks(h_ref.shape[0]):
        h = h_ref[rows, :]
        ug, uv = _bdot(h, wgb_ref[...]), _bdot(h, wvb_ref[...])
        gate = _conv_rows(ug, prev_g, cwg_ref) + bg_ref[...]
        val = _conv_rows(uv, prev_v, cwv_ref) + bv_ref[...]
        act_ref[rows, :] = (gate * _sigmoid(gate) * val).astype(BF16)
        prev_g, prev_v = _last8(ug), _last8(uv)
    carry_ref[j, 0] = prev_g
    carry_ref[j, 1] = prev_v
    tg_ref[0] = prev_g
    tv_ref[0] = prev_v


def _ffn_up_state_kernel(x_ref, g_ref, sc_ref, sh_ref, wg_ref, wv_ref, cwg_ref, cwv_ref,
                         bg_ref, bv_ref, pg0_ref, pg1_ref, pv0_ref, pv1_ref,
                         act_ref, ug_ref, uv_ref, h_ref, wgb_ref, wvb_ref):
    @pl.when(pl.program_id(1) == 0)
    def _():
        _norm_prologue(x_ref, g_ref, sc_ref, sh_ref, h_ref)

    _cast_weight(wg_ref, wgb_ref)
    _cast_weight(wv_ref, wvb_ref)
    h = h_ref[...]
    ug, uv = _bdot(h, wgb_ref[...]), _bdot(h, wvb_ref[...])
    gate = _conv_state(pg0_ref[...], pg1_ref[...], ug, cwg_ref) + bg_ref[...]
    val = _conv_state(pv0_ref[...], pv1_ref[...], uv, cwv_ref) + bv_ref[...]
    act_ref[...] = (gate * _sigmoid(gate) * val).astype(BF16)
    ug_ref[...] = ug
    uv_ref[...] = uv


def _ffn_up(x, g, mod, w_up, conv_w, conv_b, tm, state=None):
    m, d = x.shape
    mb = mod.shape[0]
    f = w_up.shape[1] // 2
    tn = _pick(f, (512, 256))
    nj = f // tn
    row = lambda i, j: (i, 0)
    lo = lambda i, j: (0, j)
    hi = lambda i, j: (0, nj + j)
    tile = pl.BlockSpec((tm, tn), lambda i, j: (i, j))
    tile_hi = pl.BlockSpec((tm, tn), lambda i, j: (i, nj + j))
    conv_b = conv_b.reshape(1, 2 * f)
    common = [pl.BlockSpec((tm, d), row, pipeline_mode=pl.Buffered(1)),
              pl.BlockSpec((1, d), lambda i, j: (0, 0)),
              pl.BlockSpec((mb, d), lambda i, j: (0, 4)),
              pl.BlockSpec((mb, d), lambda i, j: (0, 3)),
              pl.BlockSpec((d, tn), lo), pl.BlockSpec((d, tn), hi),
              pl.BlockSpec((CONV_K, tn), lo), pl.BlockSpec((CONV_K, tn), hi),
              pl.BlockSpec((1, tn), lo), pl.BlockSpec((1, tn), hi)]
    args = (x, g, mod, mod, w_up, w_up, conv_w, conv_w, conv_b, conv_b)
    scratch = [pltpu.VMEM((tm, d), BF16), pltpu.VMEM((d, tn), BF16), pltpu.VMEM((d, tn), BF16)]
    if state is None:
        tail = pl.BlockSpec((1, SUBLANES, tn), lambda i, j: (i, 0, j))
        return pl.pallas_call(
            _ffn_up_seq_kernel,
            grid=(m // tm, nj),
            in_specs=common,
            out_specs=[tile, tail, tail],
            out_shape=[jax.ShapeDtypeStruct((m, f), BF16),
                       jax.ShapeDtypeStruct((m // tm, SUBLANES, f), F32),
                       jax.ShapeDtypeStruct((m // tm, SUBLANES, f), F32)],
            scratch_shapes=scratch + [pltpu.VMEM((nj, 2, SUBLANES, tn), F32)],
            compiler_params=_params(2),
            name="ffn_up_seq",
        )(*args)
    p0, p1 = state
    return pl.pallas_call(
        _ffn_up_state_kernel,
        grid=(m // tm, nj),
        in_specs=common + [tile, tile, tile_hi, tile_hi],
        out_specs=[tile, tile, tile],
        out_shape=[jax.ShapeDtypeStruct((m, f), BF16),
                   jax.ShapeDtypeStruct((m, f), F32),
                   jax.ShapeDtypeStruct((m, f), F32)],
        scratch_shapes=scratch,
        compiler_params=_params(2),
        name="ffn_up_state",
    )(*args, p0, p1, p0, p1)


def _ffn_down_kernel(a_ref, w_ref, x_ref, g2_ref, fg_ref, y_ref, wb_ref):
    kk = pl.program_id(1)
    last = pl.num_programs(1) - 1
    _cast_weight(w_ref, wb_ref)
    blocks = _row_blocks(a_ref.shape[0])

    @pl.when(kk == 0)
    def _():
        for rows in blocks:
            y_ref[rows, :] = _bdot(a_ref[rows, :], wb_ref[...])

    @pl.when((kk > 0) & (kk < last))
    def _():
        for rows in blocks:
            y_ref[rows, :] += _bdot(a_ref[rows, :], wb_ref[...])

    @pl.when(kk == last)
    def _():
        for rows in blocks:
            acc = y_ref[rows, :] + _bdot(a_ref[rows, :], wb_ref[...])
            x = x_ref[rows, :] + _mod_rows(g2_ref, rows) * acc
            ms = jnp.mean(x * x, axis=-1, keepdims=True)
            y_ref[rows, :] = x * lax.rsqrt(ms + NORM_EPS) * fg_ref[...]


def _ffn_down(act, w_down, x, mod, final_g, tm):
    m, d = x.shape
    f = act.shape[1]
    mb = mod.shape[0]
    tk = _pick(f, (512, 256, 128))
    assert f // tk >= 2, "the accumulate-in-output schedule needs at least two K steps"
    return pl.pallas_call(
        _ffn_down_kernel,
        grid=(m // tm, f // tk),
        in_specs=[pl.BlockSpec((tm, tk), lambda i, k: (i, k)),
                  pl.BlockSpec((tk, d), lambda i, k: (k, 0)),
                  pl.BlockSpec((tm, d), lambda i, k: (i, 0), pipeline_mode=pl.Buffered(1)),
                  pl.BlockSpec((mb, d), lambda i, k: (0, 5)),
                  pl.BlockSpec((1, d), lambda i, k: (0, 0))],
        out_specs=pl.BlockSpec((tm, d), lambda i, k: (i, 0)),
        out_shape=jax.ShapeDtypeStruct((m, d), F32),
        scratch_shapes=[pltpu.VMEM((tk, d), BF16)],
        compiler_params=_params(2),
        name="ffn_down",
    )(act, w_down, x, mod, final_g)


def _rope_tables(pos):
    inv_freq = ROPE_THETA ** (-jnp.arange(0, ROT_DIM, 2, dtype=F32) / ROT_DIM)
    ang = pos[:, None] * inv_freq[None, :]
    cos, sin = jnp.cos(ang), jnp.sin(ang)
    d = jnp.arange(LANES) % HEAD_DIM
    idx = d % ROT_HALF
    cos_t = jnp.where(d[None, :] < ROT_DIM, cos[:, idx], 1.0)
    sa_t = jnp.where((d[None, :] >= ROT_HALF) & (d[None, :] < ROT_DIM), sin[:, idx], 0.0)
    sb_t = jnp.where(d[None, :] < ROT_HALF, -sin[:, idx], 0.0)
    return cos_t.astype(F32), sa_t.astype(F32), sb_t.astype(F32)


def kernel(x_prompt, x_sample, cache_k, cache_v, state_conv, state_ffn, page_table, c_prompt, c_sample, w_ada, b_ada, norm1_g, w_in, lambda_q1, lambda_k1, lambda_q2, lambda_k2, subln_g, w_attn_out, conv_w, w_conv_out, w_o, norm2_g, w_up, ffn_conv_w, ffn_conv_b, w_down, final_g):
    depth = w_ada.shape[0]
    bp, seq, d = x_prompt.shape
    bs, dec_seq, _ = x_sample.shape
    assert bp == 1 and dec_seq == 1, "one prompt sequence and one new token per sample sequence"
    assert depth == 1, "the final norm is fused into the ConvFFN-down kernel of the only layer"
    n_pool = cache_k.shape[1]
    past_len = page_table.shape[1] * PAGE_SIZE
    qk_w = N_HEADS * QK_DIM
    cw = conv_w.shape[-1]
    f = w_down.shape[1]
    col_conv = 2 * qk_w + ATTN_W
    col_ga = col_conv + 3 * cw
    col_gc = col_ga + d
    tm_p = _pick(seq, (1024, 512, 256))
    tm_s = bs

    xp = x_prompt.reshape(seq, d)
    xs = x_sample.reshape(bs, d)
    pad = (-(bs + bp)) % 16
    c_rows = jnp.concatenate([c_sample, c_prompt, jnp.zeros((pad, d), F32)], axis=0)
    rope_p = _rope_tables(jnp.arange(seq, dtype=F32))
    rope_s = _rope_tables(jnp.full((bs,), past_len, dtype=F32))
    fin_g = final_g.reshape(1, d)

    outs = [[] for _ in range(8)]
    for li in range(depth):
        lam_init = _lambda_init(li)
        mod = _ada(c_rows, w_ada[li], b_ada[li])
        mod_s, mod_p = mod[:bs], mod[bs:bs + bp]
        lams = tuple(a[li].reshape(1, HEAD_DIM) for a in (lambda_q1, lambda_k1, lambda_q2, lambda_k2))
        g1, g2 = norm1_g[li].reshape(1, d), norm2_g[li].reshape(1, d)
        sub_g = subln_g[li].reshape(1, V_DIM)

        h, q, kf, kb, vf, vb = _qkv(xp, g1, mod_p, w_in[li], *rope_p, tm_p)
        u, tail = _sconv(h, w_in[li], conv_w[li], col_conv, tm_p)
        o = _prompt_attention(q, kb, vb, lams, sub_g, lam_init)
        merged = _merge(h, o, u, w_in[li], w_attn_out[li], w_conv_out[li], col_ga, col_gc, tm_p)
        x1 = _resid_proj(merged, w_o[li], xp, mod_p, 2, tm_p)
        act, tg, tv = _ffn_up(x1, g2, mod_p, w_up[li], ffn_conv_w[li], ffn_conv_b[li], tm_p)
        xp = _ffn_down(act, w_down[li], x1, mod_p, fin_g, tm_p)
        keep = SUBLANES - (CONV_K - 1)
        outs[0].append(kf.reshape(bp, seq, N_HEADS, QK_DIM))
        outs[1].append(vf.reshape(bp, seq, N_HEADS, V_DIM))
        outs[2].append(tail[-1, keep:].reshape(bp, CONV_K - 1, cw))
        outs[3].append(jnp.concatenate([tg[-1, keep:], tv[-1, keep:]], axis=-1).reshape(bp, CONV_K - 1, 2 * f))

        sc_prev, sf_prev = state_conv[li], state_ffn[li]
        h, q, kf, kb, vf, vb = _qkv(xs, g1, mod_s, w_in[li], *rope_s, tm_s)
        u, z = _sconv(h, w_in[li], conv_w[li], col_conv, tm_s, state=(sc_prev[:, 0], sc_prev[:, 1]))
        per_head = lambda a: a.reshape(bs, N_HEADS, QK_DIM)
        o = _sample_attention(per_head(q), per_head(kb), per_head(vb), cache_k, cache_v,
                              li, page_table, lams, sub_g, lam_init).reshape(bs, ATTN_W)
        merged = _merge(h, o, u, w_in[li], w_attn_out[li], w_conv_out[li], col_ga, col_gc, tm_s)
        x1 = _resid_proj(merged, w_o[li], xs, mod_s, 2, tm_s)
        act, ug, uv = _ffn_up(x1, g2, mod_s, w_up[li], ffn_conv_w[li], ffn_conv_b[li], tm_s,
                              state=(sf_prev[:, 0], sf_prev[:, 1]))
        xs = _ffn_down(act, w_down[li], x1, mod_s, fin_g, tm_s)
        outs[4].append(kf.reshape(bs, dec_seq, N_HEADS, QK_DIM))
        outs[5].append(vf.reshape(bs, dec_seq, N_HEADS, V_DIM))
        outs[6].append(jnp.stack([sc_prev[:, 1], z], axis=1))
        outs[7].append(jnp.stack([sf_prev[:, 1], jnp.concatenate([ug, uv], axis=-1)], axis=1))

    return (xp.reshape(bp, seq, d), xs.reshape(bs, dec_seq, d),
            *(jnp.stack(o) for o in outs))
```

```python
import functools
import math

import jax
import jax.numpy as jnp
from jax import lax
from jax.experimental import pallas as pl
from jax.experimental.pallas import tpu as pltpu

F32 = jnp.float32
BF16 = jnp.bfloat16

N_HEADS = 8
HEAD_DIM = 64
QK_DIM = 2 * HEAD_DIM
V_DIM = 2 * HEAD_DIM
ROT_DIM = HEAD_DIM // 4
ROT_HALF = ROT_DIM // 2
ROPE_THETA = 500000.0
ATTN_W = N_HEADS * V_DIM
CONV_K = 3
NORM_EPS = 1e-6
SUBLN_EPS = 1e-5
ATTN_SCALE = HEAD_DIM ** -0.5
Q_SCALE = ATTN_SCALE * math.log2(math.e)
PAGE_SIZE = 128
N_MAPS = 2 * N_HEADS

LANES = 128
SUBLANES = 8
BF16_ROWS = 2 * SUBLANES
MXU_DIM = 256
VMEM_LIMIT_BYTES = 56 * 1024 * 1024
ROW_SUB = MXU_DIM
CAST_ROWS = 256
NEG_INF = float("-inf")


def _lambda_init(layer):
    return 0.8 - 0.6 * math.exp(-0.3 * layer)


def _params(n_axes):
    return pltpu.CompilerParams(dimension_semantics=("arbitrary",) * n_axes,
                                vmem_limit_bytes=VMEM_LIMIT_BYTES)


def _sigmoid(x):
    return 1.0 / (1.0 + jnp.exp(-x))


def _bdot(a, b):
    return jnp.dot(a, b, preferred_element_type=F32)


def _pick(n, candidates):
    for c in candidates:
        if n % c == 0:
            return c
    return n


def _row_blocks(tm):
    return [slice(r, min(r + ROW_SUB, tm)) for r in range(0, tm, ROW_SUB)]


def _cast_weight(w_ref, wb_ref):
    k = w_ref.shape[0]
    for r in range(0, k, CAST_ROWS):
        rows = slice(r, min(r + CAST_ROWS, k))
        wb_ref[rows, :] = w_ref[rows, :].astype(BF16)


def _bf16_weights(w_refs, copy_refs):
    if not copy_refs:
        return w_refs
    for w_ref, wb_ref in zip(w_refs, copy_refs, strict=True):
        _cast_weight(w_ref, wb_ref)
    return copy_refs


def _weight_specs(weights, tn, k_tiled=False):
    in_specs, out_specs, out_shapes = [], [], []
    for w, col in weights:
        k, n = w.shape
        if k_tiled:
            in_specs.append(pl.BlockSpec((tn, n), lambda i, j: (j, 0)))
        else:
            in_specs.append(pl.BlockSpec((k, tn), lambda i, j, c=col // tn: (0, c + j)))
    if weights[0][0].dtype == F32:
        for w, _ in weights:
            k, n = w.shape
            if k_tiled:
                out_specs.append(pl.BlockSpec((tn, n), lambda i, j: (j, 0)))
                out_shapes.append(jax.ShapeDtypeStruct((k, n), BF16))
            else:
                out_specs.append(pl.BlockSpec((k, tn), lambda i, j: (0, j)))
    return in_specs, out_specs, out_shapes


def _copy_shapes(weights, width):
    if weights[0][0].dtype != F32:
        return []
    return [jax.ShapeDtypeStruct((w.shape[0], width), BF16) for w, _ in weights]


def _mod_rows(ref, rows):
    return ref[...] if ref.shape[0] == 1 else ref[rows, :]


def _ada_kernel(c_ref, w_ref, b_ref, o_ref, wb_ref):
    c = c_ref[...]
    s = (c * _sigmoid(c)).astype(BF16)
    _cast_weight(w_ref, wb_ref)
    o_ref[...] = _bdot(s, wb_ref[...]) + b_ref[...]


def _ada(c_rows, w_ada, b_ada):
    m, d = c_rows.shape
    n = w_ada.shape[1]
    tn = _pick(n, (1024, 512, 256, 128))
    return pl.pallas_call(
        _ada_kernel,
        grid=(n // tn,),
        in_specs=[pl.BlockSpec((m, d), lambda j: (0, 0)),
                  pl.BlockSpec((d, tn), lambda j: (0, j)),
                  pl.BlockSpec((1, tn), lambda j: (0, j))],
        out_specs=pl.BlockSpec((m, tn), lambda j: (0, j)),
        out_shape=jax.ShapeDtypeStruct((m, n), F32),
        scratch_shapes=[pltpu.VMEM((d, tn), BF16)],
        compiler_params=_params(1),
        name="ada_mod",
    )(c_rows, w_ada, b_ada.reshape(1, n))


def _modulated_rmsnorm(x, g, sc, sh):
    ms = jnp.mean(x * x, axis=-1, keepdims=True)
    y = x * lax.rsqrt(ms + NORM_EPS)
    return y * g * (1.0 + sc) + sh


def _norm_prologue(x_ref, g_ref, sc_ref, sh_ref, h_ref):
    for rows in _row_blocks(x_ref.shape[0]):
        h = _modulated_rmsnorm(x_ref[rows, :], g_ref[...], _mod_rows(sc_ref, rows), _mod_rows(sh_ref, rows))
        h_ref[rows, :] = h.astype(BF16)


def _rope_tile(r, rope_ref, rows):
    cos, sa, sb = rope_ref[0, rows, :], rope_ref[1, rows, :], rope_ref[2, rows, :]
    outs = []
    for hh in range(r.shape[1] // LANES):
        xh = r[:, hh * LANES:(hh + 1) * LANES]
        outs.append(xh * cos + pltpu.roll(xh, ROT_HALF, 1) * sa
                    + pltpu.roll(xh, LANES - ROT_HALF, 1) * sb)
    return jnp.concatenate(outs, axis=1) if len(outs) > 1 else outs[0]


def _qkv_kernel(x_ref, g_ref, sc_ref, sh_ref, w_ref, rope_ref, h_ref, q_ref, k_ref, v_ref,
                *w_copy, nq):
    j = pl.program_id(1)

    @pl.when(j == 0)
    def _():
        _norm_prologue(x_ref, g_ref, sc_ref, sh_ref, h_ref)

    (wb_ref,) = _bf16_weights((w_ref,), w_copy)

    def run(epilogue):
        for rows in _row_blocks(h_ref.shape[0]):
            epilogue(rows, _bdot(h_ref[rows, :], wb_ref[...]))

    @pl.when(j < nq)
    def _():
        def epilogue(rows, res):
            q_ref[rows, :] = (_rope_tile(res, rope_ref, rows) * Q_SCALE).astype(BF16)
        run(epilogue)

    @pl.when((j >= nq) & (j < 2 * nq))
    def _():
        def epilogue(rows, res):
            k_ref[rows, :] = _rope_tile(res, rope_ref, rows)
        run(epilogue)

    @pl.when(j >= 2 * nq)
    def _():
        def epilogue(rows, res):
            v_ref[rows, :] = res
        run(epilogue)


def _qkv(x, g, mod, w, rope, tm):
    m, d = x.shape
    mb = mod.shape[0]
    width = N_HEADS * QK_DIM
    tn = 512 if w.dtype == F32 else width
    nq = width // tn
    weights = [(w, 0)]
    w_in, w_out, _ = _weight_specs(weights, tn)
    row = lambda i, j: (i, 0)
    out_col = lambda lo: (lambda i, j: (i, jnp.clip(j - lo, 0, nq - 1)))
    return pl.pallas_call(
        functools.partial(_qkv_kernel, nq=nq),
        grid=(m // tm, 3 * nq),
        in_specs=[pl.BlockSpec((tm, d), row, pipeline_mode=pl.Buffered(1)),
                  pl.BlockSpec((1, d), lambda i, j: (0, 0)),
                  pl.BlockSpec((mb, d), lambda i, j: (0, 1)),
                  pl.BlockSpec((mb, d), lambda i, j: (0, 0)),
                  *w_in,
                  pl.BlockSpec((3, tm, LANES), lambda i, j: (0, i, 0))],
        out_specs=[pl.BlockSpec((tm, d), row),
                   pl.BlockSpec((tm, tn), out_col(0)),
                   pl.BlockSpec((tm, tn), out_col(nq)),
                   pl.BlockSpec((tm, tn), out_col(2 * nq)),
                   *w_out],
        out_shape=[jax.ShapeDtypeStruct((m, d), BF16),
                   jax.ShapeDtypeStruct((m, width), BF16),
                   jax.ShapeDtypeStruct((m, width), F32),
                   jax.ShapeDtypeStruct((m, width), F32),
                   *_copy_shapes(weights, 3 * width)],
        compiler_params=_params(2),
        name="qkv_proj",
    )(x, g, mod, mod, w, rope)


def _conv_rows(z, prev8, w_ref):
    n = z.shape[0]
    zz = jnp.concatenate([prev8, z], axis=0)
    z1 = zz[SUBLANES - 1:SUBLANES - 1 + n]
    z2 = zz[SUBLANES - 2:SUBLANES - 2 + n]
    return z2 * w_ref[0:1, :] + z1 * w_ref[1:2, :] + z * w_ref[2:3, :]


def _conv_state(p0, p1, z, w_ref):
    return p0 * w_ref[0:1, :] + p1 * w_ref[1:2, :] + z * w_ref[2:3, :]


def _last8(z):
    return z[z.shape[0] - SUBLANES:]


def _sconv_seq_kernel(h_ref, wb_ref, wc_ref, wx_ref, cw_ref, u_ref, tail_ref, carry_ref):
    i, j = pl.program_id(0), pl.program_id(1)

    @pl.when(i == 0)
    def _():
        carry_ref[j] = jnp.zeros(carry_ref.shape[1:], F32)

    prev = carry_ref[j]
    for rows in _row_blocks(h_ref.shape[0]):
        h = h_ref[rows, :]
        z = _bdot(h, wc_ref[...]) * _bdot(h, wx_ref[...])
        u_ref[rows, :] = (_bdot(h, wb_ref[...]) * _conv_rows(z, prev, cw_ref)).astype(BF16)
        prev = _last8(z)
    carry_ref[j] = prev
    tail_ref[0] = prev


def _sconv_state_kernel(h_ref, wb_ref, wc_ref, wx_ref, cw_ref, p0_ref, p1_ref, u_ref, z_ref, *w_copy):
    wbb_ref, wcb_ref, wxb_ref = _bf16_weights((wb_ref, wc_ref, wx_ref), w_copy)
    h = h_ref[...]
    z = _bdot(h, wcb_ref[...]) * _bdot(h, wxb_ref[...])
    conv = _conv_state(p0_ref[...], p1_ref[...], z, cw_ref)
    u_ref[...] = (_bdot(h, wbb_ref[...]) * conv).astype(BF16)
    z_ref[...] = z


def _sconv(h, weights, conv_w, tm, state=None):
    m, d = h.shape
    cw = conv_w.shape[1]
    tn = 512 if state is not None else _pick(cw, (1024, 512))
    nj = cw // tn
    w_in, w_out, _ = _weight_specs(weights, tn)
    tile = pl.BlockSpec((tm, tn), lambda i, j: (i, j))
    common = [pl.BlockSpec((tm, d), lambda i, j: (i, 0)), *w_in,
              pl.BlockSpec((CONV_K, tn), lambda i, j: (0, j))]
    operands = (h, *(w for w, _ in weights), conv_w)
    if state is None:
        return pl.pallas_call(
            _sconv_seq_kernel,
            grid=(m // tm, nj),
            in_specs=common,
            out_specs=[tile, pl.BlockSpec((1, SUBLANES, tn), lambda i, j: (i, 0, j))],
            out_shape=[jax.ShapeDtypeStruct((m, cw), BF16),
                       jax.ShapeDtypeStruct((m // tm, SUBLANES, cw), F32)],
            scratch_shapes=[pltpu.VMEM((nj, SUBLANES, tn), F32)],
            compiler_params=_params(2),
            name="sconv_seq",
        )(*operands)
    return pl.pallas_call(
        _sconv_state_kernel,
        grid=(m // tm, nj),
        in_specs=common + [tile, tile],
        out_specs=[tile, tile, *w_out],
        out_shape=[jax.ShapeDtypeStruct((m, cw), BF16), jax.ShapeDtypeStruct((m, cw), F32),
                   *_copy_shapes(weights, cw)],
        compiler_params=_params(2),
        name="sconv_state",
    )(*operands, *state)


def _lambda_full(lq1_ref, lk1_ref, lq2_ref, lk2_ref, lam_init):
    a = jnp.sum(lq1_ref[...] * lk1_ref[...], axis=-1, keepdims=True)
    b = jnp.sum(lq2_ref[...] * lk2_ref[...], axis=-1, keepdims=True)
    return jnp.exp(a) - jnp.exp(b) + lam_init


def _attn_kernel(lq1_ref, lk1_ref, lq2_ref, lk2_ref, g_ref, q_ref, k_ref, v_ref, o_ref,
                 qst_ref, kb_ref, vt_ref, m_ref, acc_ref, s0_ref, s1_ref, *, bq, lam_init):
    qi = pl.program_id(1)
    t = k_ref.shape[0]
    bk = s0_ref.shape[0]

    @pl.when(qi == 0)
    def _():
        for c in range(t // bq):
            rows = slice(c * bq, (c + 1) * bq)
            kb_ref[rows, :] = k_ref[rows, :].astype(BF16)
            vt_ref[0:V_DIM, rows] = v_ref[rows, :].T.astype(BF16)
        vt_ref[V_DIM:, :] = jnp.ones((vt_ref.shape[0] - V_DIM, t), BF16)

    qt = q_ref[...].astype(F32).T
    dim = lax.broadcasted_iota(jnp.int32, qt.shape, 0)
    qst_ref[:, 0:bq] = jnp.where(dim < HEAD_DIM, qt, 0.0).astype(BF16)
    qst_ref[:, bq:2 * bq] = jnp.where(dim >= HEAD_DIM, qt, 0.0).astype(BF16)
    m_ref[...] = jnp.full(m_ref.shape, NEG_INF, F32)
    acc_ref[...] = jnp.zeros(acc_ref.shape, F32)

    def scores(j, s_ref):
        off = pl.multiple_of(j * bk, bk)
        s_ref[...] = _bdot(kb_ref[pl.ds(off, bk), :], qst_ref[...])

    def softmax_pv(j, s_ref, diag_key0=None):
        off = pl.multiple_of(j * bk, bk)
        st = s_ref[...]
        if diag_key0 is not None:
            key = lax.broadcasted_iota(jnp.int32, st.shape, 0) + diag_key0
            qry = lax.broadcasted_iota(jnp.int32, st.shape, 1)
            qry = jnp.where(qry >= bq, qry - bq, qry)
            st = jnp.where(key <= qry, st, NEG_INF)
        m_prev = m_ref[...]
        m_new = jnp.maximum(m_prev, jnp.max(st, axis=0, keepdims=True))
        alpha = jnp.exp2(m_prev - m_new)
        pt = jnp.exp2(st - m_new)
        acc_ref[...] = alpha * acc_ref[...] + _bdot(vt_ref[:, pl.ds(off, bk)], pt.astype(BF16))
        m_ref[...] = m_new

    def body(jj, carry):
        j = 2 * jj
        scores(j + 1, s1_ref)
        softmax_pv(j, s0_ref)
        scores(j + 2, s0_ref)
        softmax_pv(j + 1, s1_ref)
        return carry

    scores(0, s0_ref)
    lax.fori_loop(0, qi, body, 0)
    scores(2 * qi + 1, s1_ref)
    softmax_pv(2 * qi, s0_ref, diag_key0=0)
    softmax_pv(2 * qi + 1, s1_ref, diag_key0=bk)

    lam = _lambda_full(lq1_ref, lk1_ref, lq2_ref, lk2_ref, lam_init)
    ot = acc_ref[0:V_DIM, :] / acc_ref[V_DIM:V_DIM + 1, :]
    dlt = ot[:, 0:bq] - lam * ot[:, bq:2 * bq]
    ms = jnp.mean(dlt * dlt, axis=0, keepdims=True)
    y = (dlt * lax.rsqrt(ms + SUBLN_EPS)).T * g_ref[...] * (1.0 - lam_init)
    o_ref[...] = y.astype(BF16)


def _prompt_attention(q, k, v, lams, subln_g, lam_init):
    t = q.shape[0]
    bq = _pick(t, (512, 256))
    vec = lambda n: pl.BlockSpec((1, n), lambda h, i: (0, 0))
    return pl.pallas_call(
        functools.partial(_attn_kernel, bq=bq, lam_init=lam_init),
        grid=(N_HEADS, t // bq),
        in_specs=[vec(HEAD_DIM)] * 4 + [vec(V_DIM),
                  pl.BlockSpec((bq, QK_DIM), lambda h, i: (i, h)),
                  pl.BlockSpec((t, QK_DIM), lambda h, i: (0, h)),
                  pl.BlockSpec((t, V_DIM), lambda h, i: (0, h))],
        out_specs=pl.BlockSpec((bq, V_DIM), lambda h, i: (i, h)),
        out_shape=jax.ShapeDtypeStruct((t, ATTN_W), BF16),
        scratch_shapes=[pltpu.VMEM((QK_DIM, 2 * bq), BF16),
                        pltpu.VMEM((t, QK_DIM), BF16),
                        pltpu.VMEM((V_DIM + BF16_ROWS, t), BF16),
                        pltpu.VMEM((1, 2 * bq), F32),
                        pltpu.VMEM((V_DIM + BF16_ROWS, 2 * bq), F32),
                        pltpu.VMEM((bq // 2, 2 * bq), F32),
                        pltpu.VMEM((bq // 2, 2 * bq), F32)],
        compiler_params=_params(2),
        name="prompt_attn",
    )(*lams, subln_g, q, k, v)


def _sattn_kernel(pt_ref, lq1_ref, lk1_ref, lq2_ref, lk2_ref, g_ref, q_ref, kn_ref, vn_ref, *rest,
                  pg, lam_init):
    k_refs, v_refs = rest[:pg], rest[pg:2 * pg]
    o_ref, qm_ref, m_ref, l_ref, acc_ref = rest[2 * pg:]
    step = pl.program_id(1)
    page_rows = PAGE_SIZE * N_HEADS
    lane = lax.broadcasted_iota(jnp.int32, (N_HEADS, QK_DIM), 1)

    @pl.when(step == 0)
    def _():
        q = q_ref[0].astype(F32)
        qm_ref[0:N_HEADS] = jnp.where(lane < HEAD_DIM, q, 0.0)
        qm_ref[N_HEADS:N_MAPS] = jnp.where(lane >= HEAD_DIM, q, 0.0)
        m_ref[...] = jnp.full(m_ref.shape, NEG_INF, F32)
        l_ref[...] = jnp.zeros(l_ref.shape, F32)
        acc_ref[...] = jnp.zeros(acc_ref.shape, F32)

    qm = qm_ref[...]
    qmb = qm.astype(BF16)
    s = jnp.concatenate(
        [lax.dot_general(qmb, k_refs[p][0, 0].reshape(page_rows, QK_DIM).astype(BF16),
                         (((1,), (1,)), ((), ())), preferred_element_type=F32)
         for p in range(pg)], axis=1)
    row = lax.broadcasted_iota(jnp.int32, s.shape, 0)
    col = lax.broadcasted_iota(jnp.int32, s.shape, 1)
    s = jnp.where(col % N_HEADS == row % N_HEADS, s, NEG_INF)
    m_prev = m_ref[...]
    m_new = jnp.maximum(m_prev, jnp.max(s, axis=-1, keepdims=True))
    alpha = jnp.exp2(m_prev - m_new)
    p_all = jnp.exp2(s - m_new)
    l_ref[...] = alpha * l_ref[...] + jnp.sum(p_all, axis=-1, keepdims=True)
    pb = p_all.astype(BF16)
    pv = None
    for p in range(pg):
        part = _bdot(pb[:, p * page_rows:(p + 1) * page_rows],
                     v_refs[p][0, 0].reshape(page_rows, V_DIM).astype(BF16))
        pv = part if pv is None else pv + part
    acc_ref[...] = alpha * acc_ref[...] + pv
    m_ref[...] = m_new

    @pl.when(step == pl.num_programs(1) - 1)
    def _():
        kn = kn_ref[0]
        vn = vn_ref[0]
        kn2 = jnp.concatenate([kn, kn], axis=0)
        vn2 = jnp.concatenate([vn, vn], axis=0)
        s_new = jnp.sum(qm * kn2, axis=-1, keepdims=True)
        m_old = m_ref[...]
        m_fin = jnp.maximum(m_old, s_new)
        a = jnp.exp2(m_old - m_fin)
        p_new = jnp.exp2(s_new - m_fin)
        l_fin = a * l_ref[...] + p_new
        o = (a * acc_ref[...] + p_new * vn2) / l_fin
        lam = _lambda_full(lq1_ref, lk1_ref, lq2_ref, lk2_ref, lam_init)
        dlt = o[0:N_HEADS] - lam * o[N_HEADS:N_MAPS]
        ms = jnp.mean(dlt * dlt, axis=-1, keepdims=True)
        o_ref[0] = dlt * lax.rsqrt(ms + SUBLN_EPS) * g_ref[...] * (1.0 - lam_init)


def _sample_attention(q, k_new, v_new, cache_k, cache_v, layer, page_table, lams, subln_g, lam_init):
    b, n_pages = page_table.shape
    pg = _pick(n_pages, (8, 4, 2, 1))
    vec = lambda n: pl.BlockSpec((1, n), lambda i, s, pt: (0, 0))
    rowspec = pl.BlockSpec((1, N_HEADS, QK_DIM), lambda i, s, pt: (i, 0, 0))
    page = lambda p: pl.BlockSpec((1, 1, PAGE_SIZE, N_HEADS, QK_DIM),
                                  lambda i, s, pt: (layer, pt[i, s * pg + p], 0, 0, 0))
    grid_spec = pltpu.PrefetchScalarGridSpec(
        num_scalar_prefetch=1,
        grid=(b, n_pages // pg),
        in_specs=[vec(HEAD_DIM)] * 4 + [vec(V_DIM), rowspec, rowspec, rowspec]
                 + [page(p) for p in range(pg)] + [page(p) for p in range(pg)],
        out_specs=rowspec,
        scratch_shapes=[pltpu.VMEM((N_MAPS, QK_DIM), F32),
                        pltpu.VMEM((N_MAPS, 1), F32),
                        pltpu.VMEM((N_MAPS, 1), F32),
                        pltpu.VMEM((N_MAPS, V_DIM), F32)],
    )
    return pl.pallas_call(
        functools.partial(_sattn_kernel, pg=pg, lam_init=lam_init),
        grid_spec=grid_spec,
        out_shape=jax.ShapeDtypeStruct((b, N_HEADS, V_DIM), F32),
        compiler_params=_params(2),
        name="sample_attn",
    )(page_table, *lams, subln_g, q, k_new, v_new, *([cache_k] * pg), *([cache_v] * pg))


def _merge_kernel(h_ref, o_ref, u_ref, wga_ref, wgc_ref, wao_ref, wco_ref, out_ref, *w_copy):
    wga, wgc, wao, wco = _bf16_weights((wga_ref, wgc_ref, wao_ref, wco_ref), w_copy)
    for rows in _row_blocks(h_ref.shape[0]):
        h = h_ref[rows, :]
        att = _sigmoid(_bdot(h, wga[...])) * _bdot(o_ref[rows, :].astype(BF16), wao[...])
        conv = _sigmoid(_bdot(h, wgc[...])) * _bdot(u_ref[rows, :], wco[...])
        out_ref[rows, :] = (att + conv).astype(BF16)


def _merge(h, o, u, weights, tm):
    m, d = h.shape
    tn = 256 if weights[0][0].dtype == F32 else _pick(d, (1024, 512))
    w_in, w_out, _ = _weight_specs(weights, tn)
    row = lambda w: pl.BlockSpec((tm, w), lambda i, j: (i, 0))
    return pl.pallas_call(
        _merge_kernel,
        grid=(m // tm, d // tn),
        in_specs=[row(d), row(o.shape[1]), row(u.shape[1]), *w_in],
        out_specs=[pl.BlockSpec((tm, tn), lambda i, j: (i, j)), *w_out],
        out_shape=[jax.ShapeDtypeStruct((m, d), BF16), *_copy_shapes(weights, d)],
        compiler_params=_params(2),
        name="merge",
    )(h, o, u, *(w for w, _ in weights))


def _resid_kernel(a_ref, w_ref, x_ref, g_ref, o_ref, *w_copy):
    (wb_ref,) = _bf16_weights((w_ref,), w_copy)
    for rows in _row_blocks(a_ref.shape[0]):
        o_ref[rows, :] = x_ref[rows, :] + _mod_rows(g_ref, rows) * _bdot(a_ref[rows, :], wb_ref[...])


def _resid_proj(a, w, x, mod, gate_col, tm):
    m, d = x.shape
    k = a.shape[1]
    mb = mod.shape[0]
    tn = 512 if w.dtype == F32 else _pick(d, (1024, 512))
    per = d // tn
    weights = [(w, 0)]
    w_in, w_out, _ = _weight_specs(weights, tn)
    return pl.pallas_call(
        _resid_kernel,
        grid=(m // tm, d // tn),
        in_specs=[pl.BlockSpec((tm, k), lambda i, j: (i, 0)),
                  *w_in,
                  pl.BlockSpec((tm, tn), lambda i, j: (i, j)),
                  pl.BlockSpec((mb, tn), lambda i, j: (0, gate_col * per + j))],
        out_specs=[pl.BlockSpec((tm, tn), lambda i, j: (i, j)), *w_out],
        out_shape=[jax.ShapeDtypeStruct((m, d), F32), *_copy_shapes(weights, d)],
        compiler_params=_params(2),
        name="resid_proj",
    )(a, w, x, mod)


def _ffn_up_seq_kernel(x_ref, g_ref, sc_ref, sh_ref, wg_ref, wv_ref, cwg_ref, cwv_ref,
                       bg_ref, bv_ref, act_ref, tg_ref, tv_ref, h_ref, carry_ref):
    i, j = pl.program_id(0), pl.program_id(1)

    @pl.when(j == 0)
    def _():
        _norm_prologue(x_ref, g_ref, sc_ref, sh_ref, h_ref)

    @pl.when(i == 0)
    def _():
        carry_ref[j] = jnp.zeros(carry_ref.shape[1:], F32)

    prev_g, prev_v = carry_ref[j, 0], carry_ref[j, 1]
    for rows in _row_blocks(h_ref.shape[0]):
        h = h_ref[rows, :]
        ug, uv = _bdot(h, wg_ref[...]), _bdot(h, wv_ref[...])
        gate = _conv_rows(ug, prev_g, cwg_ref) + bg_ref[...]
        val = _conv_rows(uv, prev_v, cwv_ref) + bv_ref[...]
        act_ref[rows, :] = (gate * _sigmoid(gate) * val).astype(BF16)
        prev_g, prev_v = _last8(ug), _last8(uv)
    carry_ref[j, 0] = prev_g
    carry_ref[j, 1] = prev_v
    tg_ref[0] = prev_g
    tv_ref[0] = prev_v


def _ffn_up_state_kernel(x_ref, g_ref, sc_ref, sh_ref, wg_ref, wv_ref, cwg_ref, cwv_ref,
                         bg_ref, bv_ref, pg0_ref, pg1_ref, pv0_ref, pv1_ref,
                         act_ref, ug_ref, uv_ref, wgb_ref, wvb_ref, h_ref):
    @pl.when(pl.program_id(1) == 0)
    def _():
        _norm_prologue(x_ref, g_ref, sc_ref, sh_ref, h_ref)

    _bf16_weights((wg_ref, wv_ref), (wgb_ref, wvb_ref))
    h = h_ref[...]
    ug, uv = _bdot(h, wgb_ref[...]), _bdot(h, wvb_ref[...])
    gate = _conv_state(pg0_ref[...], pg1_ref[...], ug, cwg_ref) + bg_ref[...]
    val = _conv_state(pv0_ref[...], pv1_ref[...], uv, cwv_ref) + bv_ref[...]
    act_ref[...] = (gate * _sigmoid(gate) * val).astype(BF16)
    ug_ref[...] = ug
    uv_ref[...] = uv


def _ffn_up(x, g, mod, weights, conv_w, conv_b, tm, state=None):
    m, d = x.shape
    mb = mod.shape[0]
    f = conv_w.shape[1] // 2
    tn = _pick(f, (512, 256)) if state is not None else _pick(f, (1408, 512, 256))
    nj = f // tn
    w_in, w_out, _ = _weight_specs(weights, tn)
    row = lambda i, j: (i, 0)
    lo = lambda i, j: (0, j)
    hi = lambda i, j: (0, nj + j)
    tile = pl.BlockSpec((tm, tn), lambda i, j: (i, j))
    tile_hi = pl.BlockSpec((tm, tn), lambda i, j: (i, nj + j))
    conv_b = conv_b.reshape(1, 2 * f)
    common = [pl.BlockSpec((tm, d), row, pipeline_mode=pl.Buffered(1)),
              pl.BlockSpec((1, d), lambda i, j: (0, 0)),
              pl.BlockSpec((mb, d), lambda i, j: (0, 4)),
              pl.BlockSpec((mb, d), lambda i, j: (0, 3)),
              *w_in,
              pl.BlockSpec((CONV_K, tn), lo), pl.BlockSpec((CONV_K, tn), hi),
              pl.BlockSpec((1, tn), lo), pl.BlockSpec((1, tn), hi)]
    args = (x, g, mod, mod, *(w for w, _ in weights), conv_w, conv_w, conv_b, conv_b)
    if state is None:
        tail = pl.BlockSpec((1, SUBLANES, tn), lambda i, j: (i, 0, j))
        return pl.pallas_call(
            _ffn_up_seq_kernel,
            grid=(m // tm, nj),
            in_specs=common,
            out_specs=[tile, tail, tail],
            out_shape=[jax.ShapeDtypeStruct((m, f), BF16),
                       jax.ShapeDtypeStruct((m // tm, SUBLANES, f), F32),
                       jax.ShapeDtypeStruct((m // tm, SUBLANES, f), F32)],
            scratch_shapes=[pltpu.VMEM((tm, d), BF16), pltpu.VMEM((nj, 2, SUBLANES, tn), F32)],
            compiler_params=_params(2),
            name="ffn_up_seq",
        )(*args)
    p0, p1 = state
    return pl.pallas_call(
        _ffn_up_state_kernel,
        grid=(m // tm, nj),
        in_specs=common + [tile, tile, tile_hi, tile_hi],
        out_specs=[tile, tile, tile, *w_out],
        out_shape=[jax.ShapeDtypeStruct((m, f), BF16),
                   jax.ShapeDtypeStruct((m, f), F32),
                   jax.ShapeDtypeStruct((m, f), F32),
                   *_copy_shapes(weights, f)],
        scratch_shapes=[pltpu.VMEM((tm, d), BF16)],
        compiler_params=_params(2),
        name="ffn_up_state",
    )(*args, p0, p1, p0, p1)


def _ffn_down_kernel(a_ref, w_ref, x_ref, g2_ref, fg_ref, y_ref, *w_copy):
    kk = pl.program_id(1)
    last = pl.num_programs(1) - 1
    (wb_ref,) = _bf16_weights((w_ref,), w_copy)
    blocks = _row_blocks(a_ref.shape[0])

    @pl.when(kk == 0)
    def _():
        for rows in blocks:
            y_ref[rows, :] = _bdot(a_ref[rows, :], wb_ref[...])

    @pl.when((kk > 0) & (kk < last))
    def _():
        for rows in blocks:
            y_ref[rows, :] += _bdot(a_ref[rows, :], wb_ref[...])

    @pl.when(kk == last)
    def _():
        for rows in blocks:
            acc = y_ref[rows, :] + _bdot(a_ref[rows, :], wb_ref[...])
            x = x_ref[rows, :] + _mod_rows(g2_ref, rows) * acc
            ms = jnp.mean(x * x, axis=-1, keepdims=True)
            y_ref[rows, :] = x * lax.rsqrt(ms + NORM_EPS) * fg_ref[...]


def _ffn_down(act, w_down, x, mod, final_g, tm):
    m, d = x.shape
    f = act.shape[1]
    mb = mod.shape[0]
    tk = _pick(f, (512, 256)) if w_down.dtype == F32 else _pick(f, (1408, 512, 256))
    assert f // tk >= 2, "the accumulate-in-output schedule needs at least two K steps"
    weights = [(w_down, 0)]
    w_in, w_out, w_shapes = _weight_specs(weights, tk, k_tiled=True)
    return pl.pallas_call(
        _ffn_down_kernel,
        grid=(m // tm, f // tk),
        in_specs=[pl.BlockSpec((tm, tk), lambda i, k: (i, k)),
                  *w_in,
                  pl.BlockSpec((tm, d), lambda i, k: (i, 0), pipeline_mode=pl.Buffered(1)),
                  pl.BlockSpec((mb, d), lambda i, k: (0, 5)),
                  pl.BlockSpec((1, d), lambda i, k: (0, 0))],
        out_specs=[pl.BlockSpec((tm, d), lambda i, k: (i, 0)), *w_out],
        out_shape=[jax.ShapeDtypeStruct((m, d), F32), *w_shapes],
        compiler_params=_params(2),
        name="ffn_down",
    )(act, w_down, x, mod, final_g)


def _rope_tables(pos):
    inv_freq = ROPE_THETA ** (-jnp.arange(0, ROT_DIM, 2, dtype=F32) / ROT_DIM)
    ang = pos[:, None] * inv_freq[None, :]
    cos, sin = jnp.cos(ang), jnp.sin(ang)
    d = jnp.arange(LANES) % HEAD_DIM
    idx = d % ROT_HALF
    cos_t = jnp.where(d[None, :] < ROT_DIM, cos[:, idx], 1.0)
    sa_t = jnp.where((d[None, :] >= ROT_HALF) & (d[None, :] < ROT_DIM), sin[:, idx], 0.0)
    sb_t = jnp.where(d[None, :] < ROT_HALF, -sin[:, idx], 0.0)
    return jnp.stack([cos_t, sa_t, sb_t]).astype(F32)


def kernel(x_prompt, x_sample, cache_k, cache_v, state_conv, state_ffn, page_table, c_prompt, c_sample, w_ada, b_ada, norm1_g, w_in, lambda_q1, lambda_k1, lambda_q2, lambda_k2, subln_g, w_attn_out, conv_w, w_conv_out, w_o, norm2_g, w_up, ffn_conv_w, ffn_conv_b, w_down, final_g):
    depth = w_ada.shape[0]
    bp, seq, d = x_prompt.shape
    bs, dec_seq, _ = x_sample.shape
    assert bp == 1 and dec_seq == 1, "one prompt sequence and one new token per sample sequence"
    assert depth == 1, "the final norm is fused into the ConvFFN-down kernel of the only layer"
    past_len = page_table.shape[1] * PAGE_SIZE
    qk_w = N_HEADS * QK_DIM
    cw = conv_w.shape[-1]
    f = w_down.shape[1]
    col_conv = 2 * qk_w + ATTN_W
    col_ga = col_conv + 3 * cw
    col_gc = col_ga + d
    tm_p = _pick(seq, (1024, 512, 256))
    tm_s = bs

    xp = x_prompt.reshape(seq, d)
    xs = x_sample.reshape(bs, d)
    pad = (-(bs + bp)) % BF16_ROWS
    c_rows = jnp.concatenate([c_sample, c_prompt, jnp.zeros((pad, d), F32)], axis=0)
    rope_p = _rope_tables(jnp.arange(seq, dtype=F32))
    rope_s = _rope_tables(jnp.full((bs,), past_len, dtype=F32))
    fin_g = final_g.reshape(1, d)
    keep = SUBLANES - (CONV_K - 1)

    outs = [[] for _ in range(8)]
    for li in range(depth):
        lam_init = _lambda_init(li)
        mod = _ada(c_rows, w_ada[li], b_ada[li])
        mod_s, mod_p = mod[:bs], mod[bs:bs + bp]
        lams = tuple(a[li].reshape(1, HEAD_DIM) for a in (lambda_q1, lambda_k1, lambda_q2, lambda_k2))
        g1, g2 = norm1_g[li].reshape(1, d), norm2_g[li].reshape(1, d)
        sub_g = subln_g[li].reshape(1, V_DIM)
        w_in_l, w_up_l = w_in[li], w_up[li]

        sc_prev, sf_prev = state_conv[li], state_ffn[li]
        h, q, k, v, wqkv_b = _qkv(xs, g1, mod_s, w_in_l, rope_s, tm_s)
        u, z, wcb_b, wcc_b, wcx_b = _sconv(
            h, [(w_in_l, col_conv), (w_in_l, col_conv + cw), (w_in_l, col_conv + 2 * cw)],
            conv_w[li], tm_s, state=(sc_prev[:, 0], sc_prev[:, 1]))
        per_head = lambda a: a.reshape(bs, N_HEADS, QK_DIM)
        o = _sample_attention(per_head(q), per_head(k), per_head(v), cache_k, cache_v,
                              li, page_table, lams, sub_g, lam_init).reshape(bs, ATTN_W)
        merged, wga_b, wgc_b, wao_b, wco_b = _merge(
            h, o, u, [(w_in_l, col_ga), (w_in_l, col_gc), (w_attn_out[li], 0), (w_conv_out[li], 0)], tm_s)
        x1, wo_b = _resid_proj(merged, w_o[li], xs, mod_s, 2, tm_s)
        act, ug, uv, wug_b, wuv_b = _ffn_up(x1, g2, mod_s, [(w_up_l, 0), (w_up_l, f)], ffn_conv_w[li],
                                            ffn_conv_b[li], tm_s, state=(sf_prev[:, 0], sf_prev[:, 1]))
        xs, wdn_b = _ffn_down(act, w_down[li], x1, mod_s, fin_g, tm_s)
        outs[4].append(k.reshape(bs, dec_seq, N_HEADS, QK_DIM))
        outs[5].append(v.reshape(bs, dec_seq, N_HEADS, V_DIM))
        outs[6].append(jnp.stack([sc_prev[:, 1], z], axis=1))
        outs[7].append(jnp.stack([sf_prev[:, 1], jnp.concatenate([ug, uv], axis=-1)], axis=1))

        h, q, k, v = _qkv(xp, g1, mod_p, wqkv_b, rope_p, tm_p)
        u, tail = _sconv(h, [(wcb_b, 0), (wcc_b, 0), (wcx_b, 0)], conv_w[li], tm_p)
        o = _prompt_attention(q, k, v, lams, sub_g, lam_init)
        (merged,) = _merge(h, o, u, [(wga_b, 0), (wgc_b, 0), (wao_b, 0), (wco_b, 0)], tm_p)
        (x1,) = _resid_proj(merged, wo_b, xp, mod_p, 2, tm_p)
        act, tg, tv = _ffn_up(x1, g2, mod_p, [(wug_b, 0), (wuv_b, 0)], ffn_conv_w[li],
                              ffn_conv_b[li], tm_p)
        (xp,) = _ffn_down(act, wdn_b, x1, mod_p, fin_g, tm_p)
        outs[0].append(k.reshape(bp, seq, N_HEADS, QK_DIM))
        outs[1].append(v.reshape(bp, seq, N_HEADS, V_DIM))
        outs[2].append(tail[-1, keep:].reshape(bp, CONV_K - 1, cw))
        outs[3].append(jnp.concatenate([tg[-1, keep:], tv[-1, keep:]], axis=-1).reshape(bp, CONV_K - 1, 2 * f))

    return (xp.reshape(bp, seq, d), xs.reshape(bs, dec_seq, d),
            *(jnp.stack(o) for o in outs))
```

```python
import functools
import math

import jax
import jax.numpy as jnp
from jax import lax
from jax.experimental import pallas as pl
from jax.experimental.pallas import tpu as pltpu

F32 = jnp.float32
BF16 = jnp.bfloat16

N_HEADS = 8
HEAD_DIM = 64
QK_DIM = 2 * HEAD_DIM
V_DIM = 2 * HEAD_DIM
ROT_DIM = HEAD_DIM // 4
ROT_HALF = ROT_DIM // 2
ROPE_THETA = 500000.0
ATTN_W = N_HEADS * V_DIM
CONV_K = 3
NORM_EPS = 1e-6
SUBLN_EPS = 1e-5
ATTN_SCALE = HEAD_DIM ** -0.5
Q_SCALE = ATTN_SCALE * math.log2(math.e)
PAGE_SIZE = 128
N_MAPS = 2 * N_HEADS

LANES = 128
SUBLANES = 8
BF16_ROWS = 2 * SUBLANES
MXU_DIM = 256
VMEM_LIMIT_BYTES = 56 * 1024 * 1024
ROW_SUB = MXU_DIM
CAST_ROWS = 256
NEG_INF = float("-inf")


def _lambda_init(layer):
    return 0.8 - 0.6 * math.exp(-0.3 * layer)


def _params(n_axes):
    return pltpu.CompilerParams(dimension_semantics=("arbitrary",) * n_axes,
                                vmem_limit_bytes=VMEM_LIMIT_BYTES)


def _sigmoid(x):
    return 1.0 / (1.0 + jnp.exp(-x))


def _bdot(a, b):
    return jnp.dot(a, b, preferred_element_type=F32)


def _pick(n, candidates):
    for c in candidates:
        if n % c == 0:
            return c
    return n


def _row_blocks(tm):
    return [slice(r, min(r + ROW_SUB, tm)) for r in range(0, tm, ROW_SUB)]


def _cast_weight(w_ref, wb_ref):
    k = w_ref.shape[0]
    for r in range(0, k, CAST_ROWS):
        rows = slice(r, min(r + CAST_ROWS, k))
        wb_ref[rows, :] = w_ref[rows, :].astype(BF16)


def _bf16_weights(w_refs, copy_refs):
    if not copy_refs:
        return w_refs
    for w_ref, wb_ref in zip(w_refs, copy_refs, strict=True):
        _cast_weight(w_ref, wb_ref)
    return copy_refs


def _weight_specs(weights, tn, k_tiled=False):
    in_specs, out_specs, out_shapes = [], [], []
    for w, col in weights:
        k, n = w.shape
        if k_tiled:
            in_specs.append(pl.BlockSpec((tn, n), lambda i, j: (j, 0)))
        else:
            in_specs.append(pl.BlockSpec((k, tn), lambda i, j, c=col // tn: (0, c + j)))
    if weights[0][0].dtype == F32:
        for w, _ in weights:
            k, n = w.shape
            if k_tiled:
                out_specs.append(pl.BlockSpec((tn, n), lambda i, j: (j, 0)))
                out_shapes.append(jax.ShapeDtypeStruct((k, n), BF16))
            else:
                out_specs.append(pl.BlockSpec((k, tn), lambda i, j: (0, j)))
    return in_specs, out_specs, out_shapes


def _copy_shapes(weights, width):
    if weights[0][0].dtype != F32:
        return []
    return [jax.ShapeDtypeStruct((w.shape[0], width), BF16) for w, _ in weights]


def _mod_rows(ref, rows):
    return ref[...] if ref.shape[0] == 1 else ref[rows, :]


def _ada_kernel(c_ref, w_ref, b_ref, o_ref, wb_ref):
    c = c_ref[...]
    s = (c * _sigmoid(c)).astype(BF16)
    _cast_weight(w_ref, wb_ref)
    o_ref[...] = _bdot(s, wb_ref[...]) + b_ref[...]


def _ada(c_rows, w_ada, b_ada):
    m, d = c_rows.shape
    n = w_ada.shape[1]
    tn = _pick(n, (1024, 512, 256, 128))
    return pl.pallas_call(
        _ada_kernel,
        grid=(n // tn,),
        in_specs=[pl.BlockSpec((m, d), lambda j: (0, 0)),
                  pl.BlockSpec((d, tn), lambda j: (0, j)),
                  pl.BlockSpec((1, tn), lambda j: (0, j))],
        out_specs=pl.BlockSpec((m, tn), lambda j: (0, j)),
        out_shape=jax.ShapeDtypeStruct((m, n), F32),
        scratch_shapes=[pltpu.VMEM((d, tn), BF16)],
        compiler_params=_params(1),
        name="ada_mod",
    )(c_rows, w_ada, b_ada.reshape(1, n))


def _modulated_rmsnorm(x, g, sc, sh):
    ms = jnp.mean(x * x, axis=-1, keepdims=True)
    y = x * lax.rsqrt(ms + NORM_EPS)
    return y * g * (1.0 + sc) + sh


def _norm_prologue(x_ref, g_ref, sc_ref, sh_ref, h_ref):
    for rows in _row_blocks(x_ref.shape[0]):
        h = _modulated_rmsnorm(x_ref[rows, :], g_ref[...], _mod_rows(sc_ref, rows), _mod_rows(sh_ref, rows))
        h_ref[rows, :] = h.astype(BF16)


def _rope_tile(r, rope_ref, rows):
    cos, sa, sb = rope_ref[0, rows, :], rope_ref[1, rows, :], rope_ref[2, rows, :]
    outs = []
    for hh in range(r.shape[1] // LANES):
        xh = r[:, hh * LANES:(hh + 1) * LANES]
        outs.append(xh * cos + pltpu.roll(xh, ROT_HALF, 1) * sa
                    + pltpu.roll(xh, LANES - ROT_HALF, 1) * sb)
    return jnp.concatenate(outs, axis=1) if len(outs) > 1 else outs[0]


def _qkv_kernel(x_ref, g_ref, sc_ref, sh_ref, w_ref, rope_ref, h_ref, q_ref, k_ref, v_ref,
                *w_copy, nq):
    j = pl.program_id(1)

    @pl.when(j == 0)
    def _():
        _norm_prologue(x_ref, g_ref, sc_ref, sh_ref, h_ref)

    (wb_ref,) = _bf16_weights((w_ref,), w_copy)

    def run(epilogue):
        for rows in _row_blocks(h_ref.shape[0]):
            epilogue(rows, _bdot(h_ref[rows, :], wb_ref[...]))

    @pl.when(j < nq)
    def _():
        def epilogue(rows, res):
            q_ref[rows, :] = (_rope_tile(res, rope_ref, rows) * Q_SCALE).astype(BF16)
        run(epilogue)

    @pl.when((j >= nq) & (j < 2 * nq))
    def _():
        def epilogue(rows, res):
            k_ref[rows, :] = _rope_tile(res, rope_ref, rows)
        run(epilogue)

    @pl.when(j >= 2 * nq)
    def _():
        def epilogue(rows, res):
            v_ref[rows, :] = res
        run(epilogue)


def _qkv(x, g, mod, w, rope, tm):
    m, d = x.shape
    mb = mod.shape[0]
    width = N_HEADS * QK_DIM
    tn = 512 if w.dtype == F32 else width
    nq = width // tn
    weights = [(w, 0)]
    w_in, w_out, _ = _weight_specs(weights, tn)
    row = lambda i, j: (i, 0)
    out_col = lambda lo: (lambda i, j: (i, jnp.clip(j - lo, 0, nq - 1)))
    return pl.pallas_call(
        functools.partial(_qkv_kernel, nq=nq),
        grid=(m // tm, 3 * nq),
        in_specs=[pl.BlockSpec((tm, d), row, pipeline_mode=pl.Buffered(1)),
                  pl.BlockSpec((1, d), lambda i, j: (0, 0)),
                  pl.BlockSpec((mb, d), lambda i, j: (0, 1)),
                  pl.BlockSpec((mb, d), lambda i, j: (0, 0)),
                  *w_in,
                  pl.BlockSpec((3, tm, LANES), lambda i, j: (0, i, 0))],
        out_specs=[pl.BlockSpec((tm, d), row),
                   pl.BlockSpec((tm, tn), out_col(0)),
                   pl.BlockSpec((tm, tn), out_col(nq)),
                   pl.BlockSpec((tm, tn), out_col(2 * nq)),
                   *w_out],
        out_shape=[jax.ShapeDtypeStruct((m, d), BF16),
                   jax.ShapeDtypeStruct((m, width), BF16),
                   jax.ShapeDtypeStruct((m, width), F32),
                   jax.ShapeDtypeStruct((m, width), F32),
                   *_copy_shapes(weights, 3 * width)],
        compiler_params=_params(2),
        name="qkv_proj",
    )(x, g, mod, mod, w, rope)


def _conv_rows(z, prev8, w_ref):
    n = z.shape[0]
    zz = jnp.concatenate([prev8, z], axis=0)
    z1 = zz[SUBLANES - 1:SUBLANES - 1 + n]
    z2 = zz[SUBLANES - 2:SUBLANES - 2 + n]
    return z2 * w_ref[0:1, :] + z1 * w_ref[1:2, :] + z * w_ref[2:3, :]


def _conv_state(p0, p1, z, w_ref):
    return p0 * w_ref[0:1, :] + p1 * w_ref[1:2, :] + z * w_ref[2:3, :]


def _last8(z):
    return z[z.shape[0] - SUBLANES:]


def _sconv_seq_kernel(h_ref, wb_ref, wc_ref, wx_ref, cw_ref, u_ref, tail_ref, carry_ref):
    i, j = pl.program_id(0), pl.program_id(1)

    @pl.when(i == 0)
    def _():
        carry_ref[j] = jnp.zeros(carry_ref.shape[1:], F32)

    prev = carry_ref[j]
    for rows in _row_blocks(h_ref.shape[0]):
        h = h_ref[rows, :]
        z = _bdot(h, wc_ref[...]) * _bdot(h, wx_ref[...])
        u_ref[rows, :] = (_bdot(h, wb_ref[...]) * _conv_rows(z, prev, cw_ref)).astype(BF16)
        prev = _last8(z)
    carry_ref[j] = prev
    tail_ref[0] = prev


def _sconv_state_kernel(h_ref, wb_ref, wc_ref, wx_ref, cw_ref, p0_ref, p1_ref, u_ref, z_ref, *w_copy):
    wbb_ref, wcb_ref, wxb_ref = _bf16_weights((wb_ref, wc_ref, wx_ref), w_copy)
    h = h_ref[...]
    z = _bdot(h, wcb_ref[...]) * _bdot(h, wxb_ref[...])
    conv = _conv_state(p0_ref[...], p1_ref[...], z, cw_ref)
    u_ref[...] = (_bdot(h, wbb_ref[...]) * conv).astype(BF16)
    z_ref[...] = z


def _sconv(h, weights, conv_w, tm, state=None):
    m, d = h.shape
    cw = conv_w.shape[1]
    tn = 512 if state is not None else _pick(cw, (1024, 512))
    nj = cw // tn
    w_in, w_out, _ = _weight_specs(weights, tn)
    tile = pl.BlockSpec((tm, tn), lambda i, j: (i, j))
    common = [pl.BlockSpec((tm, d), lambda i, j: (i, 0)), *w_in,
              pl.BlockSpec((CONV_K, tn), lambda i, j: (0, j))]
    operands = (h, *(w for w, _ in weights), conv_w)
    if state is None:
        return pl.pallas_call(
            _sconv_seq_kernel,
            grid=(m // tm, nj),
            in_specs=common,
            out_specs=[tile, pl.BlockSpec((1, SUBLANES, tn), lambda i, j: (i, 0, j))],
            out_shape=[jax.ShapeDtypeStruct((m, cw), BF16),
                       jax.ShapeDtypeStruct((m // tm, SUBLANES, cw), F32)],
            scratch_shapes=[pltpu.VMEM((nj, SUBLANES, tn), F32)],
            compiler_params=_params(2),
            name="sconv_seq",
        )(*operands)
    return pl.pallas_call(
        _sconv_state_kernel,
        grid=(m // tm, nj),
        in_specs=common + [tile, tile],
        out_specs=[tile, tile, *w_out],
        out_shape=[jax.ShapeDtypeStruct((m, cw), BF16), jax.ShapeDtypeStruct((m, cw), F32),
                   *_copy_shapes(weights, cw)],
        compiler_params=_params(2),
        name="sconv_state",
    )(*operands, *state)


def _lambda_full(lq1_ref, lk1_ref, lq2_ref, lk2_ref, lam_init):
    a = jnp.sum(lq1_ref[...] * lk1_ref[...], axis=-1, keepdims=True)
    b = jnp.sum(lq2_ref[...] * lk2_ref[...], axis=-1, keepdims=True)
    return jnp.exp(a) - jnp.exp(b) + lam_init


def _attn_kernel(lq1_ref, lk1_ref, lq2_ref, lk2_ref, g_ref, q_ref, k_ref, v_ref, o_ref,
                 qst_ref, kb_ref, vt_ref, m_ref, acc_ref, s0_ref, s1_ref, *, bq, lam_init):
    qi = pl.program_id(1)
    t = k_ref.shape[0]
    bk = s0_ref.shape[0]

    @pl.when(qi == 0)
    def _():
        for c in range(t // bq):
            rows = slice(c * bq, (c + 1) * bq)
            kb_ref[rows, :] = k_ref[rows, :].astype(BF16)
            vt_ref[0:V_DIM, rows] = v_ref[rows, :].T.astype(BF16)
        vt_ref[V_DIM:, :] = jnp.ones((vt_ref.shape[0] - V_DIM, t), BF16)

    qt = q_ref[...].astype(F32).T
    dim = lax.broadcasted_iota(jnp.int32, qt.shape, 0)
    qst_ref[:, 0:bq] = jnp.where(dim < HEAD_DIM, qt, 0.0).astype(BF16)
    qst_ref[:, bq:2 * bq] = jnp.where(dim >= HEAD_DIM, qt, 0.0).astype(BF16)
    m_ref[...] = jnp.full(m_ref.shape, NEG_INF, F32)
    acc_ref[...] = jnp.zeros(acc_ref.shape, F32)

    def scores(j, s_ref):
        off = pl.multiple_of(j * bk, bk)
        s_ref[...] = _bdot(kb_ref[pl.ds(off, bk), :], qst_ref[...])

    def softmax_pv(j, s_ref, diag_key0=None):
        off = pl.multiple_of(j * bk, bk)
        st = s_ref[...]
        if diag_key0 is not None:
            key = lax.broadcasted_iota(jnp.int32, st.shape, 0) + diag_key0
            qry = lax.broadcasted_iota(jnp.int32, st.shape, 1)
            qry = jnp.where(qry >= bq, qry - bq, qry)
            st = jnp.where(key <= qry, st, NEG_INF)
        m_prev = m_ref[...]
        m_new = jnp.maximum(m_prev, jnp.max(st, axis=0, keepdims=True))
        alpha = jnp.exp2(m_prev - m_new)
        pt = jnp.exp2(st - m_new)
        acc_ref[...] = alpha * acc_ref[...] + _bdot(vt_ref[:, pl.ds(off, bk)], pt.astype(BF16))
        m_ref[...] = m_new

    def body(jj, carry):
        j = 2 * jj
        scores(j + 1, s1_ref)
        softmax_pv(j, s0_ref)
        scores(j + 2, s0_ref)
        softmax_pv(j + 1, s1_ref)
        return carry

    scores(0, s0_ref)
    lax.fori_loop(0, qi, body, 0)
    scores(2 * qi + 1, s1_ref)
    softmax_pv(2 * qi, s0_ref, diag_key0=0)
    softmax_pv(2 * qi + 1, s1_ref, diag_key0=bk)

    lam = _lambda_full(lq1_ref, lk1_ref, lq2_ref, lk2_ref, lam_init)
    ot = acc_ref[0:V_DIM, :] / acc_ref[V_DIM:V_DIM + 1, :]
    dlt = ot[:, 0:bq] - lam * ot[:, bq:2 * bq]
    ms = jnp.mean(dlt * dlt, axis=0, keepdims=True)
    y = (dlt * lax.rsqrt(ms + SUBLN_EPS)).T * g_ref[...] * (1.0 - lam_init)
    o_ref[...] = y.astype(BF16)


def _prompt_attention(q, k, v, lams, subln_g, lam_init):
    t = q.shape[0]
    bq = _pick(t, (512, 256))
    vec = lambda n: pl.BlockSpec((1, n), lambda h, i: (0, 0))
    return pl.pallas_call(
        functools.partial(_attn_kernel, bq=bq, lam_init=lam_init),
        grid=(N_HEADS, t // bq),
        in_specs=[vec(HEAD_DIM)] * 4 + [vec(V_DIM),
                  pl.BlockSpec((bq, QK_DIM), lambda h, i: (i, h)),
                  pl.BlockSpec((t, QK_DIM), lambda h, i: (0, h)),
                  pl.BlockSpec((t, V_DIM), lambda h, i: (0, h))],
        out_specs=pl.BlockSpec((bq, V_DIM), lambda h, i: (i, h)),
        out_shape=jax.ShapeDtypeStruct((t, ATTN_W), BF16),
        scratch_shapes=[pltpu.VMEM((QK_DIM, 2 * bq), BF16),
                        pltpu.VMEM((t, QK_DIM), BF16),
                        pltpu.VMEM((V_DIM + BF16_ROWS, t), BF16),
                        pltpu.VMEM((1, 2 * bq), F32),
                        pltpu.VMEM((V_DIM + BF16_ROWS, 2 * bq), F32),
                        pltpu.VMEM((bq // 2, 2 * bq), F32),
                        pltpu.VMEM((bq // 2, 2 * bq), F32)],
        compiler_params=_params(2),
        name="prompt_attn",
    )(*lams, subln_g, q, k, v)


def _sattn_kernel(pt_ref, lq1_ref, lk1_ref, lq2_ref, lk2_ref, g_ref, q_ref, kn_ref, vn_ref, *rest,
                  pg, lam_init):
    k_refs, v_refs = rest[:pg], rest[pg:2 * pg]
    o_ref, qm_ref, m_ref, l_ref, acc_ref = rest[2 * pg:]
    step = pl.program_id(1)
    page_rows = PAGE_SIZE * N_HEADS
    lane = lax.broadcasted_iota(jnp.int32, (N_HEADS, QK_DIM), 1)

    @pl.when(step == 0)
    def _():
        q = q_ref[0].astype(F32)
        qm_ref[0:N_HEADS] = jnp.where(lane < HEAD_DIM, q, 0.0)
        qm_ref[N_HEADS:N_MAPS] = jnp.where(lane >= HEAD_DIM, q, 0.0)
        m_ref[...] = jnp.full(m_ref.shape, NEG_INF, F32)
        l_ref[...] = jnp.zeros(l_ref.shape, F32)
        acc_ref[...] = jnp.zeros(acc_ref.shape, F32)

    qm = qm_ref[...]
    qmb = qm.astype(BF16)
    s = jnp.concatenate(
        [lax.dot_general(qmb, k_refs[p][0, 0].reshape(page_rows, QK_DIM).astype(BF16),
                         (((1,), (1,)), ((), ())), preferred_element_type=F32)
         for p in range(pg)], axis=1)
    row = lax.broadcasted_iota(jnp.int32, s.shape, 0)
    col = lax.broadcasted_iota(jnp.int32, s.shape, 1)
    s = jnp.where(col % N_HEADS == row % N_HEADS, s, NEG_INF)
    m_prev = m_ref[...]
    m_new = jnp.maximum(m_prev, jnp.max(s, axis=-1, keepdims=True))
    alpha = jnp.exp2(m_prev - m_new)
    p_all = jnp.exp2(s - m_new)
    l_ref[...] = alpha * l_ref[...] + jnp.sum(p_all, axis=-1, keepdims=True)
    pb = p_all.astype(BF16)
    pv = None
    for p in range(pg):
        part = _bdot(pb[:, p * page_rows:(p + 1) * page_rows],
                     v_refs[p][0, 0].reshape(page_rows, V_DIM).astype(BF16))
        pv = part if pv is None else pv + part
    acc_ref[...] = alpha * acc_ref[...] + pv
    m_ref[...] = m_new

    @pl.when(step == pl.num_programs(1) - 1)
    def _():
        kn = kn_ref[0]
        vn = vn_ref[0]
        kn2 = jnp.concatenate([kn, kn], axis=0)
        vn2 = jnp.concatenate([vn, vn], axis=0)
        s_new = jnp.sum(qm * kn2, axis=-1, keepdims=True)
        m_old = m_ref[...]
        m_fin = jnp.maximum(m_old, s_new)
        a = jnp.exp2(m_old - m_fin)
        p_new = jnp.exp2(s_new - m_fin)
        l_fin = a * l_ref[...] + p_new
        o = (a * acc_ref[...] + p_new * vn2) / l_fin
        lam = _lambda_full(lq1_ref, lk1_ref, lq2_ref, lk2_ref, lam_init)
        dlt = o[0:N_HEADS] - lam * o[N_HEADS:N_MAPS]
        ms = jnp.mean(dlt * dlt, axis=-1, keepdims=True)
        o_ref[0] = dlt * lax.rsqrt(ms + SUBLN_EPS) * g_ref[...] * (1.0 - lam_init)


def _sample_attention(q, k_new, v_new, cache_k, cache_v, layer, page_table, lams, subln_g, lam_init):
    b, n_pages = page_table.shape
    pg = _pick(n_pages, (16, 8, 4, 2, 1))
    vec = lambda n: pl.BlockSpec((1, n), lambda i, s, pt: (0, 0))
    rowspec = pl.BlockSpec((1, N_HEADS, QK_DIM), lambda i, s, pt: (i, 0, 0))
    page = lambda p: pl.BlockSpec((1, 1, PAGE_SIZE, N_HEADS, QK_DIM),
                                  lambda i, s, pt: (layer, pt[i, s * pg + p], 0, 0, 0))
    grid_spec = pltpu.PrefetchScalarGridSpec(
        num_scalar_prefetch=1,
        grid=(b, n_pages // pg),
        in_specs=[vec(HEAD_DIM)] * 4 + [vec(V_DIM), rowspec, rowspec, rowspec]
                 + [page(p) for p in range(pg)] + [page(p) for p in range(pg)],
        out_specs=rowspec,
        scratch_shapes=[pltpu.VMEM((N_MAPS, QK_DIM), F32),
                        pltpu.VMEM((N_MAPS, 1), F32),
                        pltpu.VMEM((N_MAPS, 1), F32),
                        pltpu.VMEM((N_MAPS, V_DIM), F32)],
    )
    return pl.pallas_call(
        functools.partial(_sattn_kernel, pg=pg, lam_init=lam_init),
        grid_spec=grid_spec,
        out_shape=jax.ShapeDtypeStruct((b, N_HEADS, V_DIM), F32),
        compiler_params=_params(2),
        name="sample_attn",
    )(page_table, *lams, subln_g, q, k_new, v_new, *([cache_k] * pg), *([cache_v] * pg))


def _merge_kernel(h_ref, o_ref, u_ref, wga_ref, wgc_ref, wao_ref, wco_ref, out_ref, *w_copy):
    wga, wgc, wao, wco = _bf16_weights((wga_ref, wgc_ref, wao_ref, wco_ref), w_copy)
    for rows in _row_blocks(h_ref.shape[0]):
        h = h_ref[rows, :]
        att = _sigmoid(_bdot(h, wga[...])) * _bdot(o_ref[rows, :].astype(BF16), wao[...])
        conv = _sigmoid(_bdot(h, wgc[...])) * _bdot(u_ref[rows, :], wco[...])
        out_ref[rows, :] = (att + conv).astype(BF16)


def _merge(h, o, u, weights, tm):
    m, d = h.shape
    tn = 256 if weights[0][0].dtype == F32 else _pick(d, (1024, 512))
    w_in, w_out, _ = _weight_specs(weights, tn)
    row = lambda w: pl.BlockSpec((tm, w), lambda i, j: (i, 0))
    return pl.pallas_call(
        _merge_kernel,
        grid=(m // tm, d // tn),
        in_specs=[row(d), row(o.shape[1]), row(u.shape[1]), *w_in],
        out_specs=[pl.BlockSpec((tm, tn), lambda i, j: (i, j)), *w_out],
        out_shape=[jax.ShapeDtypeStruct((m, d), BF16), *_copy_shapes(weights, d)],
        compiler_params=_params(2),
        name="merge",
    )(h, o, u, *(w for w, _ in weights))


def _resid_kernel(a_ref, w_ref, x_ref, g_ref, o_ref, *w_copy):
    (wb_ref,) = _bf16_weights((w_ref,), w_copy)
    for rows in _row_blocks(a_ref.shape[0]):
        o_ref[rows, :] = x_ref[rows, :] + _mod_rows(g_ref, rows) * _bdot(a_ref[rows, :], wb_ref[...])


def _resid_proj(a, w, x, mod, gate_col, tm):
    m, d = x.shape
    k = a.shape[1]
    mb = mod.shape[0]
    tn = 512 if w.dtype == F32 else _pick(d, (1024, 512))
    per = d // tn
    weights = [(w, 0)]
    w_in, w_out, _ = _weight_specs(weights, tn)
    return pl.pallas_call(
        _resid_kernel,
        grid=(m // tm, d // tn),
        in_specs=[pl.BlockSpec((tm, k), lambda i, j: (i, 0)),
                  *w_in,
                  pl.BlockSpec((tm, tn), lambda i, j: (i, j)),
                  pl.BlockSpec((mb, tn), lambda i, j: (0, gate_col * per + j))],
        out_specs=[pl.BlockSpec((tm, tn), lambda i, j: (i, j)), *w_out],
        out_shape=[jax.ShapeDtypeStruct((m, d), F32), *_copy_shapes(weights, d)],
        compiler_params=_params(2),
        name="resid_proj",
    )(a, w, x, mod)


def _ffn_up_seq_kernel(x_ref, g_ref, sc_ref, sh_ref, wg_ref, wv_ref, cwg_ref, cwv_ref,
                       bg_ref, bv_ref, act_ref, tg_ref, tv_ref, h_ref, carry_ref):
    i, j = pl.program_id(0), pl.program_id(1)

    @pl.when(j == 0)
    def _():
        _norm_prologue(x_ref, g_ref, sc_ref, sh_ref, h_ref)

    @pl.when(i == 0)
    def _():
        carry_ref[j] = jnp.zeros(carry_ref.shape[1:], F32)

    prev_g, prev_v = carry_ref[j, 0], carry_ref[j, 1]
    for rows in _row_blocks(h_ref.shape[0]):
        h = h_ref[rows, :]
        ug, uv = _bdot(h, wg_ref[...]), _bdot(h, wv_ref[...])
        gate = _conv_rows(ug, prev_g, cwg_ref) + bg_ref[...]
        val = _conv_rows(uv, prev_v, cwv_ref) + bv_ref[...]
        act_ref[rows, :] = (gate * _sigmoid(gate) * val).astype(BF16)
        prev_g, prev_v = _last8(ug), _last8(uv)
    carry_ref[j, 0] = prev_g
    carry_ref[j, 1] = prev_v
    tg_ref[0] = prev_g
    tv_ref[0] = prev_v


def _ffn_up_state_kernel(x_ref, g_ref, sc_ref, sh_ref, wg_ref, wv_ref, cwg_ref, cwv_ref,
                         bg_ref, bv_ref, pg0_ref, pg1_ref, pv0_ref, pv1_ref,
                         act_ref, ug_ref, uv_ref, wgb_ref, wvb_ref, h_ref):
    @pl.when(pl.program_id(1) == 0)
    def _():
        _norm_prologue(x_ref, g_ref, sc_ref, sh_ref, h_ref)

    _bf16_weights((wg_ref, wv_ref), (wgb_ref, wvb_ref))
    h = h_ref[...]
    ug, uv = _bdot(h, wgb_ref[...]), _bdot(h, wvb_ref[...])
    gate = _conv_state(pg0_ref[...], pg1_ref[...], ug, cwg_ref) + bg_ref[...]
    val = _conv_state(pv0_ref[...], pv1_ref[...], uv, cwv_ref) + bv_ref[...]
    act_ref[...] = (gate * _sigmoid(gate) * val).astype(BF16)
    ug_ref[...] = ug
    uv_ref[...] = uv


def _ffn_up(x, g, mod, weights, conv_w, conv_b, tm, state=None):
    m, d = x.shape
    mb = mod.shape[0]
    f = conv_w.shape[1] // 2
    tn = _pick(f, (512, 256)) if state is not None else _pick(f, (1408, 512, 256))
    nj = f // tn
    w_in, w_out, _ = _weight_specs(weights, tn)
    row = lambda i, j: (i, 0)
    lo = lambda i, j: (0, j)
    hi = lambda i, j: (0, nj + j)
    tile = pl.BlockSpec((tm, tn), lambda i, j: (i, j))
    tile_hi = pl.BlockSpec((tm, tn), lambda i, j: (i, nj + j))
    conv_b = conv_b.reshape(1, 2 * f)
    common = [pl.BlockSpec((tm, d), row),
              pl.BlockSpec((1, d), lambda i, j: (0, 0)),
              pl.BlockSpec((mb, d), lambda i, j: (0, 4)),
              pl.BlockSpec((mb, d), lambda i, j: (0, 3)),
              *w_in,
              pl.BlockSpec((CONV_K, tn), lo), pl.BlockSpec((CONV_K, tn), hi),
              pl.BlockSpec((1, tn), lo), pl.BlockSpec((1, tn), hi)]
    args = (x, g, mod, mod, *(w for w, _ in weights), conv_w, conv_w, conv_b, conv_b)
    if state is None:
        tail = pl.BlockSpec((1, SUBLANES, tn), lambda i, j: (i, 0, j))
        return pl.pallas_call(
            _ffn_up_seq_kernel,
            grid=(m // tm, nj),
            in_specs=common,
            out_specs=[tile, tail, tail],
            out_shape=[jax.ShapeDtypeStruct((m, f), BF16),
                       jax.ShapeDtypeStruct((m // tm, SUBLANES, f), F32),
                       jax.ShapeDtypeStruct((m // tm, SUBLANES, f), F32)],
            scratch_shapes=[pltpu.VMEM((tm, d), BF16), pltpu.VMEM((nj, 2, SUBLANES, tn), F32)],
            compiler_params=_params(2),
            name="ffn_up_seq",
        )(*args)
    p0, p1 = state
    return pl.pallas_call(
        _ffn_up_state_kernel,
        grid=(m // tm, nj),
        in_specs=common + [tile, tile, tile_hi, tile_hi],
        out_specs=[tile, tile, tile, *w_out],
        out_shape=[jax.ShapeDtypeStruct((m, f), BF16),
                   jax.ShapeDtypeStruct((m, f), F32),
                   jax.ShapeDtypeStruct((m, f), F32),
                   *_copy_shapes(weights, f)],
        scratch_shapes=[pltpu.VMEM((tm, d), BF16)],
        compiler_params=_params(2),
        name="ffn_up_state",
    )(*args, p0, p1, p0, p1)


def _ffn_down_kernel(a_ref, w_ref, x_ref, g2_ref, fg_ref, y_ref, *w_copy):
    kk = pl.program_id(1)
    last = pl.num_programs(1) - 1
    (wb_ref,) = _bf16_weights((w_ref,), w_copy)
    blocks = _row_blocks(a_ref.shape[0])

    @pl.when(kk == 0)
    def _():
        for rows in blocks:
            y_ref[rows, :] = _bdot(a_ref[rows, :], wb_ref[...])

    @pl.when((kk > 0) & (kk < last))
    def _():
        for rows in blocks:
            y_ref[rows, :] += _bdot(a_ref[rows, :], wb_ref[...])

    @pl.when(kk == last)
    def _():
        for rows in blocks:
            acc = y_ref[rows, :] + _bdot(a_ref[rows, :], wb_ref[...])
            x = x_ref[rows, :] + _mod_rows(g2_ref, rows) * acc
            ms = jnp.mean(x * x, axis=-1, keepdims=True)
            y_ref[rows, :] = x * lax.rsqrt(ms + NORM_EPS) * fg_ref[...]


def _ffn_down(act, w_down, x, mod, final_g, tm):
    m, d = x.shape
    f = act.shape[1]
    mb = mod.shape[0]
    tk = _pick(f, (512, 256)) if w_down.dtype == F32 else _pick(f, (1408, 512, 256))
    assert f // tk >= 2, "the accumulate-in-output schedule needs at least two K steps"
    weights = [(w_down, 0)]
    w_in, w_out, w_shapes = _weight_specs(weights, tk, k_tiled=True)
    return pl.pallas_call(
        _ffn_down_kernel,
        grid=(m // tm, f // tk),
        in_specs=[pl.BlockSpec((tm, tk), lambda i, k: (i, k)),
                  *w_in,
                  pl.BlockSpec((tm, d), lambda i, k: (i, 0)),
                  pl.BlockSpec((mb, d), lambda i, k: (0, 5)),
                  pl.BlockSpec((1, d), lambda i, k: (0, 0))],
        out_specs=[pl.BlockSpec((tm, d), lambda i, k: (i, 0)), *w_out],
        out_shape=[jax.ShapeDtypeStruct((m, d), F32), *w_shapes],
        compiler_params=_params(2),
        name="ffn_down",
    )(act, w_down, x, mod, final_g)


def _rope_tables(pos):
    inv_freq = ROPE_THETA ** (-jnp.arange(0, ROT_DIM, 2, dtype=F32) / ROT_DIM)
    ang = pos[:, None] * inv_freq[None, :]
    cos, sin = jnp.cos(ang), jnp.sin(ang)
    d = jnp.arange(LANES) % HEAD_DIM
    idx = d % ROT_HALF
    cos_t = jnp.where(d[None, :] < ROT_DIM, cos[:, idx], 1.0)
    sa_t = jnp.where((d[None, :] >= ROT_HALF) & (d[None, :] < ROT_DIM), sin[:, idx], 0.0)
    sb_t = jnp.where(d[None, :] < ROT_HALF, -sin[:, idx], 0.0)
    return jnp.stack([cos_t, sa_t, sb_t]).astype(F32)


def kernel(x_prompt, x_sample, cache_k, cache_v, state_conv, state_ffn, page_table, c_prompt, c_sample, w_ada, b_ada, norm1_g, w_in, lambda_q1, lambda_k1, lambda_q2, lambda_k2, subln_g, w_attn_out, conv_w, w_conv_out, w_o, norm2_g, w_up, ffn_conv_w, ffn_conv_b, w_down, final_g):
    depth = w_ada.shape[0]
    bp, seq, d = x_prompt.shape
    bs, dec_seq, _ = x_sample.shape
    assert bp == 1 and dec_seq == 1, "one prompt sequence and one new token per sample sequence"
    assert depth == 1, "the final norm is fused into the ConvFFN-down kernel of the only layer"
    past_len = page_table.shape[1] * PAGE_SIZE
    qk_w = N_HEADS * QK_DIM
    cw = conv_w.shape[-1]
    f = w_down.shape[1]
    col_conv = 2 * qk_w + ATTN_W
    col_ga = col_conv + 3 * cw
    col_gc = col_ga + d
    tm_p = _pick(seq, (1024, 512, 256))
    tm_s = bs

    xp = x_prompt.reshape(seq, d)
    xs = x_sample.reshape(bs, d)
    pad = (-(bs + bp)) % BF16_ROWS
    c_rows = jnp.concatenate([c_sample, c_prompt, jnp.zeros((pad, d), F32)], axis=0)
    rope_p = _rope_tables(jnp.arange(seq, dtype=F32))
    rope_s = _rope_tables(jnp.full((bs,), past_len, dtype=F32))
    fin_g = final_g.reshape(1, d)
    keep = SUBLANES - (CONV_K - 1)

    outs = [[] for _ in range(8)]
    for li in range(depth):
        lam_init = _lambda_init(li)
        mod = _ada(c_rows, w_ada[li], b_ada[li])
        mod_s, mod_p = mod[:bs], mod[bs:bs + bp]
        lams = tuple(a[li].reshape(1, HEAD_DIM) for a in (lambda_q1, lambda_k1, lambda_q2, lambda_k2))
        g1, g2 = norm1_g[li].reshape(1, d), norm2_g[li].reshape(1, d)
        sub_g = subln_g[li].reshape(1, V_DIM)
        w_in_l, w_up_l = w_in[li], w_up[li]

        sc_prev, sf_prev = state_conv[li], state_ffn[li]
        h, q, k, v, wqkv_b = _qkv(xs, g1, mod_s, w_in_l, rope_s, tm_s)
        u, z, wcb_b, wcc_b, wcx_b = _sconv(
            h, [(w_in_l, col_conv), (w_in_l, col_conv + cw), (w_in_l, col_conv + 2 * cw)],
            conv_w[li], tm_s, state=(sc_prev[:, 0], sc_prev[:, 1]))
        per_head = lambda a: a.reshape(bs, N_HEADS, QK_DIM)
        o = _sample_attention(per_head(q), per_head(k), per_head(v), cache_k, cache_v,
                              li, page_table, lams, sub_g, lam_init).reshape(bs, ATTN_W)
        merged, wga_b, wgc_b, wao_b, wco_b = _merge(
            h, o, u, [(w_in_l, col_ga), (w_in_l, col_gc), (w_attn_out[li], 0), (w_conv_out[li], 0)], tm_s)
        x1, wo_b = _resid_proj(merged, w_o[li], xs, mod_s, 2, tm_s)
        act, ug, uv, wug_b, wuv_b = _ffn_up(x1, g2, mod_s, [(w_up_l, 0), (w_up_l, f)], ffn_conv_w[li],
                                            ffn_conv_b[li], tm_s, state=(sf_prev[:, 0], sf_prev[:, 1]))
        xs, wdn_b = _ffn_down(act, w_down[li], x1, mod_s, fin_g, tm_s)
        outs[4].append(k.reshape(bs, dec_seq, N_HEADS, QK_DIM))
        outs[5].append(v.reshape(bs, dec_seq, N_HEADS, V_DIM))
        outs[6].append(jnp.stack([sc_prev[:, 1], z], axis=1))
        outs[7].append(jnp.stack([sf_prev[:, 1], jnp.concatenate([ug, uv], axis=-1)], axis=1))

        h, q, k, v = _qkv(xp, g1, mod_p, wqkv_b, rope_p, tm_p)
        u, tail = _sconv(h, [(wcb_b, 0), (wcc_b, 0), (wcx_b, 0)], conv_w[li], tm_p)
        o = _prompt_attention(q, k, v, lams, sub_g, lam_init)
        (merged,) = _merge(h, o, u, [(wga_b, 0), (wgc_b, 0), (wao_b, 0), (wco_b, 0)], tm_p)
        (x1,) = _resid_proj(merged, wo_b, xp, mod_p, 2, tm_p)
        act, tg, tv = _ffn_up(x1, g2, mod_p, [(wug_b, 0), (wuv_b, 0)], ffn_conv_w[li],
                              ffn_conv_b[li], tm_p)
        (xp,) = _ffn_down(act, wdn_b, x1, mod_p, fin_g, tm_p)
        outs[0].append(k.reshape(bp, seq, N_HEADS, QK_DIM))
        outs[1].append(v.reshape(bp, seq, N_HEADS, V_DIM))
        outs[2].append(tail[-1, keep:].reshape(bp, CONV_K - 1, cw))
        outs[3].append(jnp.concatenate([tg[-1, keep:], tv[-1, keep:]], axis=-1).reshape(bp, CONV_K - 1, 2 * f))

    return (xp.reshape(bp, seq, d), xs.reshape(bs, dec_seq, d),
            *(jnp.stack(o) for o in outs))
```

```python
import functools
import math

import jax
import jax.numpy as jnp
from jax import lax
from jax.experimental import pallas as pl
from jax.experimental.pallas import tpu as pltpu

F32 = jnp.float32
BF16 = jnp.bfloat16

N_HEADS = 8
HEAD_DIM = 64
QK_DIM = 2 * HEAD_DIM
V_DIM = 2 * HEAD_DIM
ROT_DIM = HEAD_DIM // 4
ROT_HALF = ROT_DIM // 2
ROPE_THETA = 500000.0
ATTN_W = N_HEADS * V_DIM
CONV_K = 3
NORM_EPS = 1e-6
SUBLN_EPS = 1e-5
ATTN_SCALE = HEAD_DIM ** -0.5
Q_SCALE = ATTN_SCALE * math.log2(math.e)
PAGE_SIZE = 128
N_MAPS = 2 * N_HEADS

LANES = 128
SUBLANES = 8
BF16_ROWS = 2 * SUBLANES
MXU_DIM = 256
VMEM_LIMIT_BYTES = 56 * 1024 * 1024
QKV_VMEM_LIMIT_BYTES = 60 * 1024 * 1024
ROW_SUB = MXU_DIM
CAST_ROWS = 256
NEG_INF = float("-inf")


def _lambda_init(layer):
    return 0.8 - 0.6 * math.exp(-0.3 * layer)


def _params(n_axes, vmem_limit=VMEM_LIMIT_BYTES):
    return pltpu.CompilerParams(dimension_semantics=("arbitrary",) * n_axes,
                                vmem_limit_bytes=vmem_limit)


def _sigmoid(x):
    return 1.0 / (1.0 + jnp.exp(-x))


def _bdot(a, b):
    return jnp.dot(a, b, preferred_element_type=F32)


def _pick(n, candidates):
    for c in candidates:
        if n % c == 0:
            return c
    return n


def _row_blocks(tm):
    return [slice(r, min(r + ROW_SUB, tm)) for r in range(0, tm, ROW_SUB)]


def _cast_weight(w_ref, wb_ref):
    k = w_ref.shape[0]
    for r in range(0, k, CAST_ROWS):
        rows = slice(r, min(r + CAST_ROWS, k))
        wb_ref[rows, :] = w_ref[rows, :].astype(BF16)


def _bf16_weights(w_refs, copy_refs):
    if not copy_refs:
        return w_refs
    for w_ref, wb_ref in zip(w_refs, copy_refs, strict=True):
        _cast_weight(w_ref, wb_ref)
    return copy_refs


def _weight_specs(weights, tn, k_tiled=False):
    in_specs, out_specs, out_shapes = [], [], []
    for w, col in weights:
        k, n = w.shape
        if k_tiled:
            in_specs.append(pl.BlockSpec((tn, n), lambda i, j: (j, 0)))
        else:
            in_specs.append(pl.BlockSpec((k, tn), lambda i, j, c=col // tn: (0, c + j)))
    if weights[0][0].dtype == F32:
        for w, _ in weights:
            k, n = w.shape
            if k_tiled:
                out_specs.append(pl.BlockSpec((tn, n), lambda i, j: (j, 0)))
                out_shapes.append(jax.ShapeDtypeStruct((k, n), BF16))
            else:
                out_specs.append(pl.BlockSpec((k, tn), lambda i, j: (0, j)))
    return in_specs, out_specs, out_shapes


def _copy_shapes(weights, width):
    if weights[0][0].dtype != F32:
        return []
    return [jax.ShapeDtypeStruct((w.shape[0], width), BF16) for w, _ in weights]


def _mod_rows(ref, rows):
    return ref[...] if ref.shape[0] == 1 else ref[rows, :]


def _ada_kernel(c_ref, w_ref, b_ref, o_ref, wb_ref):
    c = c_ref[...]
    s = (c * _sigmoid(c)).astype(BF16)
    _cast_weight(w_ref, wb_ref)
    o_ref[...] = _bdot(s, wb_ref[...]) + b_ref[...]


def _ada(c_rows, w_ada, b_ada):
    m, d = c_rows.shape
    n = w_ada.shape[1]
    tn = _pick(n, (1024, 512, 256, 128))
    return pl.pallas_call(
        _ada_kernel,
        grid=(n // tn,),
        in_specs=[pl.BlockSpec((m, d), lambda j: (0, 0)),
                  pl.BlockSpec((d, tn), lambda j: (0, j)),
                  pl.BlockSpec((1, tn), lambda j: (0, j))],
        out_specs=pl.BlockSpec((m, tn), lambda j: (0, j)),
        out_shape=jax.ShapeDtypeStruct((m, n), F32),
        scratch_shapes=[pltpu.VMEM((d, tn), BF16)],
        compiler_params=_params(1),
        name="ada_mod",
    )(c_rows, w_ada, b_ada.reshape(1, n))


def _modulated_rmsnorm(x, g, sc, sh):
    ms = jnp.mean(x * x, axis=-1, keepdims=True)
    y = x * lax.rsqrt(ms + NORM_EPS)
    return y * g * (1.0 + sc) + sh


def _norm_prologue(x_ref, g_ref, sc_ref, sh_ref, h_ref):
    for rows in _row_blocks(x_ref.shape[0]):
        h = _modulated_rmsnorm(x_ref[rows, :], g_ref[...], _mod_rows(sc_ref, rows), _mod_rows(sh_ref, rows))
        h_ref[rows, :] = h.astype(BF16)


def _rope_tile(r, rope_ref, rows):
    cos, sa, sb = rope_ref[0, rows, :], rope_ref[1, rows, :], rope_ref[2, rows, :]
    outs = []
    for hh in range(r.shape[1] // LANES):
        xh = r[:, hh * LANES:(hh + 1) * LANES]
        outs.append(xh * cos + pltpu.roll(xh, ROT_HALF, 1) * sa
                    + pltpu.roll(xh, LANES - ROT_HALF, 1) * sb)
    return jnp.concatenate(outs, axis=1) if len(outs) > 1 else outs[0]


def _qkv_kernel(x_ref, g_ref, sc_ref, sh_ref, w_ref, rope_ref, h_ref, q_ref, k_ref, v_ref,
                *w_copy, nq):
    j = pl.program_id(1)

    @pl.when(j == 0)
    def _():
        _norm_prologue(x_ref, g_ref, sc_ref, sh_ref, h_ref)

    (wb_ref,) = _bf16_weights((w_ref,), w_copy)

    def run(epilogue):
        for rows in _row_blocks(h_ref.shape[0]):
            epilogue(rows, _bdot(h_ref[rows, :], wb_ref[...]))

    @pl.when(j < nq)
    def _():
        def epilogue(rows, res):
            q_ref[rows, :] = (_rope_tile(res, rope_ref, rows) * Q_SCALE).astype(BF16)
        run(epilogue)

    @pl.when((j >= nq) & (j < 2 * nq))
    def _():
        def epilogue(rows, res):
            k_ref[rows, :] = _rope_tile(res, rope_ref, rows)
        run(epilogue)

    @pl.when(j >= 2 * nq)
    def _():
        def epilogue(rows, res):
            v_ref[rows, :] = res
        run(epilogue)


def _qkv(x, g, mod, w, rope, tm):
    m, d = x.shape
    mb = mod.shape[0]
    width = N_HEADS * QK_DIM
    tn = 512 if w.dtype == F32 else width
    nq = width // tn
    weights = [(w, 0)]
    w_in, w_out, _ = _weight_specs(weights, tn)
    row = lambda i, j: (i, 0)
    out_col = lambda lo: (lambda i, j: (i, jnp.clip(j - lo, 0, nq - 1)))
    return pl.pallas_call(
        functools.partial(_qkv_kernel, nq=nq),
        grid=(m // tm, 3 * nq),
        in_specs=[pl.BlockSpec((tm, d), row),
                  pl.BlockSpec((1, d), lambda i, j: (0, 0)),
                  pl.BlockSpec((mb, d), lambda i, j: (0, 1)),
                  pl.BlockSpec((mb, d), lambda i, j: (0, 0)),
                  *w_in,
                  pl.BlockSpec((3, tm, LANES), lambda i, j: (0, i, 0))],
        out_specs=[pl.BlockSpec((tm, d), row),
                   pl.BlockSpec((tm, tn), out_col(0)),
                   pl.BlockSpec((tm, tn), out_col(nq)),
                   pl.BlockSpec((tm, tn), out_col(2 * nq)),
                   *w_out],
        out_shape=[jax.ShapeDtypeStruct((m, d), BF16),
                   jax.ShapeDtypeStruct((m, width), BF16),
                   jax.ShapeDtypeStruct((m, width), F32),
                   jax.ShapeDtypeStruct((m, width), F32),
                   *_copy_shapes(weights, 3 * width)],
        compiler_params=_params(2, QKV_VMEM_LIMIT_BYTES),
        name="qkv_proj",
    )(x, g, mod, mod, w, rope)


def _conv_rows(z, prev8, w_ref):
    n = z.shape[0]
    zz = jnp.concatenate([prev8, z], axis=0)
    z1 = zz[SUBLANES - 1:SUBLANES - 1 + n]
    z2 = zz[SUBLANES - 2:SUBLANES - 2 + n]
    return z2 * w_ref[0:1, :] + z1 * w_ref[1:2, :] + z * w_ref[2:3, :]


def _conv_state(p0, p1, z, w_ref):
    return p0 * w_ref[0:1, :] + p1 * w_ref[1:2, :] + z * w_ref[2:3, :]


def _last8(z):
    return z[z.shape[0] - SUBLANES:]


def _sconv_seq_kernel(h_ref, wb_ref, wc_ref, wx_ref, cw_ref, u_ref, tail_ref, carry_ref):
    i, j = pl.program_id(0), pl.program_id(1)

    @pl.when(i == 0)
    def _():
        carry_ref[j] = jnp.zeros(carry_ref.shape[1:], F32)

    prev = carry_ref[j]
    for rows in _row_blocks(h_ref.shape[0]):
        h = h_ref[rows, :]
        z = _bdot(h, wc_ref[...]) * _bdot(h, wx_ref[...])
        u_ref[rows, :] = (_bdot(h, wb_ref[...]) * _conv_rows(z, prev, cw_ref)).astype(BF16)
        prev = _last8(z)
    carry_ref[j] = prev
    tail_ref[0] = prev


def _sconv_state_kernel(h_ref, wb_ref, wc_ref, wx_ref, cw_ref, p0_ref, p1_ref, u_ref, z_ref, *w_copy):
    wbb_ref, wcb_ref, wxb_ref = _bf16_weights((wb_ref, wc_ref, wx_ref), w_copy)
    h = h_ref[...]
    z = _bdot(h, wcb_ref[...]) * _bdot(h, wxb_ref[...])
    conv = _conv_state(p0_ref[...], p1_ref[...], z, cw_ref)
    u_ref[...] = (_bdot(h, wbb_ref[...]) * conv).astype(BF16)
    z_ref[...] = z


def _sconv(h, weights, conv_w, tm, state=None):
    m, d = h.shape
    cw = conv_w.shape[1]
    tn = 512 if state is not None else _pick(cw, (1024, 512))
    nj = cw // tn
    w_in, w_out, _ = _weight_specs(weights, tn)
    tile = pl.BlockSpec((tm, tn), lambda i, j: (i, j))
    common = [pl.BlockSpec((tm, d), lambda i, j: (i, 0)), *w_in,
              pl.BlockSpec((CONV_K, tn), lambda i, j: (0, j))]
    operands = (h, *(w for w, _ in weights), conv_w)
    if state is None:
        return pl.pallas_call(
            _sconv_seq_kernel,
            grid=(m // tm, nj),
            in_specs=common,
            out_specs=[tile, pl.BlockSpec((1, SUBLANES, tn), lambda i, j: (i, 0, j))],
            out_shape=[jax.ShapeDtypeStruct((m, cw), BF16),
                       jax.ShapeDtypeStruct((m // tm, SUBLANES, cw), F32)],
            scratch_shapes=[pltpu.VMEM((nj, SUBLANES, tn), F32)],
            compiler_params=_params(2),
            name="sconv_seq",
        )(*operands)
    return pl.pallas_call(
        _sconv_state_kernel,
        grid=(m // tm, nj),
        in_specs=common + [tile, tile],
        out_specs=[tile, tile, *w_out],
        out_shape=[jax.ShapeDtypeStruct((m, cw), BF16), jax.ShapeDtypeStruct((m, cw), F32),
                   *_copy_shapes(weights, cw)],
        compiler_params=_params(2),
        name="sconv_state",
    )(*operands, *state)


def _lambda_full(lq1_ref, lk1_ref, lq2_ref, lk2_ref, lam_init):
    a = jnp.sum(lq1_ref[...] * lk1_ref[...], axis=-1, keepdims=True)
    b = jnp.sum(lq2_ref[...] * lk2_ref[...], axis=-1, keepdims=True)
    return jnp.exp(a) - jnp.exp(b) + lam_init


def _attn_kernel(lq1_ref, lk1_ref, lq2_ref, lk2_ref, g_ref, q_ref, k_ref, v_ref, o_ref,
                 qst_ref, kb_ref, vt_ref, m_ref, acc_ref, s0_ref, s1_ref, *, bq, lam_init):
    qi = pl.program_id(1)
    t = k_ref.shape[0]
    bk = s0_ref.shape[0]

    @pl.when(qi == 0)
    def _():
        for c in range(t // bq):
            rows = slice(c * bq, (c + 1) * bq)
            kb_ref[rows, :] = k_ref[rows, :].astype(BF16)
            vt_ref[0:V_DIM, rows] = v_ref[rows, :].T.astype(BF16)
        vt_ref[V_DIM:, :] = jnp.ones((vt_ref.shape[0] - V_DIM, t), BF16)

    qt = q_ref[...].astype(F32).T
    dim = lax.broadcasted_iota(jnp.int32, qt.shape, 0)
    qst_ref[:, 0:bq] = jnp.where(dim < HEAD_DIM, qt, 0.0).astype(BF16)
    qst_ref[:, bq:2 * bq] = jnp.where(dim >= HEAD_DIM, qt, 0.0).astype(BF16)
    m_ref[...] = jnp.full(m_ref.shape, NEG_INF, F32)
    acc_ref[...] = jnp.zeros(acc_ref.shape, F32)

    def scores(j, s_ref):
        off = pl.multiple_of(j * bk, bk)
        s_ref[...] = _bdot(kb_ref[pl.ds(off, bk), :], qst_ref[...])

    def softmax_pv(j, s_ref, diag_key0=None):
        off = pl.multiple_of(j * bk, bk)
        st = s_ref[...]
        if diag_key0 is not None:
            key = lax.broadcasted_iota(jnp.int32, st.shape, 0) + diag_key0
            qry = lax.broadcasted_iota(jnp.int32, st.shape, 1)
            qry = jnp.where(qry >= bq, qry - bq, qry)
            st = jnp.where(key <= qry, st, NEG_INF)
        m_prev = m_ref[...]
        m_new = jnp.maximum(m_prev, jnp.max(st, axis=0, keepdims=True))
        alpha = jnp.exp2(m_prev - m_new)
        pt = jnp.exp2(st - m_new)
        acc_ref[...] = alpha * acc_ref[...] + _bdot(vt_ref[:, pl.ds(off, bk)], pt.astype(BF16))
        m_ref[...] = m_new

    def body(jj, carry):
        j = 2 * jj
        scores(j + 1, s1_ref)
        softmax_pv(j, s0_ref)
        scores(j + 2, s0_ref)
        softmax_pv(j + 1, s1_ref)
        return carry

    scores(0, s0_ref)
    lax.fori_loop(0, qi, body, 0)
    scores(2 * qi + 1, s1_ref)
    softmax_pv(2 * qi, s0_ref, diag_key0=0)
    softmax_pv(2 * qi + 1, s1_ref, diag_key0=bk)

    lam = _lambda_full(lq1_ref, lk1_ref, lq2_ref, lk2_ref, lam_init)
    ot = acc_ref[0:V_DIM, :] / acc_ref[V_DIM:V_DIM + 1, :]
    dlt = ot[:, 0:bq] - lam * ot[:, bq:2 * bq]
    ms = jnp.mean(dlt * dlt, axis=0, keepdims=True)
    y = (dlt * lax.rsqrt(ms + SUBLN_EPS)).T * g_ref[...] * (1.0 - lam_init)
    o_ref[...] = y.astype(BF16)


def _prompt_attention(q, k, v, lams, subln_g, lam_init):
    t = q.shape[0]
    bq = _pick(t, (512, 256))
    vec = lambda n: pl.BlockSpec((1, n), lambda h, i: (0, 0))
    return pl.pallas_call(
        functools.partial(_attn_kernel, bq=bq, lam_init=lam_init),
        grid=(N_HEADS, t // bq),
        in_specs=[vec(HEAD_DIM)] * 4 + [vec(V_DIM),
                  pl.BlockSpec((bq, QK_DIM), lambda h, i: (i, h)),
                  pl.BlockSpec((t, QK_DIM), lambda h, i: (0, h)),
                  pl.BlockSpec((t, V_DIM), lambda h, i: (0, h))],
        out_specs=pl.BlockSpec((bq, V_DIM), lambda h, i: (i, h)),
        out_shape=jax.ShapeDtypeStruct((t, ATTN_W), BF16),
        scratch_shapes=[pltpu.VMEM((QK_DIM, 2 * bq), BF16),
                        pltpu.VMEM((t, QK_DIM), BF16),
                        pltpu.VMEM((V_DIM + BF16_ROWS, t), BF16),
                        pltpu.VMEM((1, 2 * bq), F32),
                        pltpu.VMEM((V_DIM + BF16_ROWS, 2 * bq), F32),
                        pltpu.VMEM((bq // 2, 2 * bq), F32),
                        pltpu.VMEM((bq // 2, 2 * bq), F32)],
        compiler_params=_params(2),
        name="prompt_attn",
    )(*lams, subln_g, q, k, v)


def _sattn_kernel(pt_ref, lq1_ref, lk1_ref, lq2_ref, lk2_ref, g_ref, q_ref, kn_ref, vn_ref, *rest,
                  pg, lam_init):
    k_refs, v_refs = rest[:pg], rest[pg:2 * pg]
    o_ref, qm_ref, m_ref, l_ref, acc_ref = rest[2 * pg:]
    step = pl.program_id(1)
    page_rows = PAGE_SIZE * N_HEADS
    lane = lax.broadcasted_iota(jnp.int32, (N_HEADS, QK_DIM), 1)

    @pl.when(step == 0)
    def _():
        q = q_ref[0].astype(F32)
        qm_ref[0:N_HEADS] = jnp.where(lane < HEAD_DIM, q, 0.0)
        qm_ref[N_HEADS:N_MAPS] = jnp.where(lane >= HEAD_DIM, q, 0.0)
        m_ref[...] = jnp.full(m_ref.shape, NEG_INF, F32)
        l_ref[...] = jnp.zeros(l_ref.shape, F32)
        acc_ref[...] = jnp.zeros(acc_ref.shape, F32)

    qm = qm_ref[...]
    qmb = qm.astype(BF16)
    s = jnp.concatenate(
        [lax.dot_general(qmb, k_refs[p][0, 0].reshape(page_rows, QK_DIM).astype(BF16),
                         (((1,), (1,)), ((), ())), preferred_element_type=F32)
         for p in range(pg)], axis=1)
    row = lax.broadcasted_iota(jnp.int32, s.shape, 0)
    col = lax.broadcasted_iota(jnp.int32, s.shape, 1)
    s = jnp.where(col % N_HEADS == row % N_HEADS, s, NEG_INF)
    m_prev = m_ref[...]
    m_new = jnp.maximum(m_prev, jnp.max(s, axis=-1, keepdims=True))
    alpha = jnp.exp2(m_prev - m_new)
    p_all = jnp.exp2(s - m_new)
    l_ref[...] = alpha * l_ref[...] + jnp.sum(p_all, axis=-1, keepdims=True)
    pb = p_all.astype(BF16)
    pv = None
    for p in range(pg):
        part = _bdot(pb[:, p * page_rows:(p + 1) * page_rows],
                     v_refs[p][0, 0].reshape(page_rows, V_DIM).astype(BF16))
        pv = part if pv is None else pv + part
    acc_ref[...] = alpha * acc_ref[...] + pv
    m_ref[...] = m_new

    @pl.when(step == pl.num_programs(1) - 1)
    def _():
        kn = kn_ref[0]
        vn = vn_ref[0]
        kn2 = jnp.concatenate([kn, kn], axis=0)
        vn2 = jnp.concatenate([vn, vn], axis=0)
        s_new = jnp.sum(qm * kn2, axis=-1, keepdims=True)
        m_old = m_ref[...]
        m_fin = jnp.maximum(m_old, s_new)
        a = jnp.exp2(m_old - m_fin)
        p_new = jnp.exp2(s_new - m_fin)
        l_fin = a * l_ref[...] + p_new
        o = (a * acc_ref[...] + p_new * vn2) / l_fin
        lam = _lambda_full(lq1_ref, lk1_ref, lq2_ref, lk2_ref, lam_init)
        dlt = o[0:N_HEADS] - lam * o[N_HEADS:N_MAPS]
        ms = jnp.mean(dlt * dlt, axis=-1, keepdims=True)
        o_ref[0] = dlt * lax.rsqrt(ms + SUBLN_EPS) * g_ref[...] * (1.0 - lam_init)


def _sample_attention(q, k_new, v_new, cache_k, cache_v, layer, page_table, lams, subln_g, lam_init):
    b, n_pages = page_table.shape
    pg = _pick(n_pages, (16, 8, 4, 2, 1))
    vec = lambda n: pl.BlockSpec((1, n), lambda i, s, pt: (0, 0))
    rowspec = pl.BlockSpec((1, N_HEADS, QK_DIM), lambda i, s, pt: (i, 0, 0))
    page = lambda p: pl.BlockSpec((1, 1, PAGE_SIZE, N_HEADS, QK_DIM),
                                  lambda i, s, pt: (layer, pt[i, s * pg + p], 0, 0, 0))
    grid_spec = pltpu.PrefetchScalarGridSpec(
        num_scalar_prefetch=1,
        grid=(b, n_pages // pg),
        in_specs=[vec(HEAD_DIM)] * 4 + [vec(V_DIM), rowspec, rowspec, rowspec]
                 + [page(p) for p in range(pg)] + [page(p) for p in range(pg)],
        out_specs=rowspec,
        scratch_shapes=[pltpu.VMEM((N_MAPS, QK_DIM), F32),
                        pltpu.VMEM((N_MAPS, 1), F32),
                        pltpu.VMEM((N_MAPS, 1), F32),
                        pltpu.VMEM((N_MAPS, V_DIM), F32)],
    )
    return pl.pallas_call(
        functools.partial(_sattn_kernel, pg=pg, lam_init=lam_init),
        grid_spec=grid_spec,
        out_shape=jax.ShapeDtypeStruct((b, N_HEADS, V_DIM), F32),
        compiler_params=_params(2),
        name="sample_attn",
    )(page_table, *lams, subln_g, q, k_new, v_new, *([cache_k] * pg), *([cache_v] * pg))


def _merge_kernel(h_ref, o_ref, u_ref, wga_ref, wgc_ref, wao_ref, wco_ref, out_ref, *w_copy):
    wga, wgc, wao, wco = _bf16_weights((wga_ref, wgc_ref, wao_ref, wco_ref), w_copy)
    for rows in _row_blocks(h_ref.shape[0]):
        h = h_ref[rows, :]
        att = _sigmoid(_bdot(h, wga[...])) * _bdot(o_ref[rows, :].astype(BF16), wao[...])
        conv = _sigmoid(_bdot(h, wgc[...])) * _bdot(u_ref[rows, :], wco[...])
        out_ref[rows, :] = (att + conv).astype(BF16)


def _merge(h, o, u, weights, tm):
    m, d = h.shape
    tn = 256 if weights[0][0].dtype == F32 else _pick(d, (1024, 512))
    w_in, w_out, _ = _weight_specs(weights, tn)
    row = lambda w: pl.BlockSpec((tm, w), lambda i, j: (i, 0))
    return pl.pallas_call(
        _merge_kernel,
        grid=(m // tm, d // tn),
        in_specs=[row(d), row(o.shape[1]), row(u.shape[1]), *w_in],
        out_specs=[pl.BlockSpec((tm, tn), lambda i, j: (i, j)), *w_out],
        out_shape=[jax.ShapeDtypeStruct((m, d), BF16), *_copy_shapes(weights, d)],
        compiler_params=_params(2),
        name="merge",
    )(h, o, u, *(w for w, _ in weights))


def _resid_norm_kernel(a_ref, w_ref, x_ref, gate_ref, g_ref, sc_ref, sh_ref, x1_ref, h_ref, *w_copy):
    (wb_ref,) = _bf16_weights((w_ref,), w_copy)
    for rows in _row_blocks(a_ref.shape[0]):
        x1 = x_ref[rows, :] + _mod_rows(gate_ref, rows) * _bdot(a_ref[rows, :], wb_ref[...])
        x1_ref[rows, :] = x1
        h = _modulated_rmsnorm(x1, g_ref[...], _mod_rows(sc_ref, rows), _mod_rows(sh_ref, rows))
        h_ref[rows, :] = h.astype(BF16)


def _resid_norm(a, w, x, mod, g, tm):
    m, d = x.shape
    k = a.shape[1]
    mb = mod.shape[0]
    weights = [(w, 0)]
    w_in, w_out, _ = _weight_specs(weights, d)
    row = lambda width: pl.BlockSpec((tm, width), lambda i, j: (i, 0))
    modcol = lambda c: pl.BlockSpec((mb, d), lambda i, j: (0, c))
    return pl.pallas_call(
        _resid_norm_kernel,
        grid=(m // tm, 1),
        in_specs=[row(k), *w_in, row(d), modcol(2),
                  pl.BlockSpec((1, d), lambda i, j: (0, 0)), modcol(4), modcol(3)],
        out_specs=[row(d), row(d), *w_out],
        out_shape=[jax.ShapeDtypeStruct((m, d), F32), jax.ShapeDtypeStruct((m, d), BF16),
                   *_copy_shapes(weights, d)],
        compiler_params=_params(2),
        name="resid_norm",
    )(a, w, x, mod, g, mod, mod)


def _ffn_up_seq_kernel(h_ref, wg_ref, wv_ref, cwg_ref, cwv_ref, bg_ref, bv_ref,
                       act_ref, tg_ref, tv_ref, carry_ref):
    i, j = pl.program_id(0), pl.program_id(1)

    @pl.when(i == 0)
    def _():
        carry_ref[j] = jnp.zeros(carry_ref.shape[1:], F32)

    prev_g, prev_v = carry_ref[j, 0], carry_ref[j, 1]
    for rows in _row_blocks(h_ref.shape[0]):
        h = h_ref[rows, :]
        ug, uv = _bdot(h, wg_ref[...]), _bdot(h, wv_ref[...])
        gate = _conv_rows(ug, prev_g, cwg_ref) + bg_ref[...]
        val = _conv_rows(uv, prev_v, cwv_ref) + bv_ref[...]
        act_ref[rows, :] = (gate * _sigmoid(gate) * val).astype(BF16)
        prev_g, prev_v = _last8(ug), _last8(uv)
    carry_ref[j, 0] = prev_g
    carry_ref[j, 1] = prev_v
    tg_ref[0] = prev_g
    tv_ref[0] = prev_v


def _ffn_up_state_kernel(h_ref, wg_ref, wv_ref, cwg_ref, cwv_ref, bg_ref, bv_ref,
                         pg0_ref, pg1_ref, pv0_ref, pv1_ref,
                         act_ref, ug_ref, uv_ref, wgb_ref, wvb_ref):
    _bf16_weights((wg_ref, wv_ref), (wgb_ref, wvb_ref))
    h = h_ref[...]
    ug, uv = _bdot(h, wgb_ref[...]), _bdot(h, wvb_ref[...])
    gate = _conv_state(pg0_ref[...], pg1_ref[...], ug, cwg_ref) + bg_ref[...]
    val = _conv_state(pv0_ref[...], pv1_ref[...], uv, cwv_ref) + bv_ref[...]
    act_ref[...] = (gate * _sigmoid(gate) * val).astype(BF16)
    ug_ref[...] = ug
    uv_ref[...] = uv


def _ffn_up(h, weights, conv_w, conv_b, tm, state=None):
    m, d = h.shape
    f = conv_w.shape[1] // 2
    tn = _pick(f, (512, 256)) if state is not None else _pick(f, (1408, 512, 256))
    nj = f // tn
    w_in, w_out, _ = _weight_specs(weights, tn)
    lo = lambda i, j: (0, j)
    hi = lambda i, j: (0, nj + j)
    tile = pl.BlockSpec((tm, tn), lambda i, j: (i, j))
    tile_hi = pl.BlockSpec((tm, tn), lambda i, j: (i, nj + j))
    conv_b = conv_b.reshape(1, 2 * f)
    common = [pl.BlockSpec((tm, d), lambda i, j: (i, 0)),
              *w_in,
              pl.BlockSpec((CONV_K, tn), lo), pl.BlockSpec((CONV_K, tn), hi),
              pl.BlockSpec((1, tn), lo), pl.BlockSpec((1, tn), hi)]
    args = (h, *(w for w, _ in weights), conv_w, conv_w, conv_b, conv_b)
    if state is None:
        tail = pl.BlockSpec((1, SUBLANES, tn), lambda i, j: (i, 0, j))
        return pl.pallas_call(
            _ffn_up_seq_kernel,
            grid=(m // tm, nj),
            in_specs=common,
            out_specs=[tile, tail, tail],
            out_shape=[jax.ShapeDtypeStruct((m, f), BF16),
                       jax.ShapeDtypeStruct((m // tm, SUBLANES, f), F32),
                       jax.ShapeDtypeStruct((m // tm, SUBLANES, f), F32)],
            scratch_shapes=[pltpu.VMEM((nj, 2, SUBLANES, tn), F32)],
            compiler_params=_params(2),
            name="ffn_up_seq",
        )(*args)
    p0, p1 = state
    return pl.pallas_call(
        _ffn_up_state_kernel,
        grid=(m // tm, nj),
        in_specs=common + [tile, tile, tile_hi, tile_hi],
        out_specs=[tile, tile, tile, *w_out],
        out_shape=[jax.ShapeDtypeStruct((m, f), BF16),
                   jax.ShapeDtypeStruct((m, f), F32),
                   jax.ShapeDtypeStruct((m, f), F32),
                   *_copy_shapes(weights, f)],
        compiler_params=_params(2),
        name="ffn_up_state",
    )(*args, p0, p1, p0, p1)


def _ffn_down_kernel(a_ref, w_ref, x_ref, g2_ref, fg_ref, y_ref, *w_copy):
    kk = pl.program_id(1)
    last = pl.num_programs(1) - 1
    (wb_ref,) = _bf16_weights((w_ref,), w_copy)
    blocks = _row_blocks(a_ref.shape[0])

    @pl.when(kk == 0)
    def _():
        for rows in blocks:
            y_ref[rows, :] = _bdot(a_ref[rows, :], wb_ref[...])

    @pl.when((kk > 0) & (kk < last))
    def _():
        for rows in blocks:
            y_ref[rows, :] += _bdot(a_ref[rows, :], wb_ref[...])

    @pl.when(kk == last)
    def _():
        for rows in blocks:
            acc = y_ref[rows, :] + _bdot(a_ref[rows, :], wb_ref[...])
            x = x_ref[rows, :] + _mod_rows(g2_ref, rows) * acc
            ms = jnp.mean(x * x, axis=-1, keepdims=True)
            y_ref[rows, :] = x * lax.rsqrt(ms + NORM_EPS) * fg_ref[...]


def _ffn_down(act, w_down, x, mod, final_g, tm):
    m, d = x.shape
    f = act.shape[1]
    mb = mod.shape[0]
    tk = _pick(f, (512, 256)) if w_down.dtype == F32 else _pick(f, (1408, 512, 256))
    assert f // tk >= 2, "the accumulate-in-output schedule needs at least two K steps"
    weights = [(w_down, 0)]
    w_in, w_out, w_shapes = _weight_specs(weights, tk, k_tiled=True)
    return pl.pallas_call(
        _ffn_down_kernel,
        grid=(m // tm, f // tk),
        in_specs=[pl.BlockSpec((tm, tk), lambda i, k: (i, k)),
                  *w_in,
                  pl.BlockSpec((tm, d), lambda i, k: (i, 0)),
                  pl.BlockSpec((mb, d), lambda i, k: (0, 5)),
                  pl.BlockSpec((1, d), lambda i, k: (0, 0))],
        out_specs=[pl.BlockSpec((tm, d), lambda i, k: (i, 0)), *w_out],
        out_shape=[jax.ShapeDtypeStruct((m, d), F32), *w_shapes],
        compiler_params=_params(2),
        name="ffn_down",
    )(act, w_down, x, mod, final_g)


def _rope_tables(pos):
    inv_freq = ROPE_THETA ** (-jnp.arange(0, ROT_DIM, 2, dtype=F32) / ROT_DIM)
    ang = pos[:, None] * inv_freq[None, :]
    cos, sin = jnp.cos(ang), jnp.sin(ang)
    d = jnp.arange(LANES) % HEAD_DIM
    idx = d % ROT_HALF
    cos_t = jnp.where(d[None, :] < ROT_DIM, cos[:, idx], 1.0)
    sa_t = jnp.where((d[None, :] >= ROT_HALF) & (d[None, :] < ROT_DIM), sin[:, idx], 0.0)
    sb_t = jnp.where(d[None, :] < ROT_HALF, -sin[:, idx], 0.0)
    return jnp.stack([cos_t, sa_t, sb_t]).astype(F32)


def kernel(x_prompt, x_sample, cache_k, cache_v, state_conv, state_ffn, page_table, c_prompt, c_sample, w_ada, b_ada, norm1_g, w_in, lambda_q1, lambda_k1, lambda_q2, lambda_k2, subln_g, w_attn_out, conv_w, w_conv_out, w_o, norm2_g, w_up, ffn_conv_w, ffn_conv_b, w_down, final_g):
    depth = w_ada.shape[0]
    bp, seq, d = x_prompt.shape
    bs, dec_seq, _ = x_sample.shape
    assert bp == 1 and dec_seq == 1, "one prompt sequence and one new token per sample sequence"
    assert depth == 1, "the final norm is fused into the ConvFFN-down kernel of the only layer"
    past_len = page_table.shape[1] * PAGE_SIZE
    qk_w = N_HEADS * QK_DIM
    cw = conv_w.shape[-1]
    f = w_down.shape[1]
    col_conv = 2 * qk_w + ATTN_W
    col_ga = col_conv + 3 * cw
    col_gc = col_ga + d
    tm_p = _pick(seq, (1024, 512, 256))
    tm_s = bs

    xp = x_prompt.reshape(seq, d)
    xs = x_sample.reshape(bs, d)
    pad = (-(bs + bp)) % BF16_ROWS
    c_rows = jnp.concatenate([c_sample, c_prompt, jnp.zeros((pad, d), F32)], axis=0)
    rope_p = _rope_tables(jnp.arange(seq, dtype=F32))
    rope_s = _rope_tables(jnp.full((bs,), past_len, dtype=F32))
    fin_g = final_g.reshape(1, d)
    keep = SUBLANES - (CONV_K - 1)

    outs = [[] for _ in range(8)]
    for li in range(depth):
        lam_init = _lambda_init(li)
        mod = _ada(c_rows, w_ada[li], b_ada[li])
        mod_s, mod_p = mod[:bs], mod[bs:bs + bp]
        lams = tuple(a[li].reshape(1, HEAD_DIM) for a in (lambda_q1, lambda_k1, lambda_q2, lambda_k2))
        g1, g2 = norm1_g[li].reshape(1, d), norm2_g[li].reshape(1, d)
        sub_g = subln_g[li].reshape(1, V_DIM)
        w_in_l, w_up_l = w_in[li], w_up[li]

        sc_prev, sf_prev = state_conv[li], state_ffn[li]
        h, q, k, v, wqkv_b = _qkv(xs, g1, mod_s, w_in_l, rope_s, tm_s)
        u, z, wcb_b, wcc_b, wcx_b = _sconv(
            h, [(w_in_l, col_conv), (w_in_l, col_conv + cw), (w_in_l, col_conv + 2 * cw)],
            conv_w[li], tm_s, state=(sc_prev[:, 0], sc_prev[:, 1]))
        per_head = lambda a: a.reshape(bs, N_HEADS, QK_DIM)
        o = _sample_attention(per_head(q), per_head(k), per_head(v), cache_k, cache_v,
                              li, page_table, lams, sub_g, lam_init).reshape(bs, ATTN_W)
        merged, wga_b, wgc_b, wao_b, wco_b = _merge(
            h, o, u, [(w_in_l, col_ga), (w_in_l, col_gc), (w_attn_out[li], 0), (w_conv_out[li], 0)], tm_s)
        x1, h2, wo_b = _resid_norm(merged, w_o[li], xs, mod_s, g2, tm_s)
        act, ug, uv, wug_b, wuv_b = _ffn_up(h2, [(w_up_l, 0), (w_up_l, f)], ffn_conv_w[li],
                                            ffn_conv_b[li], tm_s, state=(sf_prev[:, 0], sf_prev[:, 1]))
        xs, wdn_b = _ffn_down(act, w_down[li], x1, mod_s, fin_g, tm_s)
        outs[4].append(k.reshape(bs, dec_seq, N_HEADS, QK_DIM))
        outs[5].append(v.reshape(bs, dec_seq, N_HEADS, V_DIM))
        outs[6].append(jnp.stack([sc_prev[:, 1], z], axis=1))
        outs[7].append(jnp.stack([sf_prev[:, 1], jnp.concatenate([ug, uv], axis=-1)], axis=1))

        h, q, k, v = _qkv(xp, g1, mod_p, wqkv_b, rope_p, tm_p)
        u, tail = _sconv(h, [(wcb_b, 0), (wcc_b, 0), (wcx_b, 0)], conv_w[li], tm_p)
        o = _prompt_attention(q, k, v, lams, sub_g, lam_init)
        (merged,) = _merge(h, o, u, [(wga_b, 0), (wgc_b, 0), (wao_b, 0), (wco_b, 0)], tm_p)
        x1, h2 = _resid_norm(merged, wo_b, xp, mod_p, g2, min(tm_p, 512))
        act, tg, tv = _ffn_up(h2, [(wug_b, 0), (wuv_b, 0)], ffn_conv_w[li], ffn_conv_b[li], tm_p)
        (xp,) = _ffn_down(act, wdn_b, x1, mod_p, fin_g, tm_p)
        outs[0].append(k.reshape(bp, seq, N_HEADS, QK_DIM))
        outs[1].append(v.reshape(bp, seq, N_HEADS, V_DIM))
        outs[2].append(tail[-1, keep:].reshape(bp, CONV_K - 1, cw))
        outs[3].append(jnp.concatenate([tg[-1, keep:], tv[-1, keep:]], axis=-1).reshape(bp, CONV_K - 1, 2 * f))

    return (xp.reshape(bp, seq, d), xs.reshape(bs, dec_seq, d),
            *(jnp.stack(o) for o in outs))
```

```python
import functools
import math

import jax
import jax.numpy as jnp
from jax import lax
from jax.experimental import pallas as pl
from jax.experimental.pallas import tpu as pltpu

F32 = jnp.float32
BF16 = jnp.bfloat16

N_HEADS = 8
HEAD_DIM = 64
QK_DIM = 2 * HEAD_DIM
V_DIM = 2 * HEAD_DIM
ROT_DIM = HEAD_DIM // 4
ROT_HALF = ROT_DIM // 2
ROPE_THETA = 500000.0
ATTN_W = N_HEADS * V_DIM
CONV_K = 3
NORM_EPS = 1e-6
SUBLN_EPS = 1e-5
ATTN_SCALE = HEAD_DIM ** -0.5
Q_SCALE = ATTN_SCALE * math.log2(math.e)
PAGE_SIZE = 128
N_MAPS = 2 * N_HEADS

LANES = 128
SUBLANES = 8
BF16_ROWS = 2 * SUBLANES
MXU_DIM = 256
VMEM_LIMIT_BYTES = 56 * 1024 * 1024
QKV_VMEM_LIMIT_BYTES = 60 * 1024 * 1024
ROW_SUB = MXU_DIM
CAST_ROWS = 256
NEG_INF = float("-inf")


def _lambda_init(layer):
    return 0.8 - 0.6 * math.exp(-0.3 * layer)


def _params(n_axes, vmem_limit=VMEM_LIMIT_BYTES):
    return pltpu.CompilerParams(dimension_semantics=("arbitrary",) * n_axes,
                                vmem_limit_bytes=vmem_limit)


def _sigmoid(x):
    return 1.0 / (1.0 + jnp.exp(-x))


def _bdot(a, b):
    return jnp.dot(a, b, preferred_element_type=F32)


def _pick(n, candidates):
    for c in candidates:
        if n % c == 0:
            return c
    return n


def _row_blocks(tm):
    return [slice(r, min(r + ROW_SUB, tm)) for r in range(0, tm, ROW_SUB)]


def _cast_weight(w_ref, wb_ref):
    k = w_ref.shape[0]
    for r in range(0, k, CAST_ROWS):
        rows = slice(r, min(r + CAST_ROWS, k))
        wb_ref[rows, :] = w_ref[rows, :].astype(BF16)


def _bf16_weights(w_refs, copy_refs):
    if not copy_refs:
        return w_refs
    for w_ref, wb_ref in zip(w_refs, copy_refs, strict=True):
        _cast_weight(w_ref, wb_ref)
    return copy_refs


def _weight_specs(weights, tn, k_tiled=False):
    in_specs, out_specs, out_shapes = [], [], []
    for w, col in weights:
        k, n = w.shape
        if k_tiled:
            in_specs.append(pl.BlockSpec((tn, n), lambda i, j: (j, 0)))
        else:
            in_specs.append(pl.BlockSpec((k, tn), lambda i, j, c=col // tn: (0, c + j)))
    if weights[0][0].dtype == F32:
        for w, _ in weights:
            k, n = w.shape
            if k_tiled:
                out_specs.append(pl.BlockSpec((tn, n), lambda i, j: (j, 0)))
                out_shapes.append(jax.ShapeDtypeStruct((k, n), BF16))
            else:
                out_specs.append(pl.BlockSpec((k, tn), lambda i, j: (0, j)))
    return in_specs, out_specs, out_shapes


def _copy_shapes(weights, width):
    if weights[0][0].dtype != F32:
        return []
    return [jax.ShapeDtypeStruct((w.shape[0], width), BF16) for w, _ in weights]


def _mod_rows(ref, rows):
    return ref[...] if ref.shape[0] == 1 else ref[rows, :]


def _ada_kernel(c_ref, w_ref, b_ref, o_ref, wb_ref):
    c = c_ref[...]
    s = (c * _sigmoid(c)).astype(BF16)
    _cast_weight(w_ref, wb_ref)
    o_ref[...] = _bdot(s, wb_ref[...]) + b_ref[...]


def _ada(c_rows, w_ada, b_ada):
    m, d = c_rows.shape
    n = w_ada.shape[1]
    tn = _pick(n, (1024, 512, 256, 128))
    return pl.pallas_call(
        _ada_kernel,
        grid=(n // tn,),
        in_specs=[pl.BlockSpec((m, d), lambda j: (0, 0)),
                  pl.BlockSpec((d, tn), lambda j: (0, j)),
                  pl.BlockSpec((1, tn), lambda j: (0, j))],
        out_specs=pl.BlockSpec((m, tn), lambda j: (0, j)),
        out_shape=jax.ShapeDtypeStruct((m, n), F32),
        scratch_shapes=[pltpu.VMEM((d, tn), BF16)],
        compiler_params=_params(1),
        name="ada_mod",
    )(c_rows, w_ada, b_ada.reshape(1, n))


def _modulated_rmsnorm(x, g, sc, sh):
    ms = jnp.mean(x * x, axis=-1, keepdims=True)
    y = x * lax.rsqrt(ms + NORM_EPS)
    return y * g * (1.0 + sc) + sh


def _norm_prologue(x_ref, g_ref, sc_ref, sh_ref, h_ref):
    for rows in _row_blocks(x_ref.shape[0]):
        h = _modulated_rmsnorm(x_ref[rows, :], g_ref[...], _mod_rows(sc_ref, rows), _mod_rows(sh_ref, rows))
        h_ref[rows, :] = h.astype(BF16)


def _rope_tile(r, rope_ref, rows):
    cos, sa, sb = rope_ref[0, rows, :], rope_ref[1, rows, :], rope_ref[2, rows, :]
    outs = []
    for hh in range(r.shape[1] // LANES):
        xh = r[:, hh * LANES:(hh + 1) * LANES]
        outs.append(xh * cos + pltpu.roll(xh, ROT_HALF, 1) * sa
                    + pltpu.roll(xh, LANES - ROT_HALF, 1) * sb)
    return jnp.concatenate(outs, axis=1) if len(outs) > 1 else outs[0]


def _qkv_kernel(x_ref, g_ref, sc_ref, sh_ref, w_ref, rope_ref, h_ref, q_ref, k_ref, v_ref,
                *w_copy, nq):
    j = pl.program_id(1)

    @pl.when(j == 0)
    def _():
        _norm_prologue(x_ref, g_ref, sc_ref, sh_ref, h_ref)

    (wb_ref,) = _bf16_weights((w_ref,), w_copy)

    def run(epilogue):
        for rows in _row_blocks(h_ref.shape[0]):
            epilogue(rows, _bdot(h_ref[rows, :], wb_ref[...]))

    @pl.when(j < nq)
    def _():
        def epilogue(rows, res):
            q_ref[rows, :] = (_rope_tile(res, rope_ref, rows) * Q_SCALE).astype(BF16)
        run(epilogue)

    @pl.when((j >= nq) & (j < 2 * nq))
    def _():
        def epilogue(rows, res):
            k_ref[rows, :] = _rope_tile(res, rope_ref, rows)
        run(epilogue)

    @pl.when(j >= 2 * nq)
    def _():
        def epilogue(rows, res):
            v_ref[rows, :] = res
        run(epilogue)


def _qkv(x, g, mod, w, rope, tm):
    m, d = x.shape
    mb = mod.shape[0]
    width = N_HEADS * QK_DIM
    tn = 512 if w.dtype == F32 else width
    nq = width // tn
    weights = [(w, 0)]
    w_in, w_out, _ = _weight_specs(weights, tn)
    row = lambda i, j: (i, 0)
    out_col = lambda lo: (lambda i, j: (i, jnp.clip(j - lo, 0, nq - 1)))
    return pl.pallas_call(
        functools.partial(_qkv_kernel, nq=nq),
        grid=(m // tm, 3 * nq),
        in_specs=[pl.BlockSpec((tm, d), row),
                  pl.BlockSpec((1, d), lambda i, j: (0, 0)),
                  pl.BlockSpec((mb, d), lambda i, j: (0, 1)),
                  pl.BlockSpec((mb, d), lambda i, j: (0, 0)),
                  *w_in,
                  pl.BlockSpec((3, tm, LANES), lambda i, j: (0, i, 0))],
        out_specs=[pl.BlockSpec((tm, d), row),
                   pl.BlockSpec((tm, tn), out_col(0)),
                   pl.BlockSpec((tm, tn), out_col(nq)),
                   pl.BlockSpec((tm, tn), out_col(2 * nq)),
                   *w_out],
        out_shape=[jax.ShapeDtypeStruct((m, d), BF16),
                   jax.ShapeDtypeStruct((m, width), BF16),
                   jax.ShapeDtypeStruct((m, width), F32),
                   jax.ShapeDtypeStruct((m, width), F32),
                   *_copy_shapes(weights, 3 * width)],
        compiler_params=_params(2, QKV_VMEM_LIMIT_BYTES),
        name="qkv_proj",
    )(x, g, mod, mod, w, rope)


def _conv_rows(z, prev8, w_ref):
    n = z.shape[0]
    zz = jnp.concatenate([prev8, z], axis=0)
    z1 = zz[SUBLANES - 1:SUBLANES - 1 + n]
    z2 = zz[SUBLANES - 2:SUBLANES - 2 + n]
    return z2 * w_ref[0:1, :] + z1 * w_ref[1:2, :] + z * w_ref[2:3, :]


def _conv_state(p0, p1, z, w_ref):
    return p0 * w_ref[0:1, :] + p1 * w_ref[1:2, :] + z * w_ref[2:3, :]


def _last8(z):
    return z[z.shape[0] - SUBLANES:]


def _sconv_seq_kernel(h_ref, wb_ref, wc_ref, wx_ref, cw_ref, u_ref, tail_ref, carry_ref):
    i, j = pl.program_id(0), pl.program_id(1)

    @pl.when(i == 0)
    def _():
        carry_ref[j] = jnp.zeros(carry_ref.shape[1:], F32)

    prev = carry_ref[j]
    for rows in _row_blocks(h_ref.shape[0]):
        h = h_ref[rows, :]
        z = _bdot(h, wc_ref[...]) * _bdot(h, wx_ref[...])
        u_ref[rows, :] = (_bdot(h, wb_ref[...]) * _conv_rows(z, prev, cw_ref)).astype(BF16)
        prev = _last8(z)
    carry_ref[j] = prev
    tail_ref[0] = prev


def _sconv_state_kernel(h_ref, wb_ref, wc_ref, wx_ref, cw_ref, p0_ref, p1_ref, u_ref, z_ref, *w_copy):
    wbb_ref, wcb_ref, wxb_ref = _bf16_weights((wb_ref, wc_ref, wx_ref), w_copy)
    h = h_ref[...]
    z = _bdot(h, wcb_ref[...]) * _bdot(h, wxb_ref[...])
    conv = _conv_state(p0_ref[...], p1_ref[...], z, cw_ref)
    u_ref[...] = (_bdot(h, wbb_ref[...]) * conv).astype(BF16)
    z_ref[...] = z


def _sconv(h, weights, conv_w, tm, state=None):
    m, d = h.shape
    cw = conv_w.shape[1]
    tn = 512 if state is not None else _pick(cw, (1024, 512))
    nj = cw // tn
    w_in, w_out, _ = _weight_specs(weights, tn)
    tile = pl.BlockSpec((tm, tn), lambda i, j: (i, j))
    common = [pl.BlockSpec((tm, d), lambda i, j: (i, 0)), *w_in,
              pl.BlockSpec((CONV_K, tn), lambda i, j: (0, j))]
    operands = (h, *(w for w, _ in weights), conv_w)
    if state is None:
        return pl.pallas_call(
            _sconv_seq_kernel,
            grid=(m // tm, nj),
            in_specs=common,
            out_specs=[tile, pl.BlockSpec((1, SUBLANES, tn), lambda i, j: (i, 0, j))],
            out_shape=[jax.ShapeDtypeStruct((m, cw), BF16),
                       jax.ShapeDtypeStruct((m // tm, SUBLANES, cw), F32)],
            scratch_shapes=[pltpu.VMEM((nj, SUBLANES, tn), F32)],
            compiler_params=_params(2),
            name="sconv_seq",
        )(*operands)
    return pl.pallas_call(
        _sconv_state_kernel,
        grid=(m // tm, nj),
        in_specs=common + [tile, tile],
        out_specs=[tile, tile, *w_out],
        out_shape=[jax.ShapeDtypeStruct((m, cw), BF16), jax.ShapeDtypeStruct((m, cw), F32),
                   *_copy_shapes(weights, cw)],
        compiler_params=_params(2),
        name="sconv_state",
    )(*operands, *state)


def _lambda_full(lq1_ref, lk1_ref, lq2_ref, lk2_ref, lam_init):
    a = jnp.sum(lq1_ref[...] * lk1_ref[...], axis=-1, keepdims=True)
    b = jnp.sum(lq2_ref[...] * lk2_ref[...], axis=-1, keepdims=True)
    return jnp.exp(a) - jnp.exp(b) + lam_init


def _attn_kernel(lq1_ref, lk1_ref, lq2_ref, lk2_ref, g_ref, q_ref, k_ref, v_ref, o_ref,
                 qst_ref, kb_ref, vt_ref, m_ref, acc_ref, s0_ref, s1_ref, *, bq, lam_init):
    qi = pl.program_id(1)
    t = k_ref.shape[0]
    bk = s0_ref.shape[0]

    @pl.when(qi == 0)
    def _():
        for c in range(t // bq):
            rows = slice(c * bq, (c + 1) * bq)
            kb_ref[rows, :] = k_ref[rows, :].astype(BF16)
            vt_ref[0:V_DIM, rows] = v_ref[rows, :].T.astype(BF16)
        vt_ref[V_DIM:, :] = jnp.ones((vt_ref.shape[0] - V_DIM, t), BF16)

    qt = q_ref[...].astype(F32).T
    dim = lax.broadcasted_iota(jnp.int32, qt.shape, 0)
    qst_ref[:, 0:bq] = jnp.where(dim < HEAD_DIM, qt, 0.0).astype(BF16)
    qst_ref[:, bq:2 * bq] = jnp.where(dim >= HEAD_DIM, qt, 0.0).astype(BF16)
    m_ref[...] = jnp.full(m_ref.shape, NEG_INF, F32)
    acc_ref[...] = jnp.zeros(acc_ref.shape, F32)

    def scores(j, s_ref):
        off = pl.multiple_of(j * bk, bk)
        s_ref[...] = _bdot(kb_ref[pl.ds(off, bk), :], qst_ref[...])

    def softmax_pv(j, s_ref, diag_key0=None):
        off = pl.multiple_of(j * bk, bk)
        vtc = vt_ref[:, pl.ds(off, bk)]
        q0 = diag_key0 or 0
        for cols in (slice(q0, bq), slice(bq + q0, 2 * bq)):
            st = s_ref[:, cols]
            if diag_key0 is not None:
                key = lax.broadcasted_iota(jnp.int32, st.shape, 0)
                qry = lax.broadcasted_iota(jnp.int32, st.shape, 1)
                st = jnp.where(key <= qry, st, NEG_INF)
            m_prev = m_ref[:, cols]
            m_new = jnp.maximum(m_prev, jnp.max(st, axis=0, keepdims=True))
            alpha = jnp.exp2(m_prev - m_new)
            pt = jnp.exp2(st - m_new)
            acc_ref[:, cols] = alpha * acc_ref[:, cols] + _bdot(vtc, pt.astype(BF16))
            m_ref[:, cols] = m_new

    def body(jj, carry):
        j = 2 * jj
        scores(j + 1, s1_ref)
        softmax_pv(j, s0_ref)
        scores(j + 2, s0_ref)
        softmax_pv(j + 1, s1_ref)
        return carry

    scores(0, s0_ref)
    lax.fori_loop(0, qi, body, 0)
    scores(2 * qi + 1, s1_ref)
    softmax_pv(2 * qi, s0_ref, diag_key0=0)
    softmax_pv(2 * qi + 1, s1_ref, diag_key0=bk)

    lam = _lambda_full(lq1_ref, lk1_ref, lq2_ref, lk2_ref, lam_init)
    ot = acc_ref[0:V_DIM, :] * (1.0 / acc_ref[V_DIM:V_DIM + 1, :])
    dlt = ot[:, 0:bq] - lam * ot[:, bq:2 * bq]
    ms = jnp.mean(dlt * dlt, axis=0, keepdims=True)
    y = (dlt * lax.rsqrt(ms + SUBLN_EPS)).T * g_ref[...] * (1.0 - lam_init)
    o_ref[...] = y.astype(BF16)


def _prompt_attention(q, k, v, lams, subln_g, lam_init):
    t = q.shape[0]
    bq = _pick(t, (512, 256))
    vec = lambda n: pl.BlockSpec((1, n), lambda h, i: (0, 0))
    return pl.pallas_call(
        functools.partial(_attn_kernel, bq=bq, lam_init=lam_init),
        grid=(N_HEADS, t // bq),
        in_specs=[vec(HEAD_DIM)] * 4 + [vec(V_DIM),
                  pl.BlockSpec((bq, QK_DIM), lambda h, i: (i, h)),
                  pl.BlockSpec((t, QK_DIM), lambda h, i: (0, h)),
                  pl.BlockSpec((t, V_DIM), lambda h, i: (0, h))],
        out_specs=pl.BlockSpec((bq, V_DIM), lambda h, i: (i, h)),
        out_shape=jax.ShapeDtypeStruct((t, ATTN_W), BF16),
        scratch_shapes=[pltpu.VMEM((QK_DIM, 2 * bq), BF16),
                        pltpu.VMEM((t, QK_DIM), BF16),
                        pltpu.VMEM((V_DIM + BF16_ROWS, t), BF16),
                        pltpu.VMEM((1, 2 * bq), F32),
                        pltpu.VMEM((V_DIM + BF16_ROWS, 2 * bq), F32),
                        pltpu.VMEM((bq // 2, 2 * bq), F32),
                        pltpu.VMEM((bq // 2, 2 * bq), F32)],
        compiler_params=_params(2),
        name="prompt_attn",
    )(*lams, subln_g, q, k, v)


def _sattn_kernel(pt_ref, lq1_ref, lk1_ref, lq2_ref, lk2_ref, g_ref, q_ref, kn_ref, vn_ref, *rest,
                  pg, lam_init):
    k_refs, v_refs = rest[:pg], rest[pg:2 * pg]
    o_ref, qm_ref, m_ref, l_ref, acc_ref = rest[2 * pg:]
    step = pl.program_id(1)
    page_rows = PAGE_SIZE * N_HEADS
    lane = lax.broadcasted_iota(jnp.int32, (N_HEADS, QK_DIM), 1)

    @pl.when(step == 0)
    def _():
        q = q_ref[0].astype(F32)
        qm_ref[0:N_HEADS] = jnp.where(lane < HEAD_DIM, q, 0.0)
        qm_ref[N_HEADS:N_MAPS] = jnp.where(lane >= HEAD_DIM, q, 0.0)
        m_ref[...] = jnp.full(m_ref.shape, NEG_INF, F32)
        l_ref[...] = jnp.zeros(l_ref.shape, F32)
        acc_ref[...] = jnp.zeros(acc_ref.shape, F32)

    qm = qm_ref[...]
    qmb = qm.astype(BF16)
    s = jnp.concatenate(
        [lax.dot_general(qmb, k_refs[p][0, 0].reshape(page_rows, QK_DIM).astype(BF16),
                         (((1,), (1,)), ((), ())), preferred_element_type=F32)
         for p in range(pg)], axis=1)
    row = lax.broadcasted_iota(jnp.int32, s.shape, 0)
    col = lax.broadcasted_iota(jnp.int32, s.shape, 1)
    s = jnp.where(col % N_HEADS == row % N_HEADS, s, NEG_INF)
    m_prev = m_ref[...]
    m_new = jnp.maximum(m_prev, jnp.max(s, axis=-1, keepdims=True))
    alpha = jnp.exp2(m_prev - m_new)
    p_all = jnp.exp2(s - m_new)
    l_ref[...] = alpha * l_ref[...] + jnp.sum(p_all, axis=-1, keepdims=True)
    pb = p_all.astype(BF16)
    pv = None
    for p in range(pg):
        part = _bdot(pb[:, p * page_rows:(p + 1) * page_rows],
                     v_refs[p][0, 0].reshape(page_rows, V_DIM).astype(BF16))
        pv = part if pv is None else pv + part
    acc_ref[...] = alpha * acc_ref[...] + pv
    m_ref[...] = m_new

    @pl.when(step == pl.num_programs(1) - 1)
    def _():
        kn = kn_ref[0]
        vn = vn_ref[0]
        kn2 = jnp.concatenate([kn, kn], axis=0)
        vn2 = jnp.concatenate([vn, vn], axis=0)
        s_new = jnp.sum(qm * kn2, axis=-1, keepdims=True)
        m_old = m_ref[...]
        m_fin = jnp.maximum(m_old, s_new)
        a = jnp.exp2(m_old - m_fin)
        p_new = jnp.exp2(s_new - m_fin)
        l_fin = a * l_ref[...] + p_new
        o = (a * acc_ref[...] + p_new * vn2) / l_fin
        lam = _lambda_full(lq1_ref, lk1_ref, lq2_ref, lk2_ref, lam_init)
        dlt = o[0:N_HEADS] - lam * o[N_HEADS:N_MAPS]
        ms = jnp.mean(dlt * dlt, axis=-1, keepdims=True)
        o_ref[0] = dlt * lax.rsqrt(ms + SUBLN_EPS) * g_ref[...] * (1.0 - lam_init)


def _sample_attention(q, k_new, v_new, cache_k, cache_v, layer, page_table, lams, subln_g, lam_init):
    b, n_pages = page_table.shape
    pg = _pick(n_pages, (16, 8, 4, 2, 1))
    vec = lambda n: pl.BlockSpec((1, n), lambda i, s, pt: (0, 0))
    rowspec = pl.BlockSpec((1, N_HEADS, QK_DIM), lambda i, s, pt: (i, 0, 0))
    page = lambda p: pl.BlockSpec((1, 1, PAGE_SIZE, N_HEADS, QK_DIM),
                                  lambda i, s, pt: (layer, pt[i, s * pg + p], 0, 0, 0))
    grid_spec = pltpu.PrefetchScalarGridSpec(
        num_scalar_prefetch=1,
        grid=(b, n_pages // pg),
        in_specs=[vec(HEAD_DIM)] * 4 + [vec(V_DIM), rowspec, rowspec, rowspec]
                 + [page(p) for p in range(pg)] + [page(p) for p in range(pg)],
        out_specs=rowspec,
        scratch_shapes=[pltpu.VMEM((N_MAPS, QK_DIM), F32),
                        pltpu.VMEM((N_MAPS, 1), F32),
                        pltpu.VMEM((N_MAPS, 1), F32),
                        pltpu.VMEM((N_MAPS, V_DIM), F32)],
    )
    return pl.pallas_call(
        functools.partial(_sattn_kernel, pg=pg, lam_init=lam_init),
        grid_spec=grid_spec,
        out_shape=jax.ShapeDtypeStruct((b, N_HEADS, V_DIM), F32),
        compiler_params=_params(2),
        name="sample_attn",
    )(page_table, *lams, subln_g, q, k_new, v_new, *([cache_k] * pg), *([cache_v] * pg))


def _merge_kernel(h_ref, o_ref, u_ref, wga_ref, wgc_ref, wao_ref, wco_ref, out_ref, *w_copy):
    wga, wgc, wao, wco = _bf16_weights((wga_ref, wgc_ref, wao_ref, wco_ref), w_copy)
    for rows in _row_blocks(h_ref.shape[0]):
        h = h_ref[rows, :]
        att = _sigmoid(_bdot(h, wga[...])) * _bdot(o_ref[rows, :].astype(BF16), wao[...])
        conv = _sigmoid(_bdot(h, wgc[...])) * _bdot(u_ref[rows, :], wco[...])
        out_ref[rows, :] = (att + conv).astype(BF16)


def _merge(h, o, u, weights, tm):
    m, d = h.shape
    tn = 256 if weights[0][0].dtype == F32 else _pick(d, (1024, 512))
    w_in, w_out, _ = _weight_specs(weights, tn)
    row = lambda w: pl.BlockSpec((tm, w), lambda i, j: (i, 0))
    return pl.pallas_call(
        _merge_kernel,
        grid=(m // tm, d // tn),
        in_specs=[row(d), row(o.shape[1]), row(u.shape[1]), *w_in],
        out_specs=[pl.BlockSpec((tm, tn), lambda i, j: (i, j)), *w_out],
        out_shape=[jax.ShapeDtypeStruct((m, d), BF16), *_copy_shapes(weights, d)],
        compiler_params=_params(2),
        name="merge",
    )(h, o, u, *(w for w, _ in weights))


def _resid_norm_kernel(a_ref, w_ref, x_ref, gate_ref, g_ref, sc_ref, sh_ref, x1_ref, h_ref, *w_copy):
    (wb_ref,) = _bf16_weights((w_ref,), w_copy)
    for rows in _row_blocks(a_ref.shape[0]):
        x1 = x_ref[rows, :] + _mod_rows(gate_ref, rows) * _bdot(a_ref[rows, :], wb_ref[...])
        x1_ref[rows, :] = x1
        h = _modulated_rmsnorm(x1, g_ref[...], _mod_rows(sc_ref, rows), _mod_rows(sh_ref, rows))
        h_ref[rows, :] = h.astype(BF16)


def _resid_norm(a, w, x, mod, g, tm):
    m, d = x.shape
    k = a.shape[1]
    mb = mod.shape[0]
    weights = [(w, 0)]
    w_in, w_out, _ = _weight_specs(weights, d)
    row = lambda width: pl.BlockSpec((tm, width), lambda i, j: (i, 0))
    modcol = lambda c: pl.BlockSpec((mb, d), lambda i, j: (0, c))
    return pl.pallas_call(
        _resid_norm_kernel,
        grid=(m // tm, 1),
        in_specs=[row(k), *w_in, row(d), modcol(2),
                  pl.BlockSpec((1, d), lambda i, j: (0, 0)), modcol(4), modcol(3)],
        out_specs=[row(d), row(d), *w_out],
        out_shape=[jax.ShapeDtypeStruct((m, d), F32), jax.ShapeDtypeStruct((m, d), BF16),
                   *_copy_shapes(weights, d)],
        compiler_params=_params(2),
        name="resid_norm",
    )(a, w, x, mod, g, mod, mod)


def _ffn_up_seq_kernel(h_ref, wg_ref, wv_ref, cwg_ref, cwv_ref, bg_ref, bv_ref,
                       act_ref, tg_ref, tv_ref, carry_ref):
    i, j = pl.program_id(0), pl.program_id(1)

    @pl.when(i == 0)
    def _():
        carry_ref[j] = jnp.zeros(carry_ref.shape[1:], F32)

    prev_g, prev_v = carry_ref[j, 0], carry_ref[j, 1]
    for rows in _row_blocks(h_ref.shape[0]):
        h = h_ref[rows, :]
        ug, uv = _bdot(h, wg_ref[...]), _bdot(h, wv_ref[...])
        gate = _conv_rows(ug, prev_g, cwg_ref) + bg_ref[...]
        val = _conv_rows(uv, prev_v, cwv_ref) + bv_ref[...]
        act_ref[rows, :] = (gate * _sigmoid(gate) * val).astype(BF16)
        prev_g, prev_v = _last8(ug), _last8(uv)
    carry_ref[j, 0] = prev_g
    carry_ref[j, 1] = prev_v
    tg_ref[0] = prev_g
    tv_ref[0] = prev_v


def _ffn_up_state_kernel(h_ref, wg_ref, wv_ref, cwg_ref, cwv_ref, bg_ref, bv_ref,
                         pg0_ref, pg1_ref, pv0_ref, pv1_ref,
                         act_ref, ug_ref, uv_ref, wgb_ref, wvb_ref):
    _bf16_weights((wg_ref, wv_ref), (wgb_ref, wvb_ref))
    h = h_ref[...]
    ug, uv = _bdot(h, wgb_ref[...]), _bdot(h, wvb_ref[...])
    gate = _conv_state(pg0_ref[...], pg1_ref[...], ug, cwg_ref) + bg_ref[...]
    val = _conv_state(pv0_ref[...], pv1_ref[...], uv, cwv_ref) + bv_ref[...]
    act_ref[...] = (gate * _sigmoid(gate) * val).astype(BF16)
    ug_ref[...] = ug
    uv_ref[...] = uv


def _ffn_up(h, weights, conv_w, conv_b, tm, state=None):
    m, d = h.shape
    f = conv_w.shape[1] // 2
    tn = _pick(f, (512, 256)) if state is not None else _pick(f, (1408, 512, 256))
    nj = f // tn
    w_in, w_out, _ = _weight_specs(weights, tn)
    lo = lambda i, j: (0, j)
    hi = lambda i, j: (0, nj + j)
    tile = pl.BlockSpec((tm, tn), lambda i, j: (i, j))
    tile_hi = pl.BlockSpec((tm, tn), lambda i, j: (i, nj + j))
    conv_b = conv_b.reshape(1, 2 * f)
    common = [pl.BlockSpec((tm, d), lambda i, j: (i, 0)),
              *w_in,
              pl.BlockSpec((CONV_K, tn), lo), pl.BlockSpec((CONV_K, tn), hi),
              pl.BlockSpec((1, tn), lo), pl.BlockSpec((1, tn), hi)]
    args = (h, *(w for w, _ in weights), conv_w, conv_w, conv_b, conv_b)
    if state is None:
        tail = pl.BlockSpec((1, SUBLANES, tn), lambda i, j: (i, 0, j))
        return pl.pallas_call(
            _ffn_up_seq_kernel,
            grid=(m // tm, nj),
            in_specs=common,
            out_specs=[tile, tail, tail],
            out_shape=[jax.ShapeDtypeStruct((m, f), BF16),
                       jax.ShapeDtypeStruct((m // tm, SUBLANES, f), F32),
                       jax.ShapeDtypeStruct((m // tm, SUBLANES, f), F32)],
            scratch_shapes=[pltpu.VMEM((nj, 2, SUBLANES, tn), F32)],
            compiler_params=_params(2),
            name="ffn_up_seq",
        )(*args)
    p0, p1 = state
    return pl.pallas_call(
        _ffn_up_state_kernel,
        grid=(m // tm, nj),
        in_specs=common + [tile, tile, tile_hi, tile_hi],
        out_specs=[tile, tile, tile, *w_out],
        out_shape=[jax.ShapeDtypeStruct((m, f), BF16),
                   jax.ShapeDtypeStruct((m, f), F32),
                   jax.ShapeDtypeStruct((m, f), F32),
                   *_copy_shapes(weights, f)],
        compiler_params=_params(2),
        name="ffn_up_state",
    )(*args, p0, p1, p0, p1)


def _ffn_down_kernel(a_ref, w_ref, x_ref, g2_ref, fg_ref, y_ref, *w_copy):
    kk = pl.program_id(1)
    last = pl.num_programs(1) - 1
    (wb_ref,) = _bf16_weights((w_ref,), w_copy)
    blocks = _row_blocks(a_ref.shape[0])

    @pl.when(kk == 0)
    def _():
        for rows in blocks:
            y_ref[rows, :] = _bdot(a_ref[rows, :], wb_ref[...])

    @pl.when((kk > 0) & (kk < last))
    def _():
        for rows in blocks:
            y_ref[rows, :] += _bdot(a_ref[rows, :], wb_ref[...])

    @pl.when(kk == last)
    def _():
        for rows in blocks:
            acc = y_ref[rows, :] + _bdot(a_ref[rows, :], wb_ref[...])
            x = x_ref[rows, :] + _mod_rows(g2_ref, rows) * acc
            ms = jnp.mean(x * x, axis=-1, keepdims=True)
            y_ref[rows, :] = x * lax.rsqrt(ms + NORM_EPS) * fg_ref[...]


def _ffn_down(act, w_down, x, mod, final_g, tm):
    m, d = x.shape
    f = act.shape[1]
    mb = mod.shape[0]
    tk = _pick(f, (512, 256)) if w_down.dtype == F32 else _pick(f, (1408, 512, 256))
    assert f // tk >= 2, "the accumulate-in-output schedule needs at least two K steps"
    weights = [(w_down, 0)]
    w_in, w_out, w_shapes = _weight_specs(weights, tk, k_tiled=True)
    return pl.pallas_call(
        _ffn_down_kernel,
        grid=(m // tm, f // tk),
        in_specs=[pl.BlockSpec((tm, tk), lambda i, k: (i, k)),
                  *w_in,
                  pl.BlockSpec((tm, d), lambda i, k: (i, 0)),
                  pl.BlockSpec((mb, d), lambda i, k: (0, 5)),
                  pl.BlockSpec((1, d), lambda i, k: (0, 0))],
        out_specs=[pl.BlockSpec((tm, d), lambda i, k: (i, 0)), *w_out],
        out_shape=[jax.ShapeDtypeStruct((m, d), F32), *w_shapes],
        compiler_params=_params(2),
        name="ffn_down",
    )(act, w_down, x, mod, final_g)


def _rope_tables(pos):
    inv_freq = ROPE_THETA ** (-jnp.arange(0, ROT_DIM, 2, dtype=F32) / ROT_DIM)
    ang = pos[:, None] * inv_freq[None, :]
    cos, sin = jnp.cos(ang), jnp.sin(ang)
    rest = jnp.zeros((pos.shape[0], HEAD_DIM - ROT_DIM), F32)
    zero = jnp.zeros_like(sin)
    per_map = jnp.stack([jnp.concatenate([cos, cos, rest + 1.0], axis=1),
                         jnp.concatenate([zero, sin, rest], axis=1),
                         jnp.concatenate([-sin, zero, rest], axis=1)])
    return jnp.tile(per_map, (1, 1, LANES // HEAD_DIM))


def kernel(x_prompt, x_sample, cache_k, cache_v, state_conv, state_ffn, page_table, c_prompt, c_sample, w_ada, b_ada, norm1_g, w_in, lambda_q1, lambda_k1, lambda_q2, lambda_k2, subln_g, w_attn_out, conv_w, w_conv_out, w_o, norm2_g, w_up, ffn_conv_w, ffn_conv_b, w_down, final_g):
    depth = w_ada.shape[0]
    bp, seq, d = x_prompt.shape
    bs, dec_seq, _ = x_sample.shape
    assert bp == 1 and dec_seq == 1, "one prompt sequence and one new token per sample sequence"
    assert depth == 1, "the final norm is fused into the ConvFFN-down kernel of the only layer"
    past_len = page_table.shape[1] * PAGE_SIZE
    qk_w = N_HEADS * QK_DIM
    cw = conv_w.shape[-1]
    f = w_down.shape[1]
    col_conv = 2 * qk_w + ATTN_W
    col_ga = col_conv + 3 * cw
    col_gc = col_ga + d
    tm_p = _pick(seq, (1024, 512, 256))
    tm_s = bs

    xp = x_prompt.reshape(seq, d)
    xs = x_sample.reshape(bs, d)
    pad = (-(bs + bp)) % BF16_ROWS
    c_rows = jnp.concatenate([c_sample, c_prompt, jnp.zeros((pad, d), F32)], axis=0)
    rope_p = _rope_tables(jnp.arange(seq, dtype=F32))
    rope_s = _rope_tables(jnp.full((bs,), past_len, dtype=F32))
    fin_g = final_g.reshape(1, d)
    keep = SUBLANES - (CONV_K - 1)

    outs = [[] for _ in range(8)]
    for li in range(depth):
        lam_init = _lambda_init(li)
        mod = _ada(c_rows, w_ada[li], b_ada[li])
        mod_s, mod_p = mod[:bs], mod[bs:bs + bp]
        lams = tuple(a[li].reshape(1, HEAD_DIM) for a in (lambda_q1, lambda_k1, lambda_q2, lambda_k2))
        g1, g2 = norm1_g[li].reshape(1, d), norm2_g[li].reshape(1, d)
        sub_g = subln_g[li].reshape(1, V_DIM)
        w_in_l, w_up_l = w_in[li], w_up[li]

        sc_prev, sf_prev = state_conv[li], state_ffn[li]
        h, q, k, v, wqkv_b = _qkv(xs, g1, mod_s, w_in_l, rope_s, tm_s)
        u, z, wcb_b, wcc_b, wcx_b = _sconv(
            h, [(w_in_l, col_conv), (w_in_l, col_conv + cw), (w_in_l, col_conv + 2 * cw)],
            conv_w[li], tm_s, state=(sc_prev[:, 0], sc_prev[:, 1]))
        per_head = lambda a: a.reshape(bs, N_HEADS, QK_DIM)
        o = _sample_attention(per_head(q), per_head(k), per_head(v), cache_k, cache_v,
                              li, page_table, lams, sub_g, lam_init).reshape(bs, ATTN_W)
        merged, wga_b, wgc_b, wao_b, wco_b = _merge(
            h, o, u, [(w_in_l, col_ga), (w_in_l, col_gc), (w_attn_out[li], 0), (w_conv_out[li], 0)], tm_s)
        x1, h2, wo_b = _resid_norm(merged, w_o[li], xs, mod_s, g2, tm_s)
        act, ug, uv, wug_b, wuv_b = _ffn_up(h2, [(w_up_l, 0), (w_up_l, f)], ffn_conv_w[li],
                                            ffn_conv_b[li], tm_s, state=(sf_prev[:, 0], sf_prev[:, 1]))
        xs, wdn_b = _ffn_down(act, w_down[li], x1, mod_s, fin_g, tm_s)
        outs[4].append(k.reshape(bs, dec_seq, N_HEADS, QK_DIM))
        outs[5].append(v.reshape(bs, dec_seq, N_HEADS, V_DIM))
        outs[6].append(jnp.stack([sc_prev[:, 1], z], axis=1))
        outs[7].append(jnp.stack([sf_prev[:, 1], jnp.concatenate([ug, uv], axis=-1)], axis=1))

        h, q, k, v = _qkv(xp, g1, mod_p, wqkv_b, rope_p, tm_p)
        u, tail = _sconv(h, [(wcb_b, 0), (wcc_b, 0), (wcx_b, 0)], conv_w[li], tm_p)
        o = _prompt_attention(q, k, v, lams, sub_g, lam_init)
        (merged,) = _merge(h, o, u, [(wga_b, 0), (wgc_b, 0), (wao_b, 0), (wco_b, 0)], tm_p)
        x1, h2 = _resid_norm(merged, wo_b, xp, mod_p, g2, min(tm_p, 512))
        act, tg, tv = _ffn_up(h2, [(wug_b, 0), (wuv_b, 0)], ffn_conv_w[li], ffn_conv_b[li], tm_p)
        (xp,) = _ffn_down(act, wdn_b, x1, mod_p, fin_g, tm_p)
        outs[0].append(k.reshape(bp, seq, N_HEADS, QK_DIM))
        outs[1].append(v.reshape(bp, seq, N_HEADS, V_DIM))
        outs[2].append(tail[-1, keep:].reshape(bp, CONV_K - 1, cw))
        outs[3].append(jnp.concatenate([tg[-1, keep:], tv[-1, keep:]], axis=-1).reshape(bp, CONV_K - 1, 2 * f))

    return (xp.reshape(bp, seq, d), xs.reshape(bs, dec_seq, d),
            *(jnp.stack(o) for o in outs))
```

```python
import functools
import math

import jax
import jax.numpy as jnp
from jax import lax
from jax.experimental import pallas as pl
from jax.experimental.pallas import tpu as pltpu

F32 = jnp.float32
BF16 = jnp.bfloat16

N_HEADS = 8
HEAD_DIM = 64
QK_DIM = 2 * HEAD_DIM
V_DIM = 2 * HEAD_DIM
ROT_DIM = HEAD_DIM // 4
ROT_HALF = ROT_DIM // 2
ROPE_THETA = 500000.0
ATTN_W = N_HEADS * V_DIM
CONV_K = 3
NORM_EPS = 1e-6
SUBLN_EPS = 1e-5
ATTN_SCALE = HEAD_DIM ** -0.5
Q_SCALE = ATTN_SCALE * math.log2(math.e)
PAGE_SIZE = 128
N_MAPS = 2 * N_HEADS

LANES = 128
SUBLANES = 8
BF16_ROWS = 2 * SUBLANES
MXU_DIM = 256
VMEM_LIMIT_BYTES = 56 * 1024 * 1024
QKV_VMEM_LIMIT_BYTES = 60 * 1024 * 1024
ROW_SUB = MXU_DIM
CAST_ROWS = 256
NEG_INF = float("-inf")


def _lambda_init(layer):
    return 0.8 - 0.6 * math.exp(-0.3 * layer)


def _params(n_axes, vmem_limit=VMEM_LIMIT_BYTES):
    return pltpu.CompilerParams(dimension_semantics=("arbitrary",) * n_axes,
                                vmem_limit_bytes=vmem_limit)


def _sigmoid(x):
    return 1.0 / (1.0 + jnp.exp(-x))


def _bdot(a, b):
    return jnp.dot(a, b, preferred_element_type=F32)


def _pick(n, candidates):
    for c in candidates:
        if n % c == 0:
            return c
    return n


def _row_blocks(tm):
    return [slice(r, min(r + ROW_SUB, tm)) for r in range(0, tm, ROW_SUB)]


def _cast_weight(w_ref, wb_ref):
    k = w_ref.shape[0]
    for r in range(0, k, CAST_ROWS):
        rows = slice(r, min(r + CAST_ROWS, k))
        wb_ref[rows, :] = w_ref[rows, :].astype(BF16)


def _bf16_weights(w_refs, copy_refs):
    if not copy_refs:
        return w_refs
    for w_ref, wb_ref in zip(w_refs, copy_refs, strict=True):
        _cast_weight(w_ref, wb_ref)
    return copy_refs


def _weight_specs(weights, tn, k_tiled=False):
    in_specs, out_specs, out_shapes = [], [], []
    for w, col in weights:
        k, n = w.shape
        if k_tiled:
            in_specs.append(pl.BlockSpec((tn, n), lambda i, j: (j, 0)))
        else:
            in_specs.append(pl.BlockSpec((k, tn), lambda i, j, c=col // tn: (0, c + j)))
    if weights[0][0].dtype == F32:
        for w, _ in weights:
            k, n = w.shape
            if k_tiled:
                out_specs.append(pl.BlockSpec((tn, n), lambda i, j: (j, 0)))
                out_shapes.append(jax.ShapeDtypeStruct((k, n), BF16))
            else:
                out_specs.append(pl.BlockSpec((k, tn), lambda i, j: (0, j)))
    return in_specs, out_specs, out_shapes


def _copy_shapes(weights, width):
    if weights[0][0].dtype != F32:
        return []
    return [jax.ShapeDtypeStruct((w.shape[0], width), BF16) for w, _ in weights]


def _mod_rows(ref, rows):
    return ref[...] if ref.shape[0] == 1 else ref[rows, :]


def _ada_kernel(c_ref, w_ref, b_ref, o_ref, wb_ref):
    c = c_ref[...]
    s = (c * _sigmoid(c)).astype(BF16)
    _cast_weight(w_ref, wb_ref)
    o_ref[...] = _bdot(s, wb_ref[...]) + b_ref[...]


def _ada(c_rows, w_ada, b_ada):
    m, d = c_rows.shape
    n = w_ada.shape[1]
    tn = _pick(n, (1024, 512, 256, 128))
    return pl.pallas_call(
        _ada_kernel,
        grid=(n // tn,),
        in_specs=[pl.BlockSpec((m, d), lambda j: (0, 0)),
                  pl.BlockSpec((d, tn), lambda j: (0, j)),
                  pl.BlockSpec((1, tn), lambda j: (0, j))],
        out_specs=pl.BlockSpec((m, tn), lambda j: (0, j)),
        out_shape=jax.ShapeDtypeStruct((m, n), F32),
        scratch_shapes=[pltpu.VMEM((d, tn), BF16)],
        compiler_params=_params(1),
        name="ada_mod",
    )(c_rows, w_ada, b_ada.reshape(1, n))


def _modulated_rmsnorm(x, g, sc, sh):
    ms = jnp.mean(x * x, axis=-1, keepdims=True)
    y = x * lax.rsqrt(ms + NORM_EPS)
    return y * g * (1.0 + sc) + sh


def _norm_prologue(x_ref, g_ref, sc_ref, sh_ref, h_ref):
    for rows in _row_blocks(x_ref.shape[0]):
        h = _modulated_rmsnorm(x_ref[rows, :], g_ref[...], _mod_rows(sc_ref, rows), _mod_rows(sh_ref, rows))
        h_ref[rows, :] = h.astype(BF16)


def _rope_tile(r, rope_ref, rows):
    cos, sa, sb = rope_ref[0, rows, :], rope_ref[1, rows, :], rope_ref[2, rows, :]
    outs = []
    for hh in range(r.shape[1] // LANES):
        xh = r[:, hh * LANES:(hh + 1) * LANES]
        outs.append(xh * cos + pltpu.roll(xh, ROT_HALF, 1) * sa
                    + pltpu.roll(xh, LANES - ROT_HALF, 1) * sb)
    return jnp.concatenate(outs, axis=1) if len(outs) > 1 else outs[0]


def _qkv_kernel(x_ref, g_ref, sc_ref, sh_ref, w_ref, rope_ref, h_ref, q_ref, k_ref, v_ref,
                *w_copy, nq):
    j = pl.program_id(1)

    @pl.when(j == 0)
    def _():
        _norm_prologue(x_ref, g_ref, sc_ref, sh_ref, h_ref)

    (wb_ref,) = _bf16_weights((w_ref,), w_copy)

    def run(epilogue):
        for rows in _row_blocks(h_ref.shape[0]):
            epilogue(rows, _bdot(h_ref[rows, :], wb_ref[...]))

    @pl.when(j < nq)
    def _():
        def epilogue(rows, res):
            q_ref[rows, :] = (_rope_tile(res, rope_ref, rows) * Q_SCALE).astype(BF16)
        run(epilogue)

    @pl.when((j >= nq) & (j < 2 * nq))
    def _():
        def epilogue(rows, res):
            k_ref[rows, :] = _rope_tile(res, rope_ref, rows)
        run(epilogue)

    @pl.when(j >= 2 * nq)
    def _():
        def epilogue(rows, res):
            v_ref[rows, :] = res
        run(epilogue)


def _qkv(x, g, mod, w, rope, tm):
    m, d = x.shape
    mb = mod.shape[0]
    width = N_HEADS * QK_DIM
    tn = 512 if w.dtype == F32 else width
    nq = width // tn
    weights = [(w, 0)]
    w_in, w_out, _ = _weight_specs(weights, tn)
    row = lambda i, j: (i, 0)
    out_col = lambda lo: (lambda i, j: (i, jnp.clip(j - lo, 0, nq - 1)))
    return pl.pallas_call(
        functools.partial(_qkv_kernel, nq=nq),
        grid=(m // tm, 3 * nq),
        in_specs=[pl.BlockSpec((tm, d), row),
                  pl.BlockSpec((1, d), lambda i, j: (0, 0)),
                  pl.BlockSpec((mb, d), lambda i, j: (0, 1)),
                  pl.BlockSpec((mb, d), lambda i, j: (0, 0)),
                  *w_in,
                  pl.BlockSpec((3, tm, LANES), lambda i, j: (0, i, 0))],
        out_specs=[pl.BlockSpec((tm, d), row),
                   pl.BlockSpec((tm, tn), out_col(0)),
                   pl.BlockSpec((tm, tn), out_col(nq)),
                   pl.BlockSpec((tm, tn), out_col(2 * nq)),
                   *w_out],
        out_shape=[jax.ShapeDtypeStruct((m, d), BF16),
                   jax.ShapeDtypeStruct((m, width), BF16),
                   jax.ShapeDtypeStruct((m, width), F32),
                   jax.ShapeDtypeStruct((m, width), F32),
                   *_copy_shapes(weights, 3 * width)],
        compiler_params=_params(2, QKV_VMEM_LIMIT_BYTES),
        name="qkv_proj",
    )(x, g, mod, mod, w, rope)


def _conv_rows(z, prev8, w_ref):
    n = z.shape[0]
    zz = jnp.concatenate([prev8, z], axis=0)
    z1 = zz[SUBLANES - 1:SUBLANES - 1 + n]
    z2 = zz[SUBLANES - 2:SUBLANES - 2 + n]
    return z2 * w_ref[0:1, :] + z1 * w_ref[1:2, :] + z * w_ref[2:3, :]


def _conv_state(p0, p1, z, w_ref):
    return p0 * w_ref[0:1, :] + p1 * w_ref[1:2, :] + z * w_ref[2:3, :]


def _last8(z):
    return z[z.shape[0] - SUBLANES:]


def _sconv_seq_kernel(h_ref, wb_ref, wc_ref, wx_ref, cw_ref, u_ref, tail_ref, carry_ref):
    i, j = pl.program_id(0), pl.program_id(1)

    @pl.when(i == 0)
    def _():
        carry_ref[j] = jnp.zeros(carry_ref.shape[1:], F32)

    prev = carry_ref[j]
    for rows in _row_blocks(h_ref.shape[0]):
        h = h_ref[rows, :]
        z = _bdot(h, wc_ref[...]) * _bdot(h, wx_ref[...])
        u_ref[rows, :] = (_bdot(h, wb_ref[...]) * _conv_rows(z, prev, cw_ref)).astype(BF16)
        prev = _last8(z)
    carry_ref[j] = prev
    tail_ref[0] = prev


def _sconv_state_kernel(h_ref, wb_ref, wc_ref, wx_ref, cw_ref, p0_ref, p1_ref, u_ref, z_ref, *w_copy):
    wbb_ref, wcb_ref, wxb_ref = _bf16_weights((wb_ref, wc_ref, wx_ref), w_copy)
    h = h_ref[...]
    z = _bdot(h, wcb_ref[...]) * _bdot(h, wxb_ref[...])
    conv = _conv_state(p0_ref[...], p1_ref[...], z, cw_ref)
    u_ref[...] = (_bdot(h, wbb_ref[...]) * conv).astype(BF16)
    z_ref[...] = z


def _sconv(h, weights, conv_w, tm, state=None):
    m, d = h.shape
    cw = conv_w.shape[1]
    tn = 512 if state is not None else _pick(cw, (1024, 512))
    nj = cw // tn
    w_in, w_out, _ = _weight_specs(weights, tn)
    tile = pl.BlockSpec((tm, tn), lambda i, j: (i, j))
    common = [pl.BlockSpec((tm, d), lambda i, j: (i, 0)), *w_in,
              pl.BlockSpec((CONV_K, tn), lambda i, j: (0, j))]
    operands = (h, *(w for w, _ in weights), conv_w)
    if state is None:
        return pl.pallas_call(
            _sconv_seq_kernel,
            grid=(m // tm, nj),
            in_specs=common,
            out_specs=[tile, pl.BlockSpec((1, SUBLANES, tn), lambda i, j: (i, 0, j))],
            out_shape=[jax.ShapeDtypeStruct((m, cw), BF16),
                       jax.ShapeDtypeStruct((m // tm, SUBLANES, cw), F32)],
            scratch_shapes=[pltpu.VMEM((nj, SUBLANES, tn), F32)],
            compiler_params=_params(2),
            name="sconv_seq",
        )(*operands)
    return pl.pallas_call(
        _sconv_state_kernel,
        grid=(m // tm, nj),
        in_specs=common + [tile, tile],
        out_specs=[tile, tile, *w_out],
        out_shape=[jax.ShapeDtypeStruct((m, cw), BF16), jax.ShapeDtypeStruct((m, cw), F32),
                   *_copy_shapes(weights, cw)],
        compiler_params=_params(2),
        name="sconv_state",
    )(*operands, *state)


def _lambda_full(lq1_ref, lk1_ref, lq2_ref, lk2_ref, lam_init):
    a = jnp.sum(lq1_ref[...] * lk1_ref[...], axis=-1, keepdims=True)
    b = jnp.sum(lq2_ref[...] * lk2_ref[...], axis=-1, keepdims=True)
    return jnp.exp(a) - jnp.exp(b) + lam_init


def _attn_kernel(lq1_ref, lk1_ref, lq2_ref, lk2_ref, g_ref, q_ref, k_ref, v_ref, o_ref,
                 qst_ref, kb_ref, vt_ref, m_ref, acc_ref, s0_ref, s1_ref, *, bq, lam_init):
    qi = pl.program_id(1)
    t = k_ref.shape[0]
    bk = s0_ref.shape[0]

    @pl.when(qi == 0)
    def _():
        for c in range(t // bq):
            rows = slice(c * bq, (c + 1) * bq)
            kb_ref[rows, :] = k_ref[rows, :].astype(BF16)
            vt_ref[0:V_DIM, rows] = v_ref[rows, :].T.astype(BF16)
        vt_ref[V_DIM:, :] = jnp.ones((vt_ref.shape[0] - V_DIM, t), BF16)

    qt = q_ref[...].astype(F32).T
    dim = lax.broadcasted_iota(jnp.int32, qt.shape, 0)
    qst_ref[:, 0:bq] = jnp.where(dim < HEAD_DIM, qt, 0.0).astype(BF16)
    qst_ref[:, bq:2 * bq] = jnp.where(dim >= HEAD_DIM, qt, 0.0).astype(BF16)
    m_ref[...] = jnp.full(m_ref.shape, NEG_INF, F32)
    acc_ref[...] = jnp.zeros(acc_ref.shape, F32)

    def scores(j, s_ref):
        off = pl.multiple_of(j * bk, bk)
        s_ref[...] = _bdot(kb_ref[pl.ds(off, bk), :], qst_ref[...])

    def softmax_pv(j, s_ref, diag_key0=None):
        off = pl.multiple_of(j * bk, bk)
        vtc = vt_ref[:, pl.ds(off, bk)]
        q0 = diag_key0 or 0
        for cols in (slice(q0, bq), slice(bq + q0, 2 * bq)):
            st = s_ref[:, cols]
            if diag_key0 is not None:
                key = lax.broadcasted_iota(jnp.int32, st.shape, 0)
                qry = lax.broadcasted_iota(jnp.int32, st.shape, 1)
                st = jnp.where(key <= qry, st, NEG_INF)
            m_prev = m_ref[:, cols]
            m_new = jnp.maximum(m_prev, jnp.max(st, axis=0, keepdims=True))
            alpha = jnp.exp2(m_prev - m_new)
            pt = jnp.exp2(st - m_new)
            acc_ref[:, cols] = alpha * acc_ref[:, cols] + _bdot(vtc, pt.astype(BF16))
            m_ref[:, cols] = m_new

    def body(jj, carry):
        j = 2 * jj
        scores(j + 1, s1_ref)
        softmax_pv(j, s0_ref)
        scores(j + 2, s0_ref)
        softmax_pv(j + 1, s1_ref)
        return carry

    scores(0, s0_ref)
    lax.fori_loop(0, qi, body, 0)
    scores(2 * qi + 1, s1_ref)
    softmax_pv(2 * qi, s0_ref, diag_key0=0)
    softmax_pv(2 * qi + 1, s1_ref, diag_key0=bk)

    lam = _lambda_full(lq1_ref, lk1_ref, lq2_ref, lk2_ref, lam_init)
    ot = acc_ref[0:V_DIM, :] * (1.0 / acc_ref[V_DIM:V_DIM + 1, :])
    dlt = ot[:, 0:bq] - lam * ot[:, bq:2 * bq]
    ms = jnp.mean(dlt * dlt, axis=0, keepdims=True)
    y = (dlt * lax.rsqrt(ms + SUBLN_EPS)).T * g_ref[...] * (1.0 - lam_init)
    o_ref[...] = y.astype(BF16)


def _prompt_attention(q, k, v, lams, subln_g, lam_init):
    t = q.shape[0]
    bq = _pick(t, (512, 256))
    vec = lambda n: pl.BlockSpec((1, n), lambda h, i: (0, 0))
    return pl.pallas_call(
        functools.partial(_attn_kernel, bq=bq, lam_init=lam_init),
        grid=(N_HEADS, t // bq),
        in_specs=[vec(HEAD_DIM)] * 4 + [vec(V_DIM),
                  pl.BlockSpec((bq, QK_DIM), lambda h, i: (i, h)),
                  pl.BlockSpec((t, QK_DIM), lambda h, i: (0, h)),
                  pl.BlockSpec((t, V_DIM), lambda h, i: (0, h))],
        out_specs=pl.BlockSpec((bq, V_DIM), lambda h, i: (i, h)),
        out_shape=jax.ShapeDtypeStruct((t, ATTN_W), BF16),
        scratch_shapes=[pltpu.VMEM((QK_DIM, 2 * bq), BF16),
                        pltpu.VMEM((t, QK_DIM), BF16),
                        pltpu.VMEM((V_DIM + BF16_ROWS, t), BF16),
                        pltpu.VMEM((1, 2 * bq), F32),
                        pltpu.VMEM((V_DIM + BF16_ROWS, 2 * bq), F32),
                        pltpu.VMEM((bq // 2, 2 * bq), F32),
                        pltpu.VMEM((bq // 2, 2 * bq), F32)],
        compiler_params=_params(2),
        name="prompt_attn",
    )(*lams, subln_g, q, k, v)


def _sattn_kernel(pt_ref, lq1_ref, lk1_ref, lq2_ref, lk2_ref, g_ref, q_ref, kn_ref, vn_ref, *rest,
                  pg, lam_init):
    k_refs, v_refs = rest[:pg], rest[pg:2 * pg]
    o_ref, qm_ref, m_ref, l_ref, acc_ref = rest[2 * pg:]
    step = pl.program_id(1)
    page_rows = PAGE_SIZE * N_HEADS
    lane = lax.broadcasted_iota(jnp.int32, (N_HEADS, QK_DIM), 1)

    @pl.when(step == 0)
    def _():
        q = q_ref[0].astype(F32)
        qm_ref[0:N_HEADS] = jnp.where(lane < HEAD_DIM, q, 0.0)
        qm_ref[N_HEADS:N_MAPS] = jnp.where(lane >= HEAD_DIM, q, 0.0)
        m_ref[...] = jnp.full(m_ref.shape, NEG_INF, F32)
        l_ref[...] = jnp.zeros(l_ref.shape, F32)
        acc_ref[...] = jnp.zeros(acc_ref.shape, F32)

    qm = qm_ref[...]
    qmb = qm.astype(BF16)
    s = jnp.concatenate(
        [lax.dot_general(qmb, k_refs[p][0, 0].reshape(page_rows, QK_DIM).astype(BF16),
                         (((1,), (1,)), ((), ())), preferred_element_type=F32)
         for p in range(pg)], axis=1)
    row = lax.broadcasted_iota(jnp.int32, s.shape, 0)
    col = lax.broadcasted_iota(jnp.int32, s.shape, 1)
    s = jnp.where(col % N_HEADS == row % N_HEADS, s, NEG_INF)
    m_prev = m_ref[...]
    m_new = jnp.maximum(m_prev, jnp.max(s, axis=-1, keepdims=True))
    alpha = jnp.exp2(m_prev - m_new)
    p_all = jnp.exp2(s - m_new)
    l_ref[...] = alpha * l_ref[...] + jnp.sum(p_all, axis=-1, keepdims=True)
    pb = p_all.astype(BF16)
    pv = None
    for p in range(pg):
        part = _bdot(pb[:, p * page_rows:(p + 1) * page_rows],
                     v_refs[p][0, 0].reshape(page_rows, V_DIM).astype(BF16))
        pv = part if pv is None else pv + part
    acc_ref[...] = alpha * acc_ref[...] + pv
    m_ref[...] = m_new

    @pl.when(step == pl.num_programs(1) - 1)
    def _():
        kn = kn_ref[0]
        vn = vn_ref[0]
        kn2 = jnp.concatenate([kn, kn], axis=0)
        vn2 = jnp.concatenate([vn, vn], axis=0)
        s_new = jnp.sum(qm * kn2, axis=-1, keepdims=True)
        m_old = m_ref[...]
        m_fin = jnp.maximum(m_old, s_new)
        a = jnp.exp2(m_old - m_fin)
        p_new = jnp.exp2(s_new - m_fin)
        l_fin = a * l_ref[...] + p_new
        o = (a * acc_ref[...] + p_new * vn2) / l_fin
        lam = _lambda_full(lq1_ref, lk1_ref, lq2_ref, lk2_ref, lam_init)
        dlt = o[0:N_HEADS] - lam * o[N_HEADS:N_MAPS]
        ms = jnp.mean(dlt * dlt, axis=-1, keepdims=True)
        o_ref[0] = dlt * lax.rsqrt(ms + SUBLN_EPS) * g_ref[...] * (1.0 - lam_init)


def _sample_attention(q, k_new, v_new, cache_k, cache_v, layer, page_table, lams, subln_g, lam_init):
    b, n_pages = page_table.shape
    pg = _pick(n_pages, (16, 8, 4, 2, 1))
    vec = lambda n: pl.BlockSpec((1, n), lambda i, s, pt: (0, 0))
    rowspec = pl.BlockSpec((1, N_HEADS, QK_DIM), lambda i, s, pt: (i, 0, 0))
    page = lambda p: pl.BlockSpec((1, 1, PAGE_SIZE, N_HEADS, QK_DIM),
                                  lambda i, s, pt: (layer, pt[i, s * pg + p], 0, 0, 0))
    grid_spec = pltpu.PrefetchScalarGridSpec(
        num_scalar_prefetch=1,
        grid=(b, n_pages // pg),
        in_specs=[vec(HEAD_DIM)] * 4 + [vec(V_DIM), rowspec, rowspec, rowspec]
                 + [page(p) for p in range(pg)] + [page(p) for p in range(pg)],
        out_specs=rowspec,
        scratch_shapes=[pltpu.VMEM((N_MAPS, QK_DIM), F32),
                        pltpu.VMEM((N_MAPS, 1), F32),
                        pltpu.VMEM((N_MAPS, 1), F32),
                        pltpu.VMEM((N_MAPS, V_DIM), F32)],
    )
    return pl.pallas_call(
        functools.partial(_sattn_kernel, pg=pg, lam_init=lam_init),
        grid_spec=grid_spec,
        out_shape=jax.ShapeDtypeStruct((b, N_HEADS, V_DIM), F32),
        compiler_params=_params(2),
        name="sample_attn",
    )(page_table, *lams, subln_g, q, k_new, v_new, *([cache_k] * pg), *([cache_v] * pg))


def _merge_kernel(h_ref, o_ref, u_ref, wga_ref, wgc_ref, wao_ref, wco_ref, out_ref, *w_copy):
    wga, wgc, wao, wco = _bf16_weights((wga_ref, wgc_ref, wao_ref, wco_ref), w_copy)
    for rows in _row_blocks(h_ref.shape[0]):
        h = h_ref[rows, :]
        att = _sigmoid(_bdot(h, wga[...])) * _bdot(o_ref[rows, :].astype(BF16), wao[...])
        conv = _sigmoid(_bdot(h, wgc[...])) * _bdot(u_ref[rows, :], wco[...])
        out_ref[rows, :] = (att + conv).astype(BF16)


def _merge(h, o, u, weights, tm):
    m, d = h.shape
    tn = 256 if weights[0][0].dtype == F32 else _pick(d, (1024, 512))
    w_in, w_out, _ = _weight_specs(weights, tn)
    row = lambda w: pl.BlockSpec((tm, w), lambda i, j: (i, 0))
    return pl.pallas_call(
        _merge_kernel,
        grid=(m // tm, d // tn),
        in_specs=[row(d), row(o.shape[1]), row(u.shape[1]), *w_in],
        out_specs=[pl.BlockSpec((tm, tn), lambda i, j: (i, j)), *w_out],
        out_shape=[jax.ShapeDtypeStruct((m, d), BF16), *_copy_shapes(weights, d)],
        compiler_params=_params(2),
        name="merge",
    )(h, o, u, *(w for w, _ in weights))


def _resid_norm_kernel(a_ref, w_ref, x_ref, gate_ref, g_ref, sc_ref, sh_ref, x1_ref, h_ref, *w_copy):
    (wb_ref,) = _bf16_weights((w_ref,), w_copy)
    for rows in _row_blocks(a_ref.shape[0]):
        x1 = x_ref[rows, :] + _mod_rows(gate_ref, rows) * _bdot(a_ref[rows, :], wb_ref[...])
        x1_ref[rows, :] = x1
        h = _modulated_rmsnorm(x1, g_ref[...], _mod_rows(sc_ref, rows), _mod_rows(sh_ref, rows))
        h_ref[rows, :] = h.astype(BF16)


def _resid_norm(a, w, x, mod, g, tm):
    m, d = x.shape
    k = a.shape[1]
    mb = mod.shape[0]
    weights = [(w, 0)]
    w_in, w_out, _ = _weight_specs(weights, d)
    row = lambda width: pl.BlockSpec((tm, width), lambda i, j: (i, 0))
    modcol = lambda c: pl.BlockSpec((mb, d), lambda i, j: (0, c))
    return pl.pallas_call(
        _resid_norm_kernel,
        grid=(m // tm, 1),
        in_specs=[row(k), *w_in, row(d), modcol(2),
                  pl.BlockSpec((1, d), lambda i, j: (0, 0)), modcol(4), modcol(3)],
        out_specs=[row(d), row(d), *w_out],
        out_shape=[jax.ShapeDtypeStruct((m, d), F32), jax.ShapeDtypeStruct((m, d), BF16),
                   *_copy_shapes(weights, d)],
        compiler_params=_params(2),
        name="resid_norm",
    )(a, w, x, mod, g, mod, mod)


def _ffn_up_seq_kernel(h_ref, w_ref, cwg_ref, cwv_ref, bg_ref, bv_ref,
                       act_ref, tg_ref, tv_ref, carry_ref):
    i, j = pl.program_id(0), pl.program_id(1)
    tn = act_ref.shape[1]

    @pl.when(i == 0)
    def _():
        carry_ref[j] = jnp.zeros(carry_ref.shape[1:], F32)

    prev_g, prev_v = carry_ref[j, 0], carry_ref[j, 1]
    for rows in _row_blocks(h_ref.shape[0]):
        u = _bdot(h_ref[rows, :], w_ref[...])
        ug, uv = u[:, 0:tn], u[:, tn:2 * tn]
        gate = _conv_rows(ug, prev_g, cwg_ref) + bg_ref[...]
        val = _conv_rows(uv, prev_v, cwv_ref) + bv_ref[...]
        act_ref[rows, :] = (gate * _sigmoid(gate) * val).astype(BF16)
        prev_g, prev_v = _last8(ug), _last8(uv)
    carry_ref[j, 0] = prev_g
    carry_ref[j, 1] = prev_v
    tg_ref[0] = prev_g
    tv_ref[0] = prev_v


def _ffn_up_state_kernel(h_ref, wg_ref, wv_ref, cwg_ref, cwv_ref, bg_ref, bv_ref,
                         pg0_ref, pg1_ref, pv0_ref, pv1_ref,
                         act_ref, ug_ref, uv_ref, wcat_ref):
    tn = act_ref.shape[1]
    _cast_weight(wg_ref, wcat_ref.at[:, 0:tn])
    _cast_weight(wv_ref, wcat_ref.at[:, tn:2 * tn])
    u = _bdot(h_ref[...], wcat_ref[...])
    ug, uv = u[:, 0:tn], u[:, tn:2 * tn]
    gate = _conv_state(pg0_ref[...], pg1_ref[...], ug, cwg_ref) + bg_ref[...]
    val = _conv_state(pv0_ref[...], pv1_ref[...], uv, cwv_ref) + bv_ref[...]
    act_ref[...] = (gate * _sigmoid(gate) * val).astype(BF16)
    ug_ref[...] = ug
    uv_ref[...] = uv


def _ffn_up(h, w, conv_w, conv_b, tm, state=None):
    m, d = h.shape
    f = conv_w.shape[1] // 2
    tn = _pick(f, (1408, 512, 256))
    nj = f // tn
    lo = lambda i, j: (0, j)
    hi = lambda i, j: (0, nj + j)
    tile = pl.BlockSpec((tm, tn), lambda i, j: (i, j))
    tile_hi = pl.BlockSpec((tm, tn), lambda i, j: (i, nj + j))
    wcat = pl.BlockSpec((d, 2 * tn), lo)
    conv_b = conv_b.reshape(1, 2 * f)
    small = [pl.BlockSpec((CONV_K, tn), lo), pl.BlockSpec((CONV_K, tn), hi),
             pl.BlockSpec((1, tn), lo), pl.BlockSpec((1, tn), hi)]
    act = pl.BlockSpec((tm, d), lambda i, j: (i, 0))
    if state is None:
        tail = pl.BlockSpec((1, SUBLANES, tn), lambda i, j: (i, 0, j))
        return pl.pallas_call(
            _ffn_up_seq_kernel,
            grid=(m // tm, nj),
            in_specs=[act, wcat, *small],
            out_specs=[tile, tail, tail],
            out_shape=[jax.ShapeDtypeStruct((m, f), BF16),
                       jax.ShapeDtypeStruct((m // tm, SUBLANES, f), F32),
                       jax.ShapeDtypeStruct((m // tm, SUBLANES, f), F32)],
            scratch_shapes=[pltpu.VMEM((nj, 2, SUBLANES, tn), F32)],
            compiler_params=_params(2),
            name="ffn_up_seq",
        )(h, w, conv_w, conv_w, conv_b, conv_b)
    p0, p1 = state
    once = lambda index_map: pl.BlockSpec((d, tn), index_map, pipeline_mode=pl.Buffered(1))
    return pl.pallas_call(
        _ffn_up_state_kernel,
        grid=(m // tm, nj),
        in_specs=[act, once(lo), once(hi), *small, tile, tile, tile_hi, tile_hi],
        out_specs=[tile, tile, tile, wcat],
        out_shape=[jax.ShapeDtypeStruct((m, f), BF16),
                   jax.ShapeDtypeStruct((m, f), F32),
                   jax.ShapeDtypeStruct((m, f), F32),
                   jax.ShapeDtypeStruct((d, 2 * f), BF16)],
        compiler_params=_params(2),
        name="ffn_up_state",
    )(h, w, w, conv_w, conv_w, conv_b, conv_b, p0, p1, p0, p1)


def _ffn_down_kernel(a_ref, w_ref, x_ref, g2_ref, fg_ref, y_ref, *w_copy):
    kk = pl.program_id(1)
    last = pl.num_programs(1) - 1
    (wb_ref,) = _bf16_weights((w_ref,), w_copy)
    blocks = _row_blocks(a_ref.shape[0])

    @pl.when(kk == 0)
    def _():
        for rows in blocks:
            y_ref[rows, :] = _bdot(a_ref[rows, :], wb_ref[...])

    @pl.when((kk > 0) & (kk < last))
    def _():
        for rows in blocks:
            y_ref[rows, :] += _bdot(a_ref[rows, :], wb_ref[...])

    @pl.when(kk == last)
    def _():
        for rows in blocks:
            acc = y_ref[rows, :] + _bdot(a_ref[rows, :], wb_ref[...])
            x = x_ref[rows, :] + _mod_rows(g2_ref, rows) * acc
            ms = jnp.mean(x * x, axis=-1, keepdims=True)
            y_ref[rows, :] = x * lax.rsqrt(ms + NORM_EPS) * fg_ref[...]


def _ffn_down(act, w_down, x, mod, final_g, tm):
    m, d = x.shape
    f = act.shape[1]
    mb = mod.shape[0]
    tk = _pick(f, (512, 256)) if w_down.dtype == F32 else _pick(f, (1408, 512, 256))
    assert f // tk >= 2, "the accumulate-in-output schedule needs at least two K steps"
    weights = [(w_down, 0)]
    w_in, w_out, w_shapes = _weight_specs(weights, tk, k_tiled=True)
    return pl.pallas_call(
        _ffn_down_kernel,
        grid=(m // tm, f // tk),
        in_specs=[pl.BlockSpec((tm, tk), lambda i, k: (i, k)),
                  *w_in,
                  pl.BlockSpec((tm, d), lambda i, k: (i, 0)),
                  pl.BlockSpec((mb, d), lambda i, k: (0, 5)),
                  pl.BlockSpec((1, d), lambda i, k: (0, 0))],
        out_specs=[pl.BlockSpec((tm, d), lambda i, k: (i, 0)), *w_out],
        out_shape=[jax.ShapeDtypeStruct((m, d), F32), *w_shapes],
        compiler_params=_params(2),
        name="ffn_down",
    )(act, w_down, x, mod, final_g)


def _rope_tables(pos):
    inv_freq = ROPE_THETA ** (-jnp.arange(0, ROT_DIM, 2, dtype=F32) / ROT_DIM)
    ang = pos[:, None] * inv_freq[None, :]
    cos, sin = jnp.cos(ang), jnp.sin(ang)
    rest = jnp.zeros((pos.shape[0], HEAD_DIM - ROT_DIM), F32)
    zero = jnp.zeros_like(sin)
    per_map = jnp.stack([jnp.concatenate([cos, cos, rest + 1.0], axis=1),
                         jnp.concatenate([zero, sin, rest], axis=1),
                         jnp.concatenate([-sin, zero, rest], axis=1)])
    return jnp.tile(per_map, (1, 1, LANES // HEAD_DIM))


def kernel(x_prompt, x_sample, cache_k, cache_v, state_conv, state_ffn, page_table, c_prompt, c_sample, w_ada, b_ada, norm1_g, w_in, lambda_q1, lambda_k1, lambda_q2, lambda_k2, subln_g, w_attn_out, conv_w, w_conv_out, w_o, norm2_g, w_up, ffn_conv_w, ffn_conv_b, w_down, final_g):
    depth = w_ada.shape[0]
    bp, seq, d = x_prompt.shape
    bs, dec_seq, _ = x_sample.shape
    assert bp == 1 and dec_seq == 1, "one prompt sequence and one new token per sample sequence"
    assert depth == 1, "the final norm is fused into the ConvFFN-down kernel of the only layer"
    past_len = page_table.shape[1] * PAGE_SIZE
    qk_w = N_HEADS * QK_DIM
    cw = conv_w.shape[-1]
    f = w_down.shape[1]
    col_conv = 2 * qk_w + ATTN_W
    col_ga = col_conv + 3 * cw
    col_gc = col_ga + d
    tm_p = _pick(seq, (1024, 512, 256))
    tm_s = bs

    xp = x_prompt.reshape(seq, d)
    xs = x_sample.reshape(bs, d)
    pad = (-(bs + bp)) % BF16_ROWS
    c_rows = jnp.concatenate([c_sample, c_prompt, jnp.zeros((pad, d), F32)], axis=0)
    rope_p = _rope_tables(jnp.arange(seq, dtype=F32))
    rope_s = _rope_tables(jnp.full((bs,), past_len, dtype=F32))
    fin_g = final_g.reshape(1, d)
    keep = SUBLANES - (CONV_K - 1)

    outs = [[] for _ in range(8)]
    for li in range(depth):
        lam_init = _lambda_init(li)
        mod = _ada(c_rows, w_ada[li], b_ada[li])
        mod_s, mod_p = mod[:bs], mod[bs:bs + bp]
        lams = tuple(a[li].reshape(1, HEAD_DIM) for a in (lambda_q1, lambda_k1, lambda_q2, lambda_k2))
        g1, g2 = norm1_g[li].reshape(1, d), norm2_g[li].reshape(1, d)
        sub_g = subln_g[li].reshape(1, V_DIM)
        w_in_l, w_up_l = w_in[li], w_up[li]

        sc_prev, sf_prev = state_conv[li], state_ffn[li]
        h, q, k, v, wqkv_b = _qkv(xs, g1, mod_s, w_in_l, rope_s, tm_s)
        u, z, wcb_b, wcc_b, wcx_b = _sconv(
            h, [(w_in_l, col_conv), (w_in_l, col_conv + cw), (w_in_l, col_conv + 2 * cw)],
            conv_w[li], tm_s, state=(sc_prev[:, 0], sc_prev[:, 1]))
        per_head = lambda a: a.reshape(bs, N_HEADS, QK_DIM)
        o = _sample_attention(per_head(q), per_head(k), per_head(v), cache_k, cache_v,
                              li, page_table, lams, sub_g, lam_init).reshape(bs, ATTN_W)
        merged, wga_b, wgc_b, wao_b, wco_b = _merge(
            h, o, u, [(w_in_l, col_ga), (w_in_l, col_gc), (w_attn_out[li], 0), (w_conv_out[li], 0)], tm_s)
        x1, h2, wo_b = _resid_norm(merged, w_o[li], xs, mod_s, g2, tm_s)
        act, ug, uv, wup_b = _ffn_up(h2, w_up_l, ffn_conv_w[li], ffn_conv_b[li], tm_s,
                                     state=(sf_prev[:, 0], sf_prev[:, 1]))
        xs, wdn_b = _ffn_down(act, w_down[li], x1, mod_s, fin_g, tm_s)
        outs[4].append(k.reshape(bs, dec_seq, N_HEADS, QK_DIM))
        outs[5].append(v.reshape(bs, dec_seq, N_HEADS, V_DIM))
        outs[6].append(jnp.stack([sc_prev[:, 1], z], axis=1))
        outs[7].append(jnp.stack([sf_prev[:, 1], jnp.concatenate([ug, uv], axis=-1)], axis=1))

        h, q, k, v = _qkv(xp, g1, mod_p, wqkv_b, rope_p, tm_p)
        u, tail = _sconv(h, [(wcb_b, 0), (wcc_b, 0), (wcx_b, 0)], conv_w[li], tm_p)
        o = _prompt_attention(q, k, v, lams, sub_g, lam_init)
        (merged,) = _merge(h, o, u, [(wga_b, 0), (wgc_b, 0), (wao_b, 0), (wco_b, 0)], tm_p)
        x1, h2 = _resid_norm(merged, wo_b, xp, mod_p, g2, min(tm_p, 512))
        act, tg, tv = _ffn_up(h2, wup_b, ffn_conv_w[li], ffn_conv_b[li], tm_p)
        (xp,) = _ffn_down(act, wdn_b, x1, mod_p, fin_g, tm_p)
        outs[0].append(k.reshape(bp, seq, N_HEADS, QK_DIM))
        outs[1].append(v.reshape(bp, seq, N_HEADS, V_DIM))
        outs[2].append(tail[-1, keep:].reshape(bp, CONV_K - 1, cw))
        outs[3].append(jnp.concatenate([tg[-1, keep:], tv[-1, keep:]], axis=-1).reshape(bp, CONV_K - 1, 2 * f))

    return (xp.reshape(bp, seq, d), xs.reshape(bs, dec_seq, d),
            *(jnp.stack(o) for o in outs))
```

```python
import functools
import math

import jax
import jax.numpy as jnp
from jax import lax
from jax.experimental import pallas as pl
from jax.experimental.pallas import tpu as pltpu

F32 = jnp.float32
BF16 = jnp.bfloat16

N_HEADS = 8
HEAD_DIM = 64
QK_DIM = 2 * HEAD_DIM
V_DIM = 2 * HEAD_DIM
ROT_DIM = HEAD_DIM // 4
ROT_HALF = ROT_DIM // 2
ROPE_THETA = 500000.0
ATTN_W = N_HEADS * V_DIM
CONV_K = 3
NORM_EPS = 1e-6
SUBLN_EPS = 1e-5
ATTN_SCALE = HEAD_DIM ** -0.5
Q_SCALE = ATTN_SCALE * math.log2(math.e)
PAGE_SIZE = 128
N_MAPS = 2 * N_HEADS

LANES = 128
SUBLANES = 8
BF16_ROWS = 2 * SUBLANES
MXU_DIM = 256
VMEM_LIMIT_BYTES = 56 * 1024 * 1024
QKV_VMEM_LIMIT_BYTES = 60 * 1024 * 1024
ROW_SUB = MXU_DIM
CAST_ROWS = 256
NEG_INF = float("-inf")


def _lambda_init(layer):
    return 0.8 - 0.6 * math.exp(-0.3 * layer)


def _params(n_axes, vmem_limit=VMEM_LIMIT_BYTES):
    return pltpu.CompilerParams(dimension_semantics=("arbitrary",) * n_axes,
                                vmem_limit_bytes=vmem_limit)


def _sigmoid(x):
    return 1.0 / (1.0 + jnp.exp(-x))


def _bdot(a, b):
    return jnp.dot(a, b, preferred_element_type=F32)


def _pick(n, candidates):
    for c in candidates:
        if n % c == 0:
            return c
    return n


def _row_blocks(tm):
    return [slice(r, min(r + ROW_SUB, tm)) for r in range(0, tm, ROW_SUB)]


def _cast_weight(w_ref, wb_ref):
    k = w_ref.shape[0]
    for r in range(0, k, CAST_ROWS):
        rows = slice(r, min(r + CAST_ROWS, k))
        wb_ref[rows, :] = w_ref[rows, :].astype(BF16)


def _bf16_weights(w_refs, copy_refs):
    if not copy_refs:
        return w_refs
    for w_ref, wb_ref in zip(w_refs, copy_refs, strict=True):
        _cast_weight(w_ref, wb_ref)
    return copy_refs


def _weight_specs(weights, tn, width=None, k_tiled=False):
    in_specs, copy_specs, copy_shapes = [], [], []
    emit = weights[0][0].dtype == F32
    for w, col in weights:
        k, n = w.shape
        if k_tiled:
            in_specs.append(pl.BlockSpec((tn, n), lambda i, j: (j, 0)))
            copy = (pl.BlockSpec((tn, n), lambda i, j: (j, 0)), (k, n))
        else:
            in_specs.append(pl.BlockSpec((k, tn), lambda i, j, c=col // tn: (0, c + j)))
            copy = (pl.BlockSpec((k, tn), lambda i, j: (0, j)), (k, width))
        if emit:
            copy_specs.append(copy[0])
            copy_shapes.append(jax.ShapeDtypeStruct(copy[1], BF16))
    return in_specs, copy_specs, copy_shapes


def _mod_rows(ref, rows):
    return ref[...] if ref.shape[0] == 1 else ref[rows, :]


def _ada_kernel(c_ref, w_ref, b_ref, o_ref, wb_ref):
    c = c_ref[...]
    s = (c * _sigmoid(c)).astype(BF16)
    _cast_weight(w_ref, wb_ref)
    o_ref[...] = _bdot(s, wb_ref[...]) + b_ref[...]


def _ada(c_rows, w_ada, b_ada):
    m, d = c_rows.shape
    n = w_ada.shape[1]
    tn = _pick(n, (1024, 512, 256, 128))
    return pl.pallas_call(
        _ada_kernel,
        grid=(n // tn,),
        in_specs=[pl.BlockSpec((m, d), lambda j: (0, 0)),
                  pl.BlockSpec((d, tn), lambda j: (0, j)),
                  pl.BlockSpec((1, tn), lambda j: (0, j))],
        out_specs=pl.BlockSpec((m, tn), lambda j: (0, j)),
        out_shape=jax.ShapeDtypeStruct((m, n), F32),
        scratch_shapes=[pltpu.VMEM((d, tn), BF16)],
        compiler_params=_params(1),
        name="ada_mod",
    )(c_rows, w_ada, b_ada.reshape(1, n))


def _modulated_rmsnorm(x, g, sc, sh):
    ms = jnp.mean(x * x, axis=-1, keepdims=True)
    y = x * lax.rsqrt(ms + NORM_EPS)
    return y * g * (1.0 + sc) + sh


def _norm_prologue(x_ref, g_ref, sc_ref, sh_ref, h_ref):
    for rows in _row_blocks(x_ref.shape[0]):
        h = _modulated_rmsnorm(x_ref[rows, :], g_ref[...], _mod_rows(sc_ref, rows), _mod_rows(sh_ref, rows))
        h_ref[rows, :] = h.astype(BF16)


def _rope_tile(r, rope_ref, rows):
    cos, sa, sb = rope_ref[0, rows, :], rope_ref[1, rows, :], rope_ref[2, rows, :]
    outs = []
    for hh in range(r.shape[1] // LANES):
        xh = r[:, hh * LANES:(hh + 1) * LANES]
        outs.append(xh * cos + pltpu.roll(xh, ROT_HALF, 1) * sa
                    + pltpu.roll(xh, LANES - ROT_HALF, 1) * sb)
    return jnp.concatenate(outs, axis=1) if len(outs) > 1 else outs[0]


def _qkv_kernel(x_ref, g_ref, sc_ref, sh_ref, w_ref, rope_ref, h_ref, q_ref, k_ref, v_ref,
                *w_copy, nq):
    j = pl.program_id(1)

    @pl.when(j == 0)
    def _():
        _norm_prologue(x_ref, g_ref, sc_ref, sh_ref, h_ref)

    (wb_ref,) = _bf16_weights((w_ref,), w_copy)

    def run(epilogue):
        for rows in _row_blocks(h_ref.shape[0]):
            epilogue(rows, _bdot(h_ref[rows, :], wb_ref[...]))

    @pl.when(j < nq)
    def _():
        def epilogue(rows, res):
            q_ref[rows, :] = (_rope_tile(res, rope_ref, rows) * Q_SCALE).astype(BF16)
        run(epilogue)

    @pl.when((j >= nq) & (j < 2 * nq))
    def _():
        def epilogue(rows, res):
            k_ref[rows, :] = _rope_tile(res, rope_ref, rows)
        run(epilogue)

    @pl.when(j >= 2 * nq)
    def _():
        def epilogue(rows, res):
            v_ref[rows, :] = res
        run(epilogue)


def _qkv(x, g, mod, w, rope, tm):
    m, d = x.shape
    mb = mod.shape[0]
    width = N_HEADS * QK_DIM
    tn = 512 if w.dtype == F32 else width
    nq = width // tn
    w_in, w_out, w_shapes = _weight_specs([(w, 0)], tn, 3 * width)
    row = lambda i, j: (i, 0)
    out_col =lambda lo: (lambda i, j: (i, jnp.clip(j - lo, 0, nq - 1)))
    return pl.pallas_call(
        functools.partial(_qkv_kernel, nq=nq),
        grid=(m // tm, 3 * nq),
        in_specs=[pl.BlockSpec((tm, d), row),
                  pl.BlockSpec((1, d), lambda i, j: (0, 0)),
                  pl.BlockSpec((mb, d), lambda i, j: (0, 1)),
                  pl.BlockSpec((mb, d), lambda i, j: (0, 0)),
                  *w_in,
                  pl.BlockSpec((3, tm, LANES), lambda i, j: (0, i, 0))],
        out_specs=[pl.BlockSpec((tm, d), row),
                   pl.BlockSpec((tm, tn), out_col(0)),
                   pl.BlockSpec((tm, tn), out_col(nq)),
                   pl.BlockSpec((tm, tn), out_col(2 * nq)),
                   *w_out],
        out_shape=[jax.ShapeDtypeStruct((m, d), BF16),
                   jax.ShapeDtypeStruct((m, width), BF16),
                   jax.ShapeDtypeStruct((m, width), F32),
                   jax.ShapeDtypeStruct((m, width), F32),
                   *w_shapes],
        compiler_params=_params(2, QKV_VMEM_LIMIT_BYTES),
        name="qkv_proj",
    )(x, g, mod, mod, w, rope)


def _conv_rows(z, prev8, w_ref):
    n = z.shape[0]
    zz = jnp.concatenate([prev8, z], axis=0)
    z1 = zz[SUBLANES - 1:SUBLANES - 1 + n]
    z2 = zz[SUBLANES - 2:SUBLANES - 2 + n]
    return z2 * w_ref[0:1, :] + z1 * w_ref[1:2, :] + z * w_ref[2:3, :]


def _conv_state(p0, p1, z, w_ref):
    return p0 * w_ref[0:1, :] + p1 * w_ref[1:2, :] + z * w_ref[2:3, :]


def _last8(z):
    return z[z.shape[0] - SUBLANES:]


def _sconv_seq_kernel(h_ref, wb_ref, wc_ref, wx_ref, cw_ref, u_ref, tail_ref, carry_ref):
    i, j = pl.program_id(0), pl.program_id(1)

    @pl.when(i == 0)
    def _():
        carry_ref[j] = jnp.zeros(carry_ref.shape[1:], F32)

    prev = carry_ref[j]
    for rows in _row_blocks(h_ref.shape[0]):
        h = h_ref[rows, :]
        z = _bdot(h, wc_ref[...]) * _bdot(h, wx_ref[...])
        u_ref[rows, :] = (_bdot(h, wb_ref[...]) * _conv_rows(z, prev, cw_ref)).astype(BF16)
        prev = _last8(z)
    carry_ref[j] = prev
    tail_ref[0] = prev


def _sconv_state_kernel(h_ref, wb_ref, wc_ref, wx_ref, cw_ref, p0_ref, p1_ref, u_ref, z_ref, *w_copy):
    wbb_ref, wcb_ref, wxb_ref = _bf16_weights((wb_ref, wc_ref, wx_ref), w_copy)
    h = h_ref[...]
    z = _bdot(h, wcb_ref[...]) * _bdot(h, wxb_ref[...])
    conv = _conv_state(p0_ref[...], p1_ref[...], z, cw_ref)
    u_ref[...] = (_bdot(h, wbb_ref[...]) * conv).astype(BF16)
    z_ref[...] = z


def _sconv(h, weights, conv_w, tm, state=None):
    m, d = h.shape
    cw = conv_w.shape[1]
    tn = 512 if state is not None else _pick(cw, (1024, 512))
    nj = cw // tn
    w_in, w_out, w_shapes = _weight_specs(weights, tn, cw)
    tile = pl.BlockSpec((tm, tn), lambda i, j: (i, j))
    common = [pl.BlockSpec((tm, d), lambda i, j: (i, 0)), *w_in,
              pl.BlockSpec((CONV_K, tn), lambda i, j: (0, j))]
    operands = (h, *(w for w, _ in weights), conv_w)
    if state is None:
        return pl.pallas_call(
            _sconv_seq_kernel,
            grid=(m // tm, nj),
            in_specs=common,
            out_specs=[tile, pl.BlockSpec((1, SUBLANES, tn), lambda i, j: (i, 0, j))],
            out_shape=[jax.ShapeDtypeStruct((m, cw), BF16),
                       jax.ShapeDtypeStruct((m // tm, SUBLANES, cw), F32)],
            scratch_shapes=[pltpu.VMEM((nj, SUBLANES, tn), F32)],
            compiler_params=_params(2),
            name="sconv_seq",
        )(*operands)
    return pl.pallas_call(
        _sconv_state_kernel,
        grid=(m // tm, nj),
        in_specs=common + [tile, tile],
        out_specs=[tile, tile, *w_out],
        out_shape=[jax.ShapeDtypeStruct((m, cw), BF16), jax.ShapeDtypeStruct((m, cw), F32),
                   *w_shapes],
        compiler_params=_params(2),
        name="sconv_state",
    )(*operands, *state)


def _lambda_full(lq1_ref, lk1_ref, lq2_ref, lk2_ref, lam_init):
    a = jnp.sum(lq1_ref[...] * lk1_ref[...], axis=-1, keepdims=True)
    b = jnp.sum(lq2_ref[...] * lk2_ref[...], axis=-1, keepdims=True)
    return jnp.exp(a) - jnp.exp(b) + lam_init


def _attn_kernel(lq1_ref, lk1_ref, lq2_ref, lk2_ref, g_ref, q_ref, k_ref, v_ref, o_ref,
                 qst_ref, kb_ref, vt_ref, m_ref, acc_ref, s0_ref, s1_ref, *, bq, lam_init):
    qi = pl.program_id(1)
    t = k_ref.shape[0]
    bk = s0_ref.shape[0]

    @pl.when(qi == 0)
    def _():
        for c in range(t // bq):
            rows = slice(c * bq, (c + 1) * bq)
            kb_ref[rows, :] = k_ref[rows, :].astype(BF16)
            vt_ref[0:V_DIM, rows] = v_ref[rows, :].T.astype(BF16)
        vt_ref[V_DIM:, :] = jnp.ones((vt_ref.shape[0] - V_DIM, t), BF16)

    qt = q_ref[...].astype(F32).T
    dim = lax.broadcasted_iota(jnp.int32, qt.shape, 0)
    qst_ref[:, 0:bq] = jnp.where(dim < HEAD_DIM, qt, 0.0).astype(BF16)
    qst_ref[:, bq:2 * bq] = jnp.where(dim >= HEAD_DIM, qt, 0.0).astype(BF16)
    m_ref[...] = jnp.full(m_ref.shape, NEG_INF, F32)
    acc_ref[...] = jnp.zeros(acc_ref.shape, F32)

    def scores(j, s_ref):
        off = pl.multiple_of(j * bk, bk)
        s_ref[...] = _bdot(kb_ref[pl.ds(off, bk), :], qst_ref[...])

    def softmax_pv(j, s_ref, diag_key0=None):
        off = pl.multiple_of(j * bk, bk)
        vtc = vt_ref[:, pl.ds(off, bk)]
        q0 = diag_key0 or 0
        for cols in (slice(q0, bq), slice(bq + q0, 2 * bq)):
            st = s_ref[:, cols]
            if diag_key0 is not None:
                key = lax.broadcasted_iota(jnp.int32, st.shape, 0)
                qry = lax.broadcasted_iota(jnp.int32, st.shape, 1)
                st = jnp.where(key <= qry, st, NEG_INF)
            m_prev = m_ref[:, cols]
            m_new = jnp.maximum(m_prev, jnp.max(st, axis=0, keepdims=True))
            alpha = jnp.exp2(m_prev - m_new)
            pt = jnp.exp2(st - m_new)
            acc_ref[:, cols] = alpha * acc_ref[:, cols] + _bdot(vtc, pt.astype(BF16))
            m_ref[:, cols] = m_new

    def body(jj, carry):
        j = 2 * jj
        scores(j + 1, s1_ref)
        softmax_pv(j, s0_ref)
        scores(j + 2, s0_ref)
        softmax_pv(j + 1, s1_ref)
        return carry

    scores(0, s0_ref)
    lax.fori_loop(0, qi, body, 0)
    scores(2 * qi + 1, s1_ref)
    softmax_pv(2 * qi, s0_ref, diag_key0=0)
    softmax_pv(2 * qi + 1, s1_ref, diag_key0=bk)

    lam = _lambda_full(lq1_ref, lk1_ref, lq2_ref, lk2_ref, lam_init)
    ot = acc_ref[0:V_DIM, :] * (1.0 / acc_ref[V_DIM:V_DIM + 1, :])
    dlt = ot[:, 0:bq] - lam * ot[:, bq:2 * bq]
    ms = jnp.mean(dlt * dlt, axis=0, keepdims=True)
    y = (dlt * lax.rsqrt(ms + SUBLN_EPS)).T * g_ref[...] * (1.0 - lam_init)
    o_ref[...] = y.astype(BF16)


def _prompt_attention(q, k, v, lams, subln_g, lam_init):
    t = q.shape[0]
    bq = _pick(t, (512, 256))
    vec = lambda n: pl.BlockSpec((1, n), lambda h, i: (0, 0))
    return pl.pallas_call(
        functools.partial(_attn_kernel, bq=bq, lam_init=lam_init),
        grid=(N_HEADS, t // bq),
        in_specs=[vec(HEAD_DIM)] * 4 + [vec(V_DIM),
                  pl.BlockSpec((bq, QK_DIM), lambda h, i: (i, h)),
                  pl.BlockSpec((t, QK_DIM), lambda h, i: (0, h)),
                  pl.BlockSpec((t, V_DIM), lambda h, i: (0, h))],
        out_specs=pl.BlockSpec((bq, V_DIM), lambda h, i: (i, h)),
        out_shape=jax.ShapeDtypeStruct((t, ATTN_W), BF16),
        scratch_shapes=[pltpu.VMEM((QK_DIM, 2 * bq), BF16),
                        pltpu.VMEM((t, QK_DIM), BF16),
                        pltpu.VMEM((V_DIM + BF16_ROWS, t), BF16),
                        pltpu.VMEM((1, 2 * bq), F32),
                        pltpu.VMEM((V_DIM + BF16_ROWS, 2 * bq), F32),
                        pltpu.VMEM((bq // 2, 2 * bq), F32),
                        pltpu.VMEM((bq // 2, 2 * bq), F32)],
        compiler_params=_params(2),
        name="prompt_attn",
    )(*lams, subln_g, q, k, v)


def _sattn_kernel(pt_ref, lq1_ref, lk1_ref, lq2_ref, lk2_ref, g_ref, q_ref, kn_ref, vn_ref, *rest,
                  pg, lam_init):
    k_refs, v_refs = rest[:pg], rest[pg:2 * pg]
    o_ref, qm_ref, m_ref, l_ref, acc_ref = rest[2 * pg:]
    step = pl.program_id(1)
    page_rows = PAGE_SIZE * N_HEADS
    lane = lax.broadcasted_iota(jnp.int32, (N_HEADS, QK_DIM), 1)

    @pl.when(step == 0)
    def _():
        q = q_ref[0].astype(F32)
        qm_ref[0:N_HEADS] = jnp.where(lane < HEAD_DIM, q, 0.0)
        qm_ref[N_HEADS:N_MAPS] = jnp.where(lane >= HEAD_DIM, q, 0.0)
        m_ref[...] = jnp.full(m_ref.shape, NEG_INF, F32)
        l_ref[...] = jnp.zeros(l_ref.shape, F32)
        acc_ref[...] = jnp.zeros(acc_ref.shape, F32)

    qm = qm_ref[...]
    qmb = qm.astype(BF16)
    s = jnp.concatenate(
        [lax.dot_general(qmb, k_refs[p][0, 0].reshape(page_rows, QK_DIM).astype(BF16),
                         (((1,), (1,)), ((), ())), preferred_element_type=F32)
         for p in range(pg)], axis=1)
    row = lax.broadcasted_iota(jnp.int32, s.shape, 0)
    col = lax.broadcasted_iota(jnp.int32, s.shape, 1)
    s = jnp.where(col % N_HEADS == row % N_HEADS, s, NEG_INF)
    m_prev = m_ref[...]
    m_new = jnp.maximum(m_prev, jnp.max(s, axis=-1, keepdims=True))
    alpha = jnp.exp2(m_prev - m_new)
    p_all = jnp.exp2(s - m_new)
    l_ref[...] = alpha * l_ref[...] + jnp.sum(p_all, axis=-1, keepdims=True)
    pb = p_all.astype(BF16)
    pv = None
    for p in range(pg):
        part = _bdot(pb[:, p * page_rows:(p + 1) * page_rows],
                     v_refs[p][0, 0].reshape(page_rows, V_DIM).astype(BF16))
        pv = part if pv is None else pv + part
    acc_ref[...] = alpha * acc_ref[...] + pv
    m_ref[...] = m_new

    @pl.when(step == pl.num_programs(1) - 1)
    def _():
        kn = kn_ref[0]
        vn = vn_ref[0]
        kn2 = jnp.concatenate([kn, kn], axis=0)
        vn2 = jnp.concatenate([vn, vn], axis=0)
        s_new = jnp.sum(qm * kn2, axis=-1, keepdims=True)
        m_old = m_ref[...]
        m_fin = jnp.maximum(m_old, s_new)
        a = jnp.exp2(m_old - m_fin)
        p_new = jnp.exp2(s_new - m_fin)
        l_fin = a * l_ref[...] + p_new
        o = (a * acc_ref[...] + p_new * vn2) / l_fin
        lam = _lambda_full(lq1_ref, lk1_ref, lq2_ref, lk2_ref, lam_init)
        dlt = o[0:N_HEADS] - lam * o[N_HEADS:N_MAPS]
        ms = jnp.mean(dlt * dlt, axis=-1, keepdims=True)
        o_ref[0] = dlt * lax.rsqrt(ms + SUBLN_EPS) * g_ref[...] * (1.0 - lam_init)


def _sample_attention(q, k_new, v_new, cache_k, cache_v, layer, page_table, lams, subln_g, lam_init):
    b, n_pages = page_table.shape
    pg = _pick(n_pages, (16, 8, 4, 2, 1))
    vec = lambda n: pl.BlockSpec((1, n), lambda i, s, pt: (0, 0))
    rowspec = pl.BlockSpec((1, N_HEADS, QK_DIM), lambda i, s, pt: (i, 0, 0))
    page = lambda p: pl.BlockSpec((1, 1, PAGE_SIZE, N_HEADS, QK_DIM),
                                  lambda i, s, pt: (layer, pt[i, s * pg + p], 0, 0, 0))
    grid_spec = pltpu.PrefetchScalarGridSpec(
        num_scalar_prefetch=1,
        grid=(b, n_pages // pg),
        in_specs=[vec(HEAD_DIM)] * 4 + [vec(V_DIM), rowspec, rowspec, rowspec]
                 + [page(p) for p in range(pg)] + [page(p) for p in range(pg)],
        out_specs=rowspec,
        scratch_shapes=[pltpu.VMEM((N_MAPS, QK_DIM), F32),
                        pltpu.VMEM((N_MAPS, 1), F32),
                        pltpu.VMEM((N_MAPS, 1), F32),
                        pltpu.VMEM((N_MAPS, V_DIM), F32)],
    )
    return pl.pallas_call(
        functools.partial(_sattn_kernel, pg=pg, lam_init=lam_init),
        grid_spec=grid_spec,
        out_shape=jax.ShapeDtypeStruct((b, N_HEADS, V_DIM), F32),
        compiler_params=_params(2),
        name="sample_attn",
    )(page_table, *lams, subln_g, q, k_new, v_new, *([cache_k] * pg), *([cache_v] * pg))


def _merge_kernel(h_ref, o_ref, u_ref, wga_ref, wgc_ref, wao_ref, wco_ref, out_ref, *w_copy):
    wga, wgc, wao, wco = _bf16_weights((wga_ref, wgc_ref, wao_ref, wco_ref), w_copy)
    for rows in _row_blocks(h_ref.shape[0]):
        h = h_ref[rows, :]
        att = _sigmoid(_bdot(h, wga[...])) * _bdot(o_ref[rows, :].astype(BF16), wao[...])
        conv = _sigmoid(_bdot(h, wgc[...])) * _bdot(u_ref[rows, :], wco[...])
        out_ref[rows, :] = (att + conv).astype(BF16)


def _merge(h, o, u, weights, tm):
    m, d = h.shape
    tn = 256 if weights[0][0].dtype == F32 else _pick(d, (1024, 512))
    w_in, w_out, w_shapes = _weight_specs(weights, tn, d)
    row = lambda w: pl.BlockSpec((tm, w), lambda i, j: (i, 0))
    return pl.pallas_call(
        _merge_kernel,
        grid=(m // tm, d // tn),
        in_specs=[row(d), row(o.shape[1]), row(u.shape[1]), *w_in],
        out_specs=[pl.BlockSpec((tm, tn), lambda i, j: (i, j)), *w_out],
        out_shape=[jax.ShapeDtypeStruct((m, d), BF16), *w_shapes],
        compiler_params=_params(2),
        name="merge",
    )(h, o, u, *(w for w, _ in weights))


def _resid_norm_kernel(a_ref, w_ref, x_ref, gate_ref, g_ref, sc_ref, sh_ref, x1_ref, h_ref, *w_copy):
    (wb_ref,) = _bf16_weights((w_ref,), w_copy)
    for rows in _row_blocks(a_ref.shape[0]):
        x1 = x_ref[rows, :] + _mod_rows(gate_ref, rows) * _bdot(a_ref[rows, :], wb_ref[...])
        x1_ref[rows, :] = x1
        h = _modulated_rmsnorm(x1, g_ref[...], _mod_rows(sc_ref, rows), _mod_rows(sh_ref, rows))
        h_ref[rows, :] = h.astype(BF16)


def _resid_norm(a, w, x, mod, g, tm):
    m, d = x.shape
    k = a.shape[1]
    mb = mod.shape[0]
    w_in, w_out, w_shapes = _weight_specs([(w, 0)], d, d)
    row = lambda width: pl.BlockSpec((tm, width), lambda i, j: (i, 0))
    modcol = lambda c: pl.BlockSpec((mb, d), lambda i, j: (0, c))
    return pl.pallas_call(
        _resid_norm_kernel,
        grid=(m // tm, 1),
        in_specs=[row(k), *w_in, row(d), modcol(2),
                  pl.BlockSpec((1, d), lambda i, j: (0, 0)), modcol(4), modcol(3)],
        out_specs=[row(d), row(d), *w_out],
        out_shape=[jax.ShapeDtypeStruct((m, d), F32), jax.ShapeDtypeStruct((m, d), BF16),
                   *w_shapes],
        compiler_params=_params(2),
        name="resid_norm",
    )(a, w, x, mod, g, mod, mod)


def _ffn_up_seq_kernel(h_ref, w_ref, cwg_ref, cwv_ref, bg_ref, bv_ref,
                       act_ref, tg_ref, tv_ref, carry_ref):
    i, j = pl.program_id(0), pl.program_id(1)
    tn = act_ref.shape[1]

    @pl.when(i == 0)
    def _():
        carry_ref[j] = jnp.zeros(carry_ref.shape[1:], F32)

    prev_g, prev_v = carry_ref[j, 0], carry_ref[j, 1]
    for rows in _row_blocks(h_ref.shape[0]):
        u = _bdot(h_ref[rows, :], w_ref[...])
        ug, uv = u[:, 0:tn], u[:, tn:2 * tn]
        gate = _conv_rows(ug, prev_g, cwg_ref) + bg_ref[...]
        val = _conv_rows(uv, prev_v, cwv_ref) + bv_ref[...]
        act_ref[rows, :] = (gate * _sigmoid(gate) * val).astype(BF16)
        prev_g, prev_v = _last8(ug), _last8(uv)
    carry_ref[j, 0] = prev_g
    carry_ref[j, 1] = prev_v
    tg_ref[0] = prev_g
    tv_ref[0] = prev_v


def _ffn_up_state_kernel(h_ref, wg_ref, wv_ref, cwg_ref, cwv_ref, bg_ref, bv_ref,
                         pg0_ref, pg1_ref, pv0_ref, pv1_ref,
                         act_ref, ug_ref, uv_ref, wcat_ref):
    tn = act_ref.shape[1]
    _cast_weight(wg_ref, wcat_ref.at[:, 0:tn])
    _cast_weight(wv_ref, wcat_ref.at[:, tn:2 * tn])
    u = _bdot(h_ref[...], wcat_ref[...])
    ug, uv = u[:, 0:tn], u[:, tn:2 * tn]
    gate = _conv_state(pg0_ref[...], pg1_ref[...], ug, cwg_ref) + bg_ref[...]
    val = _conv_state(pv0_ref[...], pv1_ref[...], uv, cwv_ref) + bv_ref[...]
    act_ref[...] = (gate * _sigmoid(gate) * val).astype(BF16)
    ug_ref[...] = ug
    uv_ref[...] = uv


def _ffn_up(h, w, conv_w, conv_b, tm, state=None):
    m, d = h.shape
    f = conv_w.shape[1] // 2
    tn = _pick(f, (1408, 512, 256))
    nj = f // tn
    lo = lambda i, j: (0, j)
    hi = lambda i, j: (0, nj + j)
    tile = pl.BlockSpec((tm, tn), lambda i, j: (i, j))
    tile_hi = pl.BlockSpec((tm, tn), lambda i, j: (i, nj + j))
    wcat = pl.BlockSpec((d, 2 * tn), lo)
    conv_b = conv_b.reshape(1, 2 * f)
    small = [pl.BlockSpec((CONV_K, tn), lo), pl.BlockSpec((CONV_K, tn), hi),
             pl.BlockSpec((1, tn), lo), pl.BlockSpec((1, tn), hi)]
    act = pl.BlockSpec((tm, d), lambda i, j: (i, 0))
    if state is None:
        tail = pl.BlockSpec((1, SUBLANES, tn), lambda i, j: (i, 0, j))
        return pl.pallas_call(
            _ffn_up_seq_kernel,
            grid=(m // tm, nj),
            in_specs=[act, wcat, *small],
            out_specs=[tile, tail, tail],
            out_shape=[jax.ShapeDtypeStruct((m, f), BF16),
                       jax.ShapeDtypeStruct((m // tm, SUBLANES, f), F32),
                       jax.ShapeDtypeStruct((m // tm, SUBLANES, f), F32)],
            scratch_shapes=[pltpu.VMEM((nj, 2, SUBLANES, tn), F32)],
            compiler_params=_params(2),
            name="ffn_up_seq",
        )(h, w, conv_w, conv_w, conv_b, conv_b)
    p0, p1 = state
    once = lambda index_map: pl.BlockSpec((d, tn), index_map, pipeline_mode=pl.Buffered(1))
    return pl.pallas_call(
        _ffn_up_state_kernel,
        grid=(m // tm, nj),
        in_specs=[act, once(lo), once(hi), *small, tile, tile, tile_hi, tile_hi],
        out_specs=[tile, tile, tile, wcat],
        out_shape=[jax.ShapeDtypeStruct((m, f), BF16),
                   jax.ShapeDtypeStruct((m, f), F32),
                   jax.ShapeDtypeStruct((m, f), F32),
                   jax.ShapeDtypeStruct((d, 2 * f), BF16)],
        compiler_params=_params(2),
        name="ffn_up_state",
    )(h, w, w, conv_w, conv_w, conv_b, conv_b, p0, p1, p0, p1)


def _ffn_down_kernel(a_ref, w_ref, x_ref, g2_ref, fg_ref, y_ref, *w_copy):
    kk = pl.program_id(1)
    last = pl.num_programs(1) - 1
    (wb_ref,) = _bf16_weights((w_ref,), w_copy)
    blocks = _row_blocks(a_ref.shape[0])

    @pl.when(kk == 0)
    def _():
        for rows in blocks:
            y_ref[rows, :] = _bdot(a_ref[rows, :], wb_ref[...])

    @pl.when((kk > 0) & (kk < last))
    def _():
        for rows in blocks:
            y_ref[rows, :] += _bdot(a_ref[rows, :], wb_ref[...])

    @pl.when(kk == last)
    def _():
        for rows in blocks:
            acc = y_ref[rows, :] + _bdot(a_ref[rows, :], wb_ref[...])
            x = x_ref[rows, :] + _mod_rows(g2_ref, rows) * acc
            ms = jnp.mean(x * x, axis=-1, keepdims=True)
            y_ref[rows, :] = x * lax.rsqrt(ms + NORM_EPS) * fg_ref[...]


def _ffn_down(act, w_down, x, mod, final_g, tm):
    m, d = x.shape
    f = act.shape[1]
    mb = mod.shape[0]
    tk = _pick(f, (512, 256)) if w_down.dtype == F32 else _pick(f, (1408, 512, 256))
    assert f // tk >= 2, "the accumulate-in-output schedule needs at least two K steps"
    w_in, w_out, w_shapes = _weight_specs([(w_down, 0)], tk, k_tiled=True)
    return pl.pallas_call(
        _ffn_down_kernel,
        grid=(m // tm, f // tk),
        in_specs=[pl.BlockSpec((tm, tk), lambda i, k: (i, k)),
                  *w_in,
                  pl.BlockSpec((tm, d), lambda i, k: (i, 0)),
                  pl.BlockSpec((mb, d), lambda i, k: (0, 5)),
                  pl.BlockSpec((1, d), lambda i, k: (0, 0))],
        out_specs=[pl.BlockSpec((tm, d), lambda i, k: (i, 0)), *w_out],
        out_shape=[jax.ShapeDtypeStruct((m, d), F32), *w_shapes],
        compiler_params=_params(2),
        name="ffn_down",
    )(act, w_down, x, mod, final_g)


def _rope_tables(pos):
    inv_freq = ROPE_THETA ** (-jnp.arange(0, ROT_DIM, 2, dtype=F32) / ROT_DIM)
    ang = pos[:, None] * inv_freq[None, :]
    cos, sin = jnp.cos(ang), jnp.sin(ang)
    rest = jnp.zeros((pos.shape[0], HEAD_DIM - ROT_DIM), F32)
    zero = jnp.zeros_like(sin)
    per_map = jnp.stack([jnp.concatenate([cos, cos, rest + 1.0], axis=1),
                         jnp.concatenate([zero, sin, rest], axis=1),
                         jnp.concatenate([-sin, zero, rest], axis=1)])
    return jnp.tile(per_map, (1, 1, LANES // HEAD_DIM))


def kernel(x_prompt, x_sample, cache_k, cache_v, state_conv, state_ffn, page_table, c_prompt, c_sample, w_ada, b_ada, norm1_g, w_in, lambda_q1, lambda_k1, lambda_q2, lambda_k2, subln_g, w_attn_out, conv_w, w_conv_out, w_o, norm2_g, w_up, ffn_conv_w, ffn_conv_b, w_down, final_g):
    depth = w_ada.shape[0]
    bp, seq, d = x_prompt.shape
    bs, dec_seq, _ = x_sample.shape
    assert bp == 1 and dec_seq == 1, "one prompt sequence and one new token per sample sequence"
    assert depth == 1, "the final norm is fused into the ConvFFN-down kernel of the only layer"
    past_len = page_table.shape[1] * PAGE_SIZE
    qk_w = N_HEADS * QK_DIM
    cw = conv_w.shape[-1]
    f = w_down.shape[1]
    col_conv = 2 * qk_w + ATTN_W
    col_ga = col_conv + 3 * cw
    col_gc = col_ga + d
    tm_p = _pick(seq, (1024, 512, 256))
    tm_s = bs

    xp = x_prompt.reshape(seq, d)
    xs = x_sample.reshape(bs, d)
    pad = (-(bs + bp)) % BF16_ROWS
    c_rows = jnp.concatenate([c_sample, c_prompt, jnp.zeros((pad, d), F32)], axis=0)
    rope_p = _rope_tables(jnp.arange(seq, dtype=F32))
    rope_s = _rope_tables(jnp.full((bs,), past_len, dtype=F32))
    fin_g = final_g.reshape(1, d)
    keep = SUBLANES - (CONV_K - 1)

    outs = [[] for _ in range(8)]
    for li in range(depth):
        lam_init = _lambda_init(li)
        mod = _ada(c_rows, w_ada[li], b_ada[li])
        mod_s, mod_p = mod[:bs], mod[bs:bs + bp]
        lams = tuple(a[li].reshape(1, HEAD_DIM) for a in (lambda_q1, lambda_k1, lambda_q2, lambda_k2))
        g1, g2 = norm1_g[li].reshape(1, d), norm2_g[li].reshape(1, d)
        sub_g = subln_g[li].reshape(1, V_DIM)
        w_in_l, w_up_l = w_in[li], w_up[li]

        sc_prev, sf_prev = state_conv[li], state_ffn[li]
        h, q, k, v, wqkv_b = _qkv(xs, g1, mod_s, w_in_l, rope_s, tm_s)
        u, z, wcb_b, wcc_b, wcx_b = _sconv(
            h, [(w_in_l, col_conv), (w_in_l, col_conv + cw), (w_in_l, col_conv + 2 * cw)],
            conv_w[li], tm_s, state=(sc_prev[:, 0], sc_prev[:, 1]))
        per_head = lambda a: a.reshape(bs, N_HEADS, QK_DIM)
        o = _sample_attention(per_head(q), per_head(k), per_head(v), cache_k, cache_v,
                              li, page_table, lams, sub_g, lam_init).reshape(bs, ATTN_W)
        merged, wga_b, wgc_b, wao_b, wco_b = _merge(
            h, o, u, [(w_in_l, col_ga), (w_in_l, col_gc), (w_attn_out[li], 0), (w_conv_out[li], 0)], tm_s)
        x1, h2, wo_b = _resid_norm(merged, w_o[li], xs, mod_s, g2, tm_s)
        act, ug, uv, wup_b = _ffn_up(h2, w_up_l, ffn_conv_w[li], ffn_conv_b[li], tm_s,
                                     state=(sf_prev[:, 0], sf_prev[:, 1]))
        xs, wdn_b = _ffn_down(act, w_down[li], x1, mod_s, fin_g, tm_s)
        outs[4].append(k.reshape(bs, dec_seq, N_HEADS, QK_DIM))
        outs[5].append(v.reshape(bs, dec_seq, N_HEADS, V_DIM))
        outs[6].append(jnp.stack([sc_prev[:, 1], z], axis=1))
        outs[7].append(jnp.stack([sf_prev[:, 1], jnp.concatenate([ug, uv], axis=-1)], axis=1))

        h, q, k, v = _qkv(xp, g1, mod_p, wqkv_b, rope_p, tm_p)
        u, tail = _sconv(h, [(wcb_b, 0), (wcc_b, 0), (wcx_b, 0)], conv_w[li], tm_p)
        o = _prompt_attention(q, k, v, lams, sub_g, lam_init)
        (merged,) = _merge(h, o, u, [(wga_b, 0), (wgc_b, 0), (wao_b, 0), (wco_b, 0)], tm_p)
        x1, h2 = _resid_norm(merged, wo_b, xp, mod_p, g2, min(tm_p, 512))
        act, tg, tv = _ffn_up(h2, wup_b, ffn_conv_w[li], ffn_conv_b[li], tm_p)
        (xp,) = _ffn_down(act, wdn_b, x1, mod_p, fin_g, tm_p)
        outs[0].append(k.reshape(bp, seq, N_HEADS, QK_DIM))
        outs[1].append(v.reshape(bp, seq, N_HEADS, V_DIM))
        outs[2].append(tail[-1, keep:].reshape(bp, CONV_K - 1, cw))
        outs[3].append(jnp.concatenate([tg[-1, keep:], tv[-1, keep:]], axis=-1).reshape(bp, CONV_K - 1, 2 * f))

    return (xp.reshape(bp, seq, d), xs.reshape(bs, dec_seq, d),
            *(jnp.stack(o) for o in outs))
```

```python
import functools
import math

import jax
import jax.numpy as jnp
from jax import lax
from jax.experimental import pallas as pl
from jax.experimental.pallas import tpu as pltpu

F32 = jnp.float32
BF16 = jnp.bfloat16

N_HEADS = 8
HEAD_DIM = 64
QK_DIM = 2 * HEAD_DIM
V_DIM = 2 * HEAD_DIM
ROT_DIM = HEAD_DIM // 4
ROT_HALF = ROT_DIM // 2
ROPE_THETA = 500000.0
ATTN_W = N_HEADS * V_DIM
CONV_K = 3
NORM_EPS = 1e-6
SUBLN_EPS = 1e-5
ATTN_SCALE = HEAD_DIM ** -0.5
Q_SCALE = ATTN_SCALE * math.log2(math.e)
PAGE_SIZE = 128
N_MAPS = 2 * N_HEADS

LANES = 128
SUBLANES = 8
BF16_ROWS = 2 * SUBLANES
MXU_DIM = 256
VMEM_LIMIT_BYTES = 56 * 1024 * 1024
QKV_VMEM_LIMIT_BYTES = 60 * 1024 * 1024
ROW_SUB = MXU_DIM
CAST_ROWS = 256
NEG_INF = float("-inf")


def _lambda_init(layer):
    return 0.8 - 0.6 * math.exp(-0.3 * layer)


def _params(n_axes, vmem_limit=VMEM_LIMIT_BYTES):
    return pltpu.CompilerParams(dimension_semantics=("arbitrary",) * n_axes,
                                vmem_limit_bytes=vmem_limit)


def _sigmoid(x):
    return 1.0 / (1.0 + jnp.exp(-x))


def _bdot(a, b):
    return jnp.dot(a, b, preferred_element_type=F32)


def _pick(n, candidates):
    for c in candidates:
        if n % c == 0:
            return c
    return n


def _row_blocks(tm):
    return [slice(r, min(r + ROW_SUB, tm)) for r in range(0, tm, ROW_SUB)]


def _cast_weight(w_ref, wb_ref):
    k = w_ref.shape[0]
    for r in range(0, k, CAST_ROWS):
        rows = slice(r, min(r + CAST_ROWS, k))
        wb_ref[rows, :] = w_ref[rows, :].astype(BF16)


def _bf16_weights(w_refs, copy_refs):
    if not copy_refs:
        return w_refs
    for w_ref, wb_ref in zip(w_refs, copy_refs, strict=True):
        _cast_weight(w_ref, wb_ref)
    return copy_refs


def _weight_specs(weights, tn, width=None, k_tiled=False):
    in_specs, copy_specs, copy_shapes = [], [], []
    emit = weights[0][0].dtype == F32
    for w, col in weights:
        k, n = w.shape
        if k_tiled:
            in_specs.append(pl.BlockSpec((tn, n), lambda i, j: (j, 0)))
            copy = (pl.BlockSpec((tn, n), lambda i, j: (j, 0)), (k, n))
        else:
            in_specs.append(pl.BlockSpec((k, tn), lambda i, j, c=col // tn: (0, c + j)))
            copy = (pl.BlockSpec((k, tn), lambda i, j: (0, j)), (k, width))
        if emit:
            copy_specs.append(copy[0])
            copy_shapes.append(jax.ShapeDtypeStruct(copy[1], BF16))
    return in_specs, copy_specs, copy_shapes


def _mod_rows(ref, rows):
    return ref[...] if ref.shape[0] == 1 else ref[rows, :]


def _ada_kernel(c_ref, w_ref, b_ref, o_ref, wb_ref):
    c = c_ref[...]
    s = (c * _sigmoid(c)).astype(BF16)
    _cast_weight(w_ref, wb_ref)
    o_ref[...] = _bdot(s, wb_ref[...]) + b_ref[...]


def _ada(c_rows, w_ada, b_ada):
    m, d = c_rows.shape
    n = w_ada.shape[1]
    tn = _pick(n, (1024, 512, 256, 128))
    return pl.pallas_call(
        _ada_kernel,
        grid=(n // tn,),
        in_specs=[pl.BlockSpec((m, d), lambda j: (0, 0)),
                  pl.BlockSpec((d, tn), lambda j: (0, j)),
                  pl.BlockSpec((1, tn), lambda j: (0, j))],
        out_specs=pl.BlockSpec((m, tn), lambda j: (0, j)),
        out_shape=jax.ShapeDtypeStruct((m, n), F32),
        scratch_shapes=[pltpu.VMEM((d, tn), BF16)],
        compiler_params=_params(1),
        name="ada_mod",
    )(c_rows, w_ada, b_ada.reshape(1, n))


def _modulated_rmsnorm(x, g, sc, sh):
    ms = jnp.mean(x * x, axis=-1, keepdims=True)
    y = x * lax.rsqrt(ms + NORM_EPS)
    return y * g * (1.0 + sc) + sh


def _norm_prologue(x_ref, g_ref, sc_ref, sh_ref, h_ref):
    for rows in _row_blocks(x_ref.shape[0]):
        h = _modulated_rmsnorm(x_ref[rows, :], g_ref[...], _mod_rows(sc_ref, rows), _mod_rows(sh_ref, rows))
        h_ref[rows, :] = h.astype(BF16)


def _rope_tile(r, rope_ref, rows):
    cos, sa, sb = rope_ref[0, rows, :], rope_ref[1, rows, :], rope_ref[2, rows, :]
    outs = []
    for hh in range(r.shape[1] // LANES):
        xh = r[:, hh * LANES:(hh + 1) * LANES]
        outs.append(xh * cos + pltpu.roll(xh, ROT_HALF, 1) * sa
                    + pltpu.roll(xh, LANES - ROT_HALF, 1) * sb)
    return jnp.concatenate(outs, axis=1) if len(outs) > 1 else outs[0]


def _qkv_kernel(x_ref, g_ref, sc_ref, sh_ref, w_ref, rope_ref, h_ref, q_ref, k_ref, v_ref,
                *w_copy, nq, q_transposed):
    j = pl.program_id(1)

    @pl.when(j == 0)
    def _():
        _norm_prologue(x_ref, g_ref, sc_ref, sh_ref, h_ref)

    (wb_ref,) = _bf16_weights((w_ref,), w_copy)

    def run(epilogue):
        for rows in _row_blocks(h_ref.shape[0]):
            epilogue(rows, _bdot(h_ref[rows, :], wb_ref[...]))

    @pl.when(j < nq)
    def _():
        def epilogue(rows, res):
            q = _rope_tile(res, rope_ref, rows) * Q_SCALE
            if q_transposed:
                q_ref[:, rows] = q.T.astype(BF16)
            else:
                q_ref[rows, :] = q.astype(BF16)
        run(epilogue)

    @pl.when((j >= nq) & (j < 2 * nq))
    def _():
        def epilogue(rows, res):
            k_ref[rows, :] = _rope_tile(res, rope_ref, rows)
        run(epilogue)

    @pl.when(j >= 2 * nq)
    def _():
        def epilogue(rows, res):
            v_ref[rows, :] = res
        run(epilogue)


def _qkv(x, g, mod, w, rope, tm):
    m, d = x.shape
    mb = mod.shape[0]
    width = N_HEADS * QK_DIM
    q_transposed = w.dtype == BF16
    tn = width if q_transposed else 512
    nq = width // tn
    q_spec = (pl.BlockSpec((width, tm), lambda i, j: (0, i)) if q_transposed
              else pl.BlockSpec((tm, tn), lambda i, j: (i, jnp.clip(j, 0, nq - 1))))
    q_shape = (width, m) if q_transposed else (m, width)
    w_in, w_out, w_shapes = _weight_specs([(w, 0)], tn, 3 * width)
    row = lambda i, j: (i, 0)
    out_col =lambda lo: (lambda i, j: (i, jnp.clip(j - lo, 0, nq - 1)))
    return pl.pallas_call(
        functools.partial(_qkv_kernel, nq=nq, q_transposed=q_transposed),
        grid=(m // tm, 3 * nq),
        in_specs=[pl.BlockSpec((tm, d), row),
                  pl.BlockSpec((1, d), lambda i, j: (0, 0)),
                  pl.BlockSpec((mb, d), lambda i, j: (0, 1)),
                  pl.BlockSpec((mb, d), lambda i, j: (0, 0)),
                  *w_in,
                  pl.BlockSpec((3, tm, LANES), lambda i, j: (0, i, 0))],
        out_specs=[pl.BlockSpec((tm, d), row),
                   q_spec,
                   pl.BlockSpec((tm, tn), out_col(nq)),
                   pl.BlockSpec((tm, tn), out_col(2 * nq)),
                   *w_out],
        out_shape=[jax.ShapeDtypeStruct((m, d), BF16),
                   jax.ShapeDtypeStruct(q_shape, BF16),
                   jax.ShapeDtypeStruct((m, width), F32),
                   jax.ShapeDtypeStruct((m, width), F32),
                   *w_shapes],
        compiler_params=_params(2, QKV_VMEM_LIMIT_BYTES),
        name="qkv_proj",
    )(x, g, mod, mod, w, rope)


def _conv_rows(z, prev8, w_ref):
    n = z.shape[0]
    zz = jnp.concatenate([prev8, z], axis=0)
    z1 = zz[SUBLANES - 1:SUBLANES - 1 + n]
    z2 = zz[SUBLANES - 2:SUBLANES - 2 + n]
    return z2 * w_ref[0:1, :] + z1 * w_ref[1:2, :] + z * w_ref[2:3, :]


def _conv_state(p0, p1, z, w_ref):
    return p0 * w_ref[0:1, :] + p1 * w_ref[1:2, :] + z * w_ref[2:3, :]


def _last8(z):
    return z[z.shape[0] - SUBLANES:]


def _sconv_seq_kernel(h_ref, wb_ref, wc_ref, wx_ref, cw_ref, u_ref, tail_ref, carry_ref):
    i, j = pl.program_id(0), pl.program_id(1)

    @pl.when(i == 0)
    def _():
        carry_ref[j] = jnp.zeros(carry_ref.shape[1:], F32)

    prev = carry_ref[j]
    for rows in _row_blocks(h_ref.shape[0]):
        h = h_ref[rows, :]
        z = _bdot(h, wc_ref[...]) * _bdot(h, wx_ref[...])
        u_ref[rows, :] = (_bdot(h, wb_ref[...]) * _conv_rows(z, prev, cw_ref)).astype(BF16)
        prev = _last8(z)
    carry_ref[j] = prev
    tail_ref[0] = prev


def _sconv_state_kernel(h_ref, wb_ref, wc_ref, wx_ref, cw_ref, p0_ref, p1_ref, u_ref, z_ref, *w_copy):
    wbb_ref, wcb_ref, wxb_ref = _bf16_weights((wb_ref, wc_ref, wx_ref), w_copy)
    h = h_ref[...]
    z = _bdot(h, wcb_ref[...]) * _bdot(h, wxb_ref[...])
    conv = _conv_state(p0_ref[...], p1_ref[...], z, cw_ref)
    u_ref[...] = (_bdot(h, wbb_ref[...]) * conv).astype(BF16)
    z_ref[...] = z


def _sconv(h, weights, conv_w, tm, state=None):
    m, d = h.shape
    cw = conv_w.shape[1]
    tn = 512 if state is not None else _pick(cw, (1024, 512))
    nj = cw // tn
    w_in, w_out, w_shapes = _weight_specs(weights, tn, cw)
    tile = pl.BlockSpec((tm, tn), lambda i, j: (i, j))
    common = [pl.BlockSpec((tm, d), lambda i, j: (i, 0)), *w_in,
              pl.BlockSpec((CONV_K, tn), lambda i, j: (0, j))]
    operands = (h, *(w for w, _ in weights), conv_w)
    if state is None:
        return pl.pallas_call(
            _sconv_seq_kernel,
            grid=(m // tm, nj),
            in_specs=common,
            out_specs=[tile, pl.BlockSpec((1, SUBLANES, tn), lambda i, j: (i, 0, j))],
            out_shape=[jax.ShapeDtypeStruct((m, cw), BF16),
                       jax.ShapeDtypeStruct((m // tm, SUBLANES, cw), F32)],
            scratch_shapes=[pltpu.VMEM((nj, SUBLANES, tn), F32)],
            compiler_params=_params(2),
            name="sconv_seq",
        )(*operands)
    return pl.pallas_call(
        _sconv_state_kernel,
        grid=(m // tm, nj),
        in_specs=common + [tile, tile],
        out_specs=[tile, tile, *w_out],
        out_shape=[jax.ShapeDtypeStruct((m, cw), BF16), jax.ShapeDtypeStruct((m, cw), F32),
                   *w_shapes],
        compiler_params=_params(2),
        name="sconv_state",
    )(*operands, *state)


def _lambda_full(lq1_ref, lk1_ref, lq2_ref, lk2_ref, lam_init):
    a = jnp.sum(lq1_ref[...] * lk1_ref[...], axis=-1, keepdims=True)
    b = jnp.sum(lq2_ref[...] * lk2_ref[...], axis=-1, keepdims=True)
    return jnp.exp(a) - jnp.exp(b) + lam_init


def _attn_kernel(lq1_ref, lk1_ref, lq2_ref, lk2_ref, g_ref, q_ref, k_ref, v_ref, o_ref,
                 qst_ref, kb_ref, vt_ref, m_ref, acc_ref, s0_ref, s1_ref, *, bq, lam_init):
    qi = pl.program_id(1)
    t = k_ref.shape[0]
    bk = s0_ref.shape[0]

    @pl.when(qi == 0)
    def _():
        for c in range(t // bq):
            rows = slice(c * bq, (c + 1) * bq)
            kb_ref[rows, :] = k_ref[rows, :].astype(BF16)
            vt_ref[0:V_DIM, rows] = v_ref[rows, :].T.astype(BF16)
        vt_ref[V_DIM:, :] = jnp.ones((vt_ref.shape[0] - V_DIM, t), BF16)

    qt = q_ref[...].astype(F32)
    dim = lax.broadcasted_iota(jnp.int32, qt.shape, 0)
    qst_ref[:, 0:bq] = jnp.where(dim < HEAD_DIM, qt, 0.0).astype(BF16)
    qst_ref[:, bq:2 * bq] = jnp.where(dim >= HEAD_DIM, qt, 0.0).astype(BF16)
    m_ref[...] = jnp.full(m_ref.shape, NEG_INF, F32)
    acc_ref[...] = jnp.zeros(acc_ref.shape, F32)

    def scores(j, s_ref):
        off = pl.multiple_of(j * bk, bk)
        s_ref[...] = _bdot(kb_ref[pl.ds(off, bk), :], qst_ref[...])

    def softmax_pv(j, s_ref, diag_key0=None):
        off = pl.multiple_of(j * bk, bk)
        vtc = vt_ref[:, pl.ds(off, bk)]
        q0 = diag_key0 or 0
        for cols in (slice(q0, bq), slice(bq + q0, 2 * bq)):
            st = s_ref[:, cols]
            if diag_key0 is not None:
                key = lax.broadcasted_iota(jnp.int32, st.shape, 0)
                qry = lax.broadcasted_iota(jnp.int32, st.shape, 1)
                st = jnp.where(key <= qry, st, NEG_INF)
            m_prev = m_ref[:, cols]
            m_new = jnp.maximum(m_prev, jnp.max(st, axis=0, keepdims=True))
            alpha = jnp.exp2(m_prev - m_new)
            pt = jnp.exp2(st - m_new)
            acc_ref[:, cols] = alpha * acc_ref[:, cols] + _bdot(vtc, pt.astype(BF16))
            m_ref[:, cols] = m_new

    def body(jj, carry):
        j = 2 * jj
        scores(j + 1, s1_ref)
        softmax_pv(j, s0_ref)
        scores(j + 2, s0_ref)
        softmax_pv(j + 1, s1_ref)
        return carry

    scores(0, s0_ref)
    lax.fori_loop(0, qi, body, 0)
    scores(2 * qi + 1, s1_ref)
    softmax_pv(2 * qi, s0_ref, diag_key0=0)
    softmax_pv(2 * qi + 1, s1_ref, diag_key0=bk)

    lam = _lambda_full(lq1_ref, lk1_ref, lq2_ref, lk2_ref, lam_init)
    ot = acc_ref[0:V_DIM, :] * (1.0 / acc_ref[V_DIM:V_DIM + 1, :])
    dlt = ot[:, 0:bq] - lam * ot[:, bq:2 * bq]
    ms = jnp.mean(dlt * dlt, axis=0, keepdims=True)
    y = (dlt * lax.rsqrt(ms + SUBLN_EPS)).T * g_ref[...] * (1.0 - lam_init)
    o_ref[...] = y.astype(BF16)


def _prompt_attention(q, k, v, lams, subln_g, lam_init):
    t = k.shape[0]
    bq = _pick(t, (512, 256))
    vec = lambda n: pl.BlockSpec((1, n), lambda h, i: (0, 0))
    return pl.pallas_call(
        functools.partial(_attn_kernel, bq=bq, lam_init=lam_init),
        grid=(N_HEADS, t // bq),
        in_specs=[vec(HEAD_DIM)] * 4 + [vec(V_DIM),
                  pl.BlockSpec((QK_DIM, bq), lambda h, i: (h, i)),
                  pl.BlockSpec((t, QK_DIM), lambda h, i: (0, h)),
                  pl.BlockSpec((t, V_DIM), lambda h, i: (0, h))],
        out_specs=pl.BlockSpec((bq, V_DIM), lambda h, i: (i, h)),
        out_shape=jax.ShapeDtypeStruct((t, ATTN_W), BF16),
        scratch_shapes=[pltpu.VMEM((QK_DIM, 2 * bq), BF16),
                        pltpu.VMEM((t, QK_DIM), BF16),
                        pltpu.VMEM((V_DIM + BF16_ROWS, t), BF16),
                        pltpu.VMEM((1, 2 * bq), F32),
                        pltpu.VMEM((V_DIM + BF16_ROWS, 2 * bq), F32),
                        pltpu.VMEM((bq // 2, 2 * bq), F32),
                        pltpu.VMEM((bq // 2, 2 * bq), F32)],
        compiler_params=_params(2),
        name="prompt_attn",
    )(*lams, subln_g, q, k, v)


def _sattn_kernel(pt_ref, lq1_ref, lk1_ref, lq2_ref, lk2_ref, g_ref, q_ref, kn_ref, vn_ref, *rest,
                  pg, lam_init):
    k_refs, v_refs = rest[:pg], rest[pg:2 * pg]
    o_ref, qm_ref, m_ref, l_ref, acc_ref = rest[2 * pg:]
    step = pl.program_id(1)
    page_rows = PAGE_SIZE * N_HEADS
    lane = lax.broadcasted_iota(jnp.int32, (N_HEADS, QK_DIM), 1)

    @pl.when(step == 0)
    def _():
        q = q_ref[0].astype(F32)
        qm_ref[0:N_HEADS] = jnp.where(lane < HEAD_DIM, q, 0.0)
        qm_ref[N_HEADS:N_MAPS] = jnp.where(lane >= HEAD_DIM, q, 0.0)
        m_ref[...] = jnp.full(m_ref.shape, NEG_INF, F32)
        l_ref[...] = jnp.zeros(l_ref.shape, F32)
        acc_ref[...] = jnp.zeros(acc_ref.shape, F32)

    qm = qm_ref[...]
    qmb = qm.astype(BF16)
    s = jnp.concatenate(
        [lax.dot_general(qmb, k_refs[p][0, 0].reshape(page_rows, QK_DIM).astype(BF16),
                         (((1,), (1,)), ((), ())), preferred_element_type=F32)
         for p in range(pg)], axis=1)
    row = lax.broadcasted_iota(jnp.int32, s.shape, 0)
    col = lax.broadcasted_iota(jnp.int32, s.shape, 1)
    s = jnp.where(col % N_HEADS == row % N_HEADS, s, NEG_INF)
    m_prev = m_ref[...]
    m_new = jnp.maximum(m_prev, jnp.max(s, axis=-1, keepdims=True))
    alpha = jnp.exp2(m_prev - m_new)
    p_all = jnp.exp2(s - m_new)
    l_ref[...] = alpha * l_ref[...] + jnp.sum(p_all, axis=-1, keepdims=True)
    pb = p_all.astype(BF16)
    pv = None
    for p in range(pg):
        part = _bdot(pb[:, p * page_rows:(p + 1) * page_rows],
                     v_refs[p][0, 0].reshape(page_rows, V_DIM).astype(BF16))
        pv = part if pv is None else pv + part
    acc_ref[...] = alpha * acc_ref[...] + pv
    m_ref[...] = m_new

    @pl.when(step == pl.num_programs(1) - 1)
    def _():
        kn = kn_ref[0]
        vn = vn_ref[0]
        kn2 = jnp.concatenate([kn, kn], axis=0)
        vn2 = jnp.concatenate([vn, vn], axis=0)
        s_new = jnp.sum(qm * kn2, axis=-1, keepdims=True)
        m_old = m_ref[...]
        m_fin = jnp.maximum(m_old, s_new)
        a = jnp.exp2(m_old - m_fin)
        p_new = jnp.exp2(s_new - m_fin)
        l_fin = a * l_ref[...] + p_new
        o = (a * acc_ref[...] + p_new * vn2) / l_fin
        lam = _lambda_full(lq1_ref, lk1_ref, lq2_ref, lk2_ref, lam_init)
        dlt = o[0:N_HEADS] - lam * o[N_HEADS:N_MAPS]
        ms = jnp.mean(dlt * dlt, axis=-1, keepdims=True)
        o_ref[0] = dlt * lax.rsqrt(ms + SUBLN_EPS) * g_ref[...] * (1.0 - lam_init)


def _sample_attention(q, k_new, v_new, cache_k, cache_v, layer, page_table, lams, subln_g, lam_init):
    b, n_pages = page_table.shape
    pg = _pick(n_pages, (16, 8, 4, 2, 1))
    vec = lambda n: pl.BlockSpec((1, n), lambda i, s, pt: (0, 0))
    rowspec = pl.BlockSpec((1, N_HEADS, QK_DIM), lambda i, s, pt: (i, 0, 0))
    page = lambda p: pl.BlockSpec((1, 1, PAGE_SIZE, N_HEADS, QK_DIM),
                                  lambda i, s, pt: (layer, pt[i, s * pg + p], 0, 0, 0))
    grid_spec = pltpu.PrefetchScalarGridSpec(
        num_scalar_prefetch=1,
        grid=(b, n_pages // pg),
        in_specs=[vec(HEAD_DIM)] * 4 + [vec(V_DIM), rowspec, rowspec, rowspec]
                 + [page(p) for p in range(pg)] + [page(p) for p in range(pg)],
        out_specs=rowspec,
        scratch_shapes=[pltpu.VMEM((N_MAPS, QK_DIM), F32),
                        pltpu.VMEM((N_MAPS, 1), F32),
                        pltpu.VMEM((N_MAPS, 1), F32),
                        pltpu.VMEM((N_MAPS, V_DIM), F32)],
    )
    return pl.pallas_call(
        functools.partial(_sattn_kernel, pg=pg, lam_init=lam_init),
        grid_spec=grid_spec,
        out_shape=jax.ShapeDtypeStruct((b, N_HEADS, V_DIM), F32),
        compiler_params=_params(2),
        name="sample_attn",
    )(page_table, *lams, subln_g, q, k_new, v_new, *([cache_k] * pg), *([cache_v] * pg))


def _merge_kernel(h_ref, o_ref, u_ref, wga_ref, wgc_ref, wao_ref, wco_ref, out_ref, *w_copy):
    wga, wgc, wao, wco = _bf16_weights((wga_ref, wgc_ref, wao_ref, wco_ref), w_copy)
    for rows in _row_blocks(h_ref.shape[0]):
        h = h_ref[rows, :]
        att = _sigmoid(_bdot(h, wga[...])) * _bdot(o_ref[rows, :].astype(BF16), wao[...])
        conv = _sigmoid(_bdot(h, wgc[...])) * _bdot(u_ref[rows, :], wco[...])
        out_ref[rows, :] = (att + conv).astype(BF16)


def _merge(h, o, u, weights, tm):
    m, d = h.shape
    tn = 256 if weights[0][0].dtype == F32 else _pick(d, (1024, 512))
    w_in, w_out, w_shapes = _weight_specs(weights, tn, d)
    row = lambda w: pl.BlockSpec((tm, w), lambda i, j: (i, 0))
    return pl.pallas_call(
        _merge_kernel,
        grid=(m // tm, d // tn),
        in_specs=[row(d), row(o.shape[1]), row(u.shape[1]), *w_in],
        out_specs=[pl.BlockSpec((tm, tn), lambda i, j: (i, j)), *w_out],
        out_shape=[jax.ShapeDtypeStruct((m, d), BF16), *w_shapes],
        compiler_params=_params(2),
        name="merge",
    )(h, o, u, *(w for w, _ in weights))


def _resid_norm_kernel(a_ref, w_ref, x_ref, gate_ref, g_ref, sc_ref, sh_ref, x1_ref, h_ref, *w_copy):
    (wb_ref,) = _bf16_weights((w_ref,), w_copy)
    for rows in _row_blocks(a_ref.shape[0]):
        x1 = x_ref[rows, :] + _mod_rows(gate_ref, rows) * _bdot(a_ref[rows, :], wb_ref[...])
        x1_ref[rows, :] = x1
        h = _modulated_rmsnorm(x1, g_ref[...], _mod_rows(sc_ref, rows), _mod_rows(sh_ref, rows))
        h_ref[rows, :] = h.astype(BF16)


def _resid_norm(a, w, x, mod, g, tm):
    m, d = x.shape
    k = a.shape[1]
    mb = mod.shape[0]
    w_in, w_out, w_shapes = _weight_specs([(w, 0)], d, d)
    row = lambda width: pl.BlockSpec((tm, width), lambda i, j: (i, 0))
    modcol = lambda c: pl.BlockSpec((mb, d), lambda i, j: (0, c))
    return pl.pallas_call(
        _resid_norm_kernel,
        grid=(m // tm, 1),
        in_specs=[row(k), *w_in, row(d), modcol(2),
                  pl.BlockSpec((1, d), lambda i, j: (0, 0)), modcol(4), modcol(3)],
        out_specs=[row(d), row(d), *w_out],
        out_shape=[jax.ShapeDtypeStruct((m, d), F32), jax.ShapeDtypeStruct((m, d), BF16),
                   *w_shapes],
        compiler_params=_params(2),
        name="resid_norm",
    )(a, w, x, mod, g, mod, mod)


def _ffn_up_seq_kernel(h_ref, w_ref, cwg_ref, cwv_ref, bg_ref, bv_ref,
                       act_ref, tg_ref, tv_ref, carry_ref):
    i, j = pl.program_id(0), pl.program_id(1)
    tn = act_ref.shape[1]

    @pl.when(i == 0)
    def _():
        carry_ref[j] = jnp.zeros(carry_ref.shape[1:], F32)

    prev_g, prev_v = carry_ref[j, 0], carry_ref[j, 1]
    for rows in _row_blocks(h_ref.shape[0]):
        u = _bdot(h_ref[rows, :], w_ref[...])
        ug, uv = u[:, 0:tn], u[:, tn:2 * tn]
        gate = _conv_rows(ug, prev_g, cwg_ref) + bg_ref[...]
        val = _conv_rows(uv, prev_v, cwv_ref) + bv_ref[...]
        act_ref[rows, :] = (gate * _sigmoid(gate) * val).astype(BF16)
        prev_g, prev_v = _last8(ug), _last8(uv)
    carry_ref[j, 0] = prev_g
    carry_ref[j, 1] = prev_v
    tg_ref[0] = prev_g
    tv_ref[0] = prev_v


def _ffn_up_state_kernel(h_ref, wg_ref, wv_ref, cwg_ref, cwv_ref, bg_ref, bv_ref,
                         pg0_ref, pg1_ref, pv0_ref, pv1_ref,
                         act_ref, ug_ref, uv_ref, wcat_ref):
    tn = act_ref.shape[1]
    _cast_weight(wg_ref, wcat_ref.at[:, 0:tn])
    _cast_weight(wv_ref, wcat_ref.at[:, tn:2 * tn])
    u = _bdot(h_ref[...], wcat_ref[...])
    ug, uv = u[:, 0:tn], u[:, tn:2 * tn]
    gate = _conv_state(pg0_ref[...], pg1_ref[...], ug, cwg_ref) + bg_ref[...]
    val = _conv_state(pv0_ref[...], pv1_ref[...], uv, cwv_ref) + bv_ref[...]
    act_ref[...] = (gate * _sigmoid(gate) * val).astype(BF16)
    ug_ref[...] = ug
    uv_ref[...] = uv


def _ffn_up(h, w, conv_w, conv_b, tm, state=None):
    m, d = h.shape
    f = conv_w.shape[1] // 2
    tn = _pick(f, (1408, 512, 256))
    nj = f // tn
    lo = lambda i, j: (0, j)
    hi = lambda i, j: (0, nj + j)
    tile = pl.BlockSpec((tm, tn), lambda i, j: (i, j))
    tile_hi = pl.BlockSpec((tm, tn), lambda i, j: (i, nj + j))
    wcat = pl.BlockSpec((d, 2 * tn), lo)
    conv_b = conv_b.reshape(1, 2 * f)
    small = [pl.BlockSpec((CONV_K, tn), lo), pl.BlockSpec((CONV_K, tn), hi),
             pl.BlockSpec((1, tn), lo), pl.BlockSpec((1, tn), hi)]
    act = pl.BlockSpec((tm, d), lambda i, j: (i, 0))
    if state is None:
        tail = pl.BlockSpec((1, SUBLANES, tn), lambda i, j: (i, 0, j))
        return pl.pallas_call(
            _ffn_up_seq_kernel,
            grid=(m // tm, nj),
            in_specs=[act, wcat, *small],
            out_specs=[tile, tail, tail],
            out_shape=[jax.ShapeDtypeStruct((m, f), BF16),
                       jax.ShapeDtypeStruct((m // tm, SUBLANES, f), F32),
                       jax.ShapeDtypeStruct((m // tm, SUBLANES, f), F32)],
            scratch_shapes=[pltpu.VMEM((nj, 2, SUBLANES, tn), F32)],
            compiler_params=_params(2),
            name="ffn_up_seq",
        )(h, w, conv_w, conv_w, conv_b, conv_b)
    p0, p1 = state
    once = lambda index_map: pl.BlockSpec((d, tn), index_map, pipeline_mode=pl.Buffered(1))
    return pl.pallas_call(
        _ffn_up_state_kernel,
        grid=(m // tm, nj),
        in_specs=[act, once(lo), once(hi), *small, tile, tile, tile_hi, tile_hi],
        out_specs=[tile, tile, tile, wcat],
        out_shape=[jax.ShapeDtypeStruct((m, f), BF16),
                   jax.ShapeDtypeStruct((m, f), F32),
                   jax.ShapeDtypeStruct((m, f), F32),
                   jax.ShapeDtypeStruct((d, 2 * f), BF16)],
        compiler_params=_params(2),
        name="ffn_up_state",
    )(h, w, w, conv_w, conv_w, conv_b, conv_b, p0, p1, p0, p1)


def _ffn_down_kernel(a_ref, w_ref, x_ref, g2_ref, fg_ref, y_ref, *w_copy):
    kk = pl.program_id(1)
    last = pl.num_programs(1) - 1
    (wb_ref,) = _bf16_weights((w_ref,), w_copy)
    blocks = _row_blocks(a_ref.shape[0])

    @pl.when(kk == 0)
    def _():
        for rows in blocks:
            y_ref[rows, :] = _bdot(a_ref[rows, :], wb_ref[...])

    @pl.when((kk > 0) & (kk < last))
    def _():
        for rows in blocks:
            y_ref[rows, :] += _bdot(a_ref[rows, :], wb_ref[...])

    @pl.when(kk == last)
    def _():
        for rows in blocks:
            acc = y_ref[rows, :] + _bdot(a_ref[rows, :], wb_ref[...])
            x = x_ref[rows, :] + _mod_rows(g2_ref, rows) * acc
            ms = jnp.mean(x * x, axis=-1, keepdims=True)
            y_ref[rows, :] = x * lax.rsqrt(ms + NORM_EPS) * fg_ref[...]


def _ffn_down(act, w_down, x, mod, final_g, tm):
    m, d = x.shape
    f = act.shape[1]
    mb = mod.shape[0]
    tk = _pick(f, (512, 256)) if w_down.dtype == F32 else _pick(f, (1408, 512, 256))
    assert f // tk >= 2, "the accumulate-in-output schedule needs at least two K steps"
    w_in, w_out, w_shapes = _weight_specs([(w_down, 0)], tk, k_tiled=True)
    return pl.pallas_call(
        _ffn_down_kernel,
        grid=(m // tm, f // tk),
        in_specs=[pl.BlockSpec((tm, tk), lambda i, k: (i, k)),
                  *w_in,
                  pl.BlockSpec((tm, d), lambda i, k: (i, 0)),
                  pl.BlockSpec((mb, d), lambda i, k: (0, 5)),
                  pl.BlockSpec((1, d), lambda i, k: (0, 0))],
        out_specs=[pl.BlockSpec((tm, d), lambda i, k: (i, 0)), *w_out],
        out_shape=[jax.ShapeDtypeStruct((m, d), F32), *w_shapes],
        compiler_params=_params(2),
        name="ffn_down",
    )(act, w_down, x, mod, final_g)


def _rope_tables(pos):
    inv_freq = ROPE_THETA ** (-jnp.arange(0, ROT_DIM, 2, dtype=F32) / ROT_DIM)
    ang = pos[:, None] * inv_freq[None, :]
    cos, sin = jnp.cos(ang), jnp.sin(ang)
    rest = jnp.zeros((pos.shape[0], HEAD_DIM - ROT_DIM), F32)
    zero = jnp.zeros_like(sin)
    per_map = jnp.stack([jnp.concatenate([cos, cos, rest + 1.0], axis=1),
                         jnp.concatenate([zero, sin, rest], axis=1),
                         jnp.concatenate([-sin, zero, rest], axis=1)])
    return jnp.tile(per_map, (1, 1, LANES // HEAD_DIM))


def kernel(x_prompt, x_sample, cache_k, cache_v, state_conv, state_ffn, page_table, c_prompt, c_sample, w_ada, b_ada, norm1_g, w_in, lambda_q1, lambda_k1, lambda_q2, lambda_k2, subln_g, w_attn_out, conv_w, w_conv_out, w_o, norm2_g, w_up, ffn_conv_w, ffn_conv_b, w_down, final_g):
    depth = w_ada.shape[0]
    bp, seq, d = x_prompt.shape
    bs, dec_seq, _ = x_sample.shape
    assert bp == 1 and dec_seq == 1, "one prompt sequence and one new token per sample sequence"
    assert depth == 1, "the final norm is fused into the ConvFFN-down kernel of the only layer"
    past_len = page_table.shape[1] * PAGE_SIZE
    qk_w = N_HEADS * QK_DIM
    cw = conv_w.shape[-1]
    f = w_down.shape[1]
    col_conv = 2 * qk_w + ATTN_W
    col_ga = col_conv + 3 * cw
    col_gc = col_ga + d
    tm_p = _pick(seq, (1024, 512, 256))
    tm_s = bs

    xp = x_prompt.reshape(seq, d)
    xs = x_sample.reshape(bs, d)
    pad = (-(bs + bp)) % BF16_ROWS
    c_rows = jnp.concatenate([c_sample, c_prompt, jnp.zeros((pad, d), F32)], axis=0)
    rope_p = _rope_tables(jnp.arange(seq, dtype=F32))
    rope_s = _rope_tables(jnp.full((bs,), past_len, dtype=F32))
    fin_g = final_g.reshape(1, d)
    keep = SUBLANES - (CONV_K - 1)

    outs = [[] for _ in range(8)]
    for li in range(depth):
        lam_init = _lambda_init(li)
        mod = _ada(c_rows, w_ada[li], b_ada[li])
        mod_s, mod_p = mod[:bs], mod[bs:bs + bp]
        lams = tuple(a[li].reshape(1, HEAD_DIM) for a in (lambda_q1, lambda_k1, lambda_q2, lambda_k2))
        g1, g2 = norm1_g[li].reshape(1, d), norm2_g[li].reshape(1, d)
        sub_g = subln_g[li].reshape(1, V_DIM)
        w_in_l, w_up_l = w_in[li], w_up[li]

        sc_prev, sf_prev = state_conv[li], state_ffn[li]
        h, q, k, v, wqkv_b = _qkv(xs, g1, mod_s, w_in_l, rope_s, tm_s)
        u, z, wcb_b, wcc_b, wcx_b = _sconv(
            h, [(w_in_l, col_conv), (w_in_l, col_conv + cw), (w_in_l, col_conv + 2 * cw)],
            conv_w[li], tm_s, state=(sc_prev[:, 0], sc_prev[:, 1]))
        per_head = lambda a: a.reshape(bs, N_HEADS, QK_DIM)
        o = _sample_attention(per_head(q), per_head(k), per_head(v), cache_k, cache_v,
                              li, page_table, lams, sub_g, lam_init).reshape(bs, ATTN_W)
        merged, wga_b, wgc_b, wao_b, wco_b = _merge(
            h, o, u, [(w_in_l, col_ga), (w_in_l, col_gc), (w_attn_out[li], 0), (w_conv_out[li], 0)], tm_s)
        x1, h2, wo_b = _resid_norm(merged, w_o[li], xs, mod_s, g2, tm_s)
        act, ug, uv, wup_b = _ffn_up(h2, w_up_l, ffn_conv_w[li], ffn_conv_b[li], tm_s,
                                     state=(sf_prev[:, 0], sf_prev[:, 1]))
        xs, wdn_b = _ffn_down(act, w_down[li], x1, mod_s, fin_g, tm_s)
        outs[4].append(k.reshape(bs, dec_seq, N_HEADS, QK_DIM))
        outs[5].append(v.reshape(bs, dec_seq, N_HEADS, V_DIM))
        outs[6].append(jnp.stack([sc_prev[:, 1], z], axis=1))
        outs[7].append(jnp.stack([sf_prev[:, 1], jnp.concatenate([ug, uv], axis=-1)], axis=1))

        h, q, k, v = _qkv(xp, g1, mod_p, wqkv_b, rope_p, tm_p)
        u, tail = _sconv(h, [(wcb_b, 0), (wcc_b, 0), (wcx_b, 0)], conv_w[li], tm_p)
        o = _prompt_attention(q, k, v, lams, sub_g, lam_init)
        (merged,) = _merge(h, o, u, [(wga_b, 0), (wgc_b, 0), (wao_b, 0), (wco_b, 0)], tm_p)
        x1, h2 = _resid_norm(merged, wo_b, xp, mod_p, g2, min(tm_p, 512))
        act, tg, tv = _ffn_up(h2, wup_b, ffn_conv_w[li], ffn_conv_b[li], tm_p)
        (xp,) = _ffn_down(act, wdn_b, x1, mod_p, fin_g, tm_p)
        outs[0].append(k.reshape(bp, seq, N_HEADS, QK_DIM))
        outs[1].append(v.reshape(bp, seq, N_HEADS, V_DIM))
        outs[2].append(tail[-1, keep:].reshape(bp, CONV_K - 1, cw))
        outs[3].append(jnp.concatenate([tg[-1, keep:], tv[-1, keep:]], axis=-1).reshape(bp, CONV_K - 1, 2 * f))

    return (xp.reshape(bp, seq, d), xs.reshape(bs, dec_seq, d),
            *(jnp.stack(o) for o in outs))
```

```python
import functools
import math

import jax
import jax.numpy as jnp
from jax import lax
from jax.experimental import pallas as pl
from jax.experimental.pallas import tpu as pltpu

F32 = jnp.float32
BF16 = jnp.bfloat16

N_HEADS = 8
HEAD_DIM = 64
QK_DIM = 2 * HEAD_DIM
V_DIM = 2 * HEAD_DIM
ROT_DIM = HEAD_DIM // 4
ROT_HALF = ROT_DIM // 2
ROPE_THETA = 500000.0
ATTN_W = N_HEADS * V_DIM
CONV_K = 3
NORM_EPS = 1e-6
SUBLN_EPS = 1e-5
ATTN_SCALE = HEAD_DIM ** -0.5
Q_SCALE = ATTN_SCALE * math.log2(math.e)
PAGE_SIZE = 128
N_MAPS = 2 * N_HEADS

LANES = 128
SUBLANES = 8
BF16_ROWS = 2 * SUBLANES
MXU_DIM = 256
VMEM_LIMIT_BYTES = 56 * 1024 * 1024
QKV_VMEM_LIMIT_BYTES = 60 * 1024 * 1024
ROW_SUB = MXU_DIM
CAST_ROWS = 256
NEG_INF = float("-inf")


def _lambda_init(layer):
    return 0.8 - 0.6 * math.exp(-0.3 * layer)


def _params(n_axes, vmem_limit=VMEM_LIMIT_BYTES):
    return pltpu.CompilerParams(dimension_semantics=("arbitrary",) * n_axes,
                                vmem_limit_bytes=vmem_limit)


def _sigmoid(x):
    return 1.0 / (1.0 + jnp.exp(-x))


def _bdot(a, b):
    return jnp.dot(a, b, preferred_element_type=F32)


def _pick(n, candidates):
    for c in candidates:
        if n % c == 0:
            return c
    return n


def _row_blocks(tm):
    return [slice(r, min(r + ROW_SUB, tm)) for r in range(0, tm, ROW_SUB)]


def _cast_weight(w_ref, wb_ref):
    k = w_ref.shape[0]
    for r in range(0, k, CAST_ROWS):
        rows = slice(r, min(r + CAST_ROWS, k))
        wb_ref[rows, :] = w_ref[rows, :].astype(BF16)


def _bf16_weights(w_refs, copy_refs):
    if not copy_refs:
        return w_refs
    for w_ref, wb_ref in zip(w_refs, copy_refs, strict=True):
        _cast_weight(w_ref, wb_ref)
    return copy_refs


def _weight_specs(weights, tn, width=None, k_tiled=False):
    in_specs, copy_specs, copy_shapes = [], [], []
    emit = weights[0][0].dtype == F32
    for w, col in weights:
        k, n = w.shape
        if k_tiled:
            in_specs.append(pl.BlockSpec((tn, n), lambda i, j: (j, 0)))
            copy = (pl.BlockSpec((tn, n), lambda i, j: (j, 0)), (k, n))
        else:
            in_specs.append(pl.BlockSpec((k, tn), lambda i, j, c=col // tn: (0, c + j)))
            copy = (pl.BlockSpec((k, tn), lambda i, j: (0, j)), (k, width))
        if emit:
            copy_specs.append(copy[0])
            copy_shapes.append(jax.ShapeDtypeStruct(copy[1], BF16))
    return in_specs, copy_specs, copy_shapes


def _mod_rows(ref, rows):
    return ref[...] if ref.shape[0] == 1 else ref[rows, :]


def _ada_kernel(c_ref, w_ref, b_ref, o_ref, wb_ref):
    c = c_ref[...]
    s = (c * _sigmoid(c)).astype(BF16)
    _cast_weight(w_ref, wb_ref)
    o_ref[...] = _bdot(s, wb_ref[...]) + b_ref[...]


def _ada(c_rows, w_ada, b_ada):
    m, d = c_rows.shape
    n = w_ada.shape[1]
    tn = _pick(n, (1024, 512, 256, 128))
    return pl.pallas_call(
        _ada_kernel,
        grid=(n // tn,),
        in_specs=[pl.BlockSpec((m, d), lambda j: (0, 0)),
                  pl.BlockSpec((d, tn), lambda j: (0, j)),
                  pl.BlockSpec((1, tn), lambda j: (0, j))],
        out_specs=pl.BlockSpec((m, tn), lambda j: (0, j)),
        out_shape=jax.ShapeDtypeStruct((m, n), F32),
        scratch_shapes=[pltpu.VMEM((d, tn), BF16)],
        compiler_params=_params(1),
        name="ada_mod",
    )(c_rows, w_ada, b_ada.reshape(1, n))


def _modulated_rmsnorm(x, g, sc, sh):
    ms = jnp.mean(x * x, axis=-1, keepdims=True)
    y = x * lax.rsqrt(ms + NORM_EPS)
    return y * g * (1.0 + sc) + sh


def _norm_prologue(x_ref, g_ref, sc_ref, sh_ref, h_ref):
    for rows in _row_blocks(x_ref.shape[0]):
        h = _modulated_rmsnorm(x_ref[rows, :], g_ref[...], _mod_rows(sc_ref, rows), _mod_rows(sh_ref, rows))
        h_ref[rows, :] = h.astype(BF16)


def _rope_tile(r, rope_ref, rows):
    cos, sa, sb = rope_ref[0, rows, :], rope_ref[1, rows, :], rope_ref[2, rows, :]
    outs = []
    for hh in range(r.shape[1] // LANES):
        xh = r[:, hh * LANES:(hh + 1) * LANES]
        outs.append(xh * cos + pltpu.roll(xh, ROT_HALF, 1) * sa
                    + pltpu.roll(xh, LANES - ROT_HALF, 1) * sb)
    return jnp.concatenate(outs, axis=1) if len(outs) > 1 else outs[0]


def _qkv_kernel(x_ref, g_ref, sc_ref, sh_ref, w_ref, rope_ref, h_ref, q_ref, k_ref, v_ref,
                *w_copy, nq, q_transposed):
    j = pl.program_id(1)

    @pl.when(j == 0)
    def _():
        _norm_prologue(x_ref, g_ref, sc_ref, sh_ref, h_ref)

    (wb_ref,) = _bf16_weights((w_ref,), w_copy)

    def run(epilogue):
        for rows in _row_blocks(h_ref.shape[0]):
            epilogue(rows, _bdot(h_ref[rows, :], wb_ref[...]))

    @pl.when(j < nq)
    def _():
        def epilogue(rows, res):
            q = _rope_tile(res, rope_ref, rows) * Q_SCALE
            if q_transposed:
                q_ref[:, rows] = q.T.astype(BF16)
            else:
                q_ref[rows, :] = q.astype(BF16)
        run(epilogue)

    @pl.when((j >= nq) & (j < 2 * nq))
    def _():
        def epilogue(rows, res):
            k_ref[rows, :] = _rope_tile(res, rope_ref, rows)
        run(epilogue)

    @pl.when(j >= 2 * nq)
    def _():
        def epilogue(rows, res):
            v_ref[rows, :] = res
        run(epilogue)


def _qkv(x, g, mod, w, rope, tm):
    m, d = x.shape
    mb = mod.shape[0]
    width = N_HEADS * QK_DIM
    q_transposed = w.dtype == BF16
    tn = width if q_transposed else 512
    nq = width // tn
    q_spec = (pl.BlockSpec((width, tm), lambda i, j: (0, i)) if q_transposed
              else pl.BlockSpec((tm, tn), lambda i, j: (i, jnp.clip(j, 0, nq - 1))))
    q_shape = (width, m) if q_transposed else (m, width)
    w_in, w_out, w_shapes = _weight_specs([(w, 0)], tn, 3 * width)
    row = lambda i, j: (i, 0)
    out_col =lambda lo: (lambda i, j: (i, jnp.clip(j - lo, 0, nq - 1)))
    return pl.pallas_call(
        functools.partial(_qkv_kernel, nq=nq, q_transposed=q_transposed),
        grid=(m // tm, 3 * nq),
        in_specs=[pl.BlockSpec((tm, d), row),
                  pl.BlockSpec((1, d), lambda i, j: (0, 0)),
                  pl.BlockSpec((mb, d), lambda i, j: (0, 1)),
                  pl.BlockSpec((mb, d), lambda i, j: (0, 0)),
                  *w_in,
                  pl.BlockSpec((3, tm, LANES), lambda i, j: (0, i, 0))],
        out_specs=[pl.BlockSpec((tm, d), row),
                   q_spec,
                   pl.BlockSpec((tm, tn), out_col(nq)),
                   pl.BlockSpec((tm, tn), out_col(2 * nq)),
                   *w_out],
        out_shape=[jax.ShapeDtypeStruct((m, d), BF16),
                   jax.ShapeDtypeStruct(q_shape, BF16),
                   jax.ShapeDtypeStruct((m, width), F32),
                   jax.ShapeDtypeStruct((m, width), F32),
                   *w_shapes],
        compiler_params=_params(2, QKV_VMEM_LIMIT_BYTES),
        name="qkv_proj",
    )(x, g, mod, mod, w, rope)


def _conv_rows(z, prev8, w_ref):
    n = z.shape[0]
    zz = jnp.concatenate([prev8, z], axis=0)
    z1 = zz[SUBLANES - 1:SUBLANES - 1 + n]
    z2 = zz[SUBLANES - 2:SUBLANES - 2 + n]
    return z2 * w_ref[0:1, :] + z1 * w_ref[1:2, :] + z * w_ref[2:3, :]


def _conv_state(p0, p1, z, w_ref):
    return p0 * w_ref[0:1, :] + p1 * w_ref[1:2, :] + z * w_ref[2:3, :]


def _last8(z):
    return z[z.shape[0] - SUBLANES:]


def _sconv_seq_kernel(h_ref, wb_ref, wc_ref, wx_ref, cw_ref, u_ref, tail_ref, carry_ref):
    i, j = pl.program_id(0), pl.program_id(1)

    @pl.when(i == 0)
    def _():
        carry_ref[j] = jnp.zeros(carry_ref.shape[1:], F32)

    prev = carry_ref[j]
    for rows in _row_blocks(h_ref.shape[0]):
        h = h_ref[rows, :]
        z = _bdot(h, wc_ref[...]) * _bdot(h, wx_ref[...])
        u_ref[rows, :] = (_bdot(h, wb_ref[...]) * _conv_rows(z, prev, cw_ref)).astype(BF16)
        prev = _last8(z)
    carry_ref[j] = prev
    tail_ref[0] = prev


def _sconv_state_kernel(h_ref, wb_ref, wc_ref, wx_ref, cw_ref, p0_ref, p1_ref, u_ref, z_ref, *w_copy):
    wbb_ref, wcb_ref, wxb_ref = _bf16_weights((wb_ref, wc_ref, wx_ref), w_copy)
    h = h_ref[...]
    z = _bdot(h, wcb_ref[...]) * _bdot(h, wxb_ref[...])
    conv = _conv_state(p0_ref[...], p1_ref[...], z, cw_ref)
    u_ref[...] = (_bdot(h, wbb_ref[...]) * conv).astype(BF16)
    z_ref[...] = z


def _sconv(h, weights, conv_w, tm, state=None):
    m, d = h.shape
    cw = conv_w.shape[1]
    tn = 512 if state is not None else _pick(cw, (1024, 512))
    nj = cw // tn
    w_in, w_out, w_shapes = _weight_specs(weights, tn, cw)
    tile = pl.BlockSpec((tm, tn), lambda i, j: (i, j))
    common = [pl.BlockSpec((tm, d), lambda i, j: (i, 0)), *w_in,
              pl.BlockSpec((CONV_K, tn), lambda i, j: (0, j))]
    operands = (h, *(w for w, _ in weights), conv_w)
    if state is None:
        return pl.pallas_call(
            _sconv_seq_kernel,
            grid=(m // tm, nj),
            in_specs=common,
            out_specs=[tile, pl.BlockSpec((1, SUBLANES, tn), lambda i, j: (i, 0, j))],
            out_shape=[jax.ShapeDtypeStruct((m, cw), BF16),
                       jax.ShapeDtypeStruct((m // tm, SUBLANES, cw), F32)],
            scratch_shapes=[pltpu.VMEM((nj, SUBLANES, tn), F32)],
            compiler_params=_params(2),
            name="sconv_seq",
        )(*operands)
    return pl.pallas_call(
        _sconv_state_kernel,
        grid=(m // tm, nj),
        in_specs=common + [tile, tile],
        out_specs=[tile, tile, *w_out],
        out_shape=[jax.ShapeDtypeStruct((m, cw), BF16), jax.ShapeDtypeStruct((m, cw), F32),
                   *w_shapes],
        compiler_params=_params(2),
        name="sconv_state",
    )(*operands, *state)


def _lambda_full(lq1_ref, lk1_ref, lq2_ref, lk2_ref, lam_init):
    a = jnp.sum(lq1_ref[...] * lk1_ref[...], axis=-1, keepdims=True)
    b = jnp.sum(lq2_ref[...] * lk2_ref[...], axis=-1, keepdims=True)
    return jnp.exp(a) - jnp.exp(b) + lam_init


def _attn_kernel(lq1_ref, lk1_ref, lq2_ref, lk2_ref, g_ref, q_ref, k_ref, v_ref, o_ref,
                 qst_ref, kb_ref, vt_ref, m_ref, acc_ref, s0_ref, s1_ref, *, bq, lam_init):
    qi = pl.program_id(1)
    t = k_ref.shape[0]
    bk = s0_ref.shape[0]

    @pl.when(qi == 0)
    def _():
        for c in range(t // bq):
            rows = slice(c * bq, (c + 1) * bq)
            kb_ref[rows, :] = k_ref[rows, :].astype(BF16)
            vt_ref[0:V_DIM, rows] = v_ref[rows, :].T.astype(BF16)
        vt_ref[V_DIM:, :] = jnp.ones((vt_ref.shape[0] - V_DIM, t), BF16)

    qt = q_ref[...].astype(F32)
    dim = lax.broadcasted_iota(jnp.int32, qt.shape, 0)
    qst_ref[:, 0:bq] = jnp.where(dim < HEAD_DIM, qt, 0.0).astype(BF16)
    qst_ref[:, bq:2 * bq] = jnp.where(dim >= HEAD_DIM, qt, 0.0).astype(BF16)
    m_ref[...] = jnp.full(m_ref.shape, NEG_INF, F32)
    acc_ref[...] = jnp.zeros(acc_ref.shape, F32)

    def scores(j, s_ref, q0=0):
        kc = kb_ref[pl.ds(pl.multiple_of(j * bk, bk), bk), :]
        for cols in ((slice(q0, bq), slice(bq + q0, 2 * bq)) if q0 else (slice(0, 2 * bq),)):
            s_ref[:, cols] = _bdot(kc, qst_ref[:, cols])

    def softmax_pv(j, s_ref, diag_key0=None):
        off = pl.multiple_of(j * bk, bk)
        vtc = vt_ref[:, pl.ds(off, bk)]
        q0 = diag_key0 or 0
        for cols in (slice(q0, bq), slice(bq + q0, 2 * bq)):
            st = s_ref[:, cols]
            if diag_key0 is not None:
                key = lax.broadcasted_iota(jnp.int32, st.shape, 0)
                qry = lax.broadcasted_iota(jnp.int32, st.shape, 1)
                st = jnp.where(key <= qry, st, NEG_INF)
            m_prev = m_ref[:, cols]
            m_new = jnp.maximum(m_prev, jnp.max(st, axis=0, keepdims=True))
            alpha = jnp.exp2(m_prev - m_new)
            pt = jnp.exp2(st - m_new)
            acc_ref[:, cols] = alpha * acc_ref[:, cols] + _bdot(vtc, pt.astype(BF16))
            m_ref[:, cols] = m_new

    def body(jj, carry):
        j = 2 * jj
        scores(j + 1, s1_ref)
        softmax_pv(j, s0_ref)
        scores(j + 2, s0_ref)
        softmax_pv(j + 1, s1_ref)
        return carry

    scores(0, s0_ref)
    lax.fori_loop(0, qi, body, 0)
    scores(2 * qi + 1, s1_ref, q0=bk)
    softmax_pv(2 * qi, s0_ref, diag_key0=0)
    softmax_pv(2 * qi + 1, s1_ref, diag_key0=bk)

    lam = _lambda_full(lq1_ref, lk1_ref, lq2_ref, lk2_ref, lam_init)
    ot = acc_ref[0:V_DIM, :] * (1.0 / acc_ref[V_DIM:V_DIM + 1, :])
    dlt = ot[:, 0:bq] - lam * ot[:, bq:2 * bq]
    ms = jnp.mean(dlt * dlt, axis=0, keepdims=True)
    y = (dlt * lax.rsqrt(ms + SUBLN_EPS)).T * g_ref[...] * (1.0 - lam_init)
    o_ref[...] = y.astype(BF16)


def _prompt_attention(q, k, v, lams, subln_g, lam_init):
    t = k.shape[0]
    bq = _pick(t, (512, 256))
    vec = lambda n: pl.BlockSpec((1, n), lambda h, i: (0, 0))
    return pl.pallas_call(
        functools.partial(_attn_kernel, bq=bq, lam_init=lam_init),
        grid=(N_HEADS, t // bq),
        in_specs=[vec(HEAD_DIM)] * 4 + [vec(V_DIM),
                  pl.BlockSpec((QK_DIM, bq), lambda h, i: (h, i)),
                  pl.BlockSpec((t, QK_DIM), lambda h, i: (0, h)),
                  pl.BlockSpec((t, V_DIM), lambda h, i: (0, h))],
        out_specs=pl.BlockSpec((bq, V_DIM), lambda h, i: (i, h)),
        out_shape=jax.ShapeDtypeStruct((t, ATTN_W), BF16),
        scratch_shapes=[pltpu.VMEM((QK_DIM, 2 * bq), BF16),
                        pltpu.VMEM((t, QK_DIM), BF16),
                        pltpu.VMEM((V_DIM + BF16_ROWS, t), BF16),
                        pltpu.VMEM((1, 2 * bq), F32),
                        pltpu.VMEM((V_DIM + BF16_ROWS, 2 * bq), F32),
                        pltpu.VMEM((bq // 2, 2 * bq), F32),
                        pltpu.VMEM((bq // 2, 2 * bq), F32)],
        compiler_params=_params(2),
        name="prompt_attn",
    )(*lams, subln_g, q, k, v)


def _sattn_kernel(pt_ref, lq1_ref, lk1_ref, lq2_ref, lk2_ref, g_ref, q_ref, kn_ref, vn_ref, *rest,
                  pg, lam_init):
    k_refs, v_refs = rest[:pg], rest[pg:2 * pg]
    o_ref, qm_ref, m_ref, l_ref, acc_ref = rest[2 * pg:]
    step = pl.program_id(1)
    page_rows = PAGE_SIZE * N_HEADS
    lane = lax.broadcasted_iota(jnp.int32, (N_HEADS, QK_DIM), 1)

    @pl.when(step == 0)
    def _():
        q = q_ref[0].astype(F32)
        qm_ref[0:N_HEADS] = jnp.where(lane < HEAD_DIM, q, 0.0)
        qm_ref[N_HEADS:N_MAPS] = jnp.where(lane >= HEAD_DIM, q, 0.0)
        m_ref[...] = jnp.full(m_ref.shape, NEG_INF, F32)
        l_ref[...] = jnp.zeros(l_ref.shape, F32)
        acc_ref[...] = jnp.zeros(acc_ref.shape, F32)

    qm = qm_ref[...]
    qmb = qm.astype(BF16)
    s = jnp.concatenate(
        [lax.dot_general(qmb, k_refs[p][0, 0].reshape(page_rows, QK_DIM).astype(BF16),
                         (((1,), (1,)), ((), ())), preferred_element_type=F32)
         for p in range(pg)], axis=1)
    row = lax.broadcasted_iota(jnp.int32, s.shape, 0)
    col = lax.broadcasted_iota(jnp.int32, s.shape, 1)
    s = jnp.where(col % N_HEADS == row % N_HEADS, s, NEG_INF)
    m_prev = m_ref[...]
    m_new = jnp.maximum(m_prev, jnp.max(s, axis=-1, keepdims=True))
    alpha = jnp.exp2(m_prev - m_new)
    p_all = jnp.exp2(s - m_new)
    l_ref[...] = alpha * l_ref[...] + jnp.sum(p_all, axis=-1, keepdims=True)
    pb = p_all.astype(BF16)
    pv = None
    for p in range(pg):
        part = _bdot(pb[:, p * page_rows:(p + 1) * page_rows],
                     v_refs[p][0, 0].reshape(page_rows, V_DIM).astype(BF16))
        pv = part if pv is None else pv + part
    acc_ref[...] = alpha * acc_ref[...] + pv
    m_ref[...] = m_new

    @pl.when(step == pl.num_programs(1) - 1)
    def _():
        kn = kn_ref[0]
        vn = vn_ref[0]
        kn2 = jnp.concatenate([kn, kn], axis=0)
        vn2 = jnp.concatenate([vn, vn], axis=0)
        s_new = jnp.sum(qm * kn2, axis=-1, keepdims=True)
        m_old = m_ref[...]
        m_fin = jnp.maximum(m_old, s_new)
        a = jnp.exp2(m_old - m_fin)
        p_new = jnp.exp2(s_new - m_fin)
        l_fin = a * l_ref[...] + p_new
        o = (a * acc_ref[...] + p_new * vn2) / l_fin
        lam = _lambda_full(lq1_ref, lk1_ref, lq2_ref, lk2_ref, lam_init)
        dlt = o[0:N_HEADS] - lam * o[N_HEADS:N_MAPS]
        ms = jnp.mean(dlt * dlt, axis=-1, keepdims=True)
        o_ref[0] = dlt * lax.rsqrt(ms + SUBLN_EPS) * g_ref[...] * (1.0 - lam_init)


def _sample_attention(q, k_new, v_new, cache_k, cache_v, layer, page_table, lams, subln_g, lam_init):
    b, n_pages = page_table.shape
    pg = _pick(n_pages, (16, 8, 4, 2, 1))
    vec = lambda n: pl.BlockSpec((1, n), lambda i, s, pt: (0, 0))
    rowspec = pl.BlockSpec((1, N_HEADS, QK_DIM), lambda i, s, pt: (i, 0, 0))
    page = lambda p: pl.BlockSpec((1, 1, PAGE_SIZE, N_HEADS, QK_DIM),
                                  lambda i, s, pt: (layer, pt[i, s * pg + p], 0, 0, 0))
    grid_spec = pltpu.PrefetchScalarGridSpec(
        num_scalar_prefetch=1,
        grid=(b, n_pages // pg),
        in_specs=[vec(HEAD_DIM)] * 4 + [vec(V_DIM), rowspec, rowspec, rowspec]
                 + [page(p) for p in range(pg)] + [page(p) for p in range(pg)],
        out_specs=rowspec,
        scratch_shapes=[pltpu.VMEM((N_MAPS, QK_DIM), F32),
                        pltpu.VMEM((N_MAPS, 1), F32),
                        pltpu.VMEM((N_MAPS, 1), F32),
                        pltpu.VMEM((N_MAPS, V_DIM), F32)],
    )
    return pl.pallas_call(
        functools.partial(_sattn_kernel, pg=pg, lam_init=lam_init),
        grid_spec=grid_spec,
        out_shape=jax.ShapeDtypeStruct((b, N_HEADS, V_DIM), F32),
        compiler_params=_params(2),
        name="sample_attn",
    )(page_table, *lams, subln_g, q, k_new, v_new, *([cache_k] * pg), *([cache_v] * pg))


def _merge_kernel(h_ref, o_ref, u_ref, wga_ref, wgc_ref, wao_ref, wco_ref, out_ref, *w_copy):
    wga, wgc, wao, wco = _bf16_weights((wga_ref, wgc_ref, wao_ref, wco_ref), w_copy)
    for rows in _row_blocks(h_ref.shape[0]):
        h = h_ref[rows, :]
        att = _sigmoid(_bdot(h, wga[...])) * _bdot(o_ref[rows, :].astype(BF16), wao[...])
        conv = _sigmoid(_bdot(h, wgc[...])) * _bdot(u_ref[rows, :], wco[...])
        out_ref[rows, :] = (att + conv).astype(BF16)


def _merge(h, o, u, weights, tm):
    m, d = h.shape
    tn = 256 if weights[0][0].dtype == F32 else _pick(d, (1024, 512))
    w_in, w_out, w_shapes = _weight_specs(weights, tn, d)
    row = lambda w: pl.BlockSpec((tm, w), lambda i, j: (i, 0))
    return pl.pallas_call(
        _merge_kernel,
        grid=(m // tm, d // tn),
        in_specs=[row(d), row(o.shape[1]), row(u.shape[1]), *w_in],
        out_specs=[pl.BlockSpec((tm, tn), lambda i, j: (i, j)), *w_out],
        out_shape=[jax.ShapeDtypeStruct((m, d), BF16), *w_shapes],
        compiler_params=_params(2),
        name="merge",
    )(h, o, u, *(w for w, _ in weights))


def _resid_norm_kernel(a_ref, w_ref, x_ref, gate_ref, g_ref, sc_ref, sh_ref, x1_ref, h_ref, *w_copy):
    (wb_ref,) = _bf16_weights((w_ref,), w_copy)
    for rows in _row_blocks(a_ref.shape[0]):
        x1 = x_ref[rows, :] + _mod_rows(gate_ref, rows) * _bdot(a_ref[rows, :], wb_ref[...])
        x1_ref[rows, :] = x1
        h = _modulated_rmsnorm(x1, g_ref[...], _mod_rows(sc_ref, rows), _mod_rows(sh_ref, rows))
        h_ref[rows, :] = h.astype(BF16)


def _resid_norm(a, w, x, mod, g, tm):
    m, d = x.shape
    k = a.shape[1]
    mb = mod.shape[0]
    w_in, w_out, w_shapes = _weight_specs([(w, 0)], d, d)
    row = lambda width: pl.BlockSpec((tm, width), lambda i, j: (i, 0))
    modcol = lambda c: pl.BlockSpec((mb, d), lambda i, j: (0, c))
    return pl.pallas_call(
        _resid_norm_kernel,
        grid=(m // tm, 1),
        in_specs=[row(k), *w_in, row(d), modcol(2),
                  pl.BlockSpec((1, d), lambda i, j: (0, 0)), modcol(4), modcol(3)],
        out_specs=[row(d), row(d), *w_out],
        out_shape=[jax.ShapeDtypeStruct((m, d), F32), jax.ShapeDtypeStruct((m, d), BF16),
                   *w_shapes],
        compiler_params=_params(2),
        name="resid_norm",
    )(a, w, x, mod, g, mod, mod)


def _ffn_up_seq_kernel(h_ref, w_ref, cwg_ref, cwv_ref, bg_ref, bv_ref,
                       act_ref, tg_ref, tv_ref, carry_ref):
    i, j = pl.program_id(0), pl.program_id(1)
    tn = act_ref.shape[1]

    @pl.when(i == 0)
    def _():
        carry_ref[j] = jnp.zeros(carry_ref.shape[1:], F32)

    prev_g, prev_v = carry_ref[j, 0], carry_ref[j, 1]
    for rows in _row_blocks(h_ref.shape[0]):
        u = _bdot(h_ref[rows, :], w_ref[...])
        ug, uv = u[:, 0:tn], u[:, tn:2 * tn]
        gate = _conv_rows(ug, prev_g, cwg_ref) + bg_ref[...]
        val = _conv_rows(uv, prev_v, cwv_ref) + bv_ref[...]
        act_ref[rows, :] = (gate * _sigmoid(gate) * val).astype(BF16)
        prev_g, prev_v = _last8(ug), _last8(uv)
    carry_ref[j, 0] = prev_g
    carry_ref[j, 1] = prev_v
    tg_ref[0] = prev_g
    tv_ref[0] = prev_v


def _ffn_up_state_kernel(h_ref, wg_ref, wv_ref, cwg_ref, cwv_ref, bg_ref, bv_ref,
                         pg0_ref, pg1_ref, pv0_ref, pv1_ref,
                         act_ref, ug_ref, uv_ref, wcat_ref):
    tn = act_ref.shape[1]
    _cast_weight(wg_ref, wcat_ref.at[:, 0:tn])
    _cast_weight(wv_ref, wcat_ref.at[:, tn:2 * tn])
    u = _bdot(h_ref[...], wcat_ref[...])
    ug, uv = u[:, 0:tn], u[:, tn:2 * tn]
    gate = _conv_state(pg0_ref[...], pg1_ref[...], ug, cwg_ref) + bg_ref[...]
    val = _conv_state(pv0_ref[...], pv1_ref[...], uv, cwv_ref) + bv_ref[...]
    act_ref[...] = (gate * _sigmoid(gate) * val).astype(BF16)
    ug_ref[...] = ug
    uv_ref[...] = uv


def _ffn_up(h, w, conv_w, conv_b, tm, state=None):
    m, d = h.shape
    f = conv_w.shape[1] // 2
    tn = _pick(f, (1408, 512, 256))
    nj = f // tn
    lo = lambda i, j: (0, j)
    hi = lambda i, j: (0, nj + j)
    tile = pl.BlockSpec((tm, tn), lambda i, j: (i, j))
    tile_hi = pl.BlockSpec((tm, tn), lambda i, j: (i, nj + j))
    wcat = pl.BlockSpec((d, 2 * tn), lo)
    conv_b = conv_b.reshape(1, 2 * f)
    small = [pl.BlockSpec((CONV_K, tn), lo), pl.BlockSpec((CONV_K, tn), hi),
             pl.BlockSpec((1, tn), lo), pl.BlockSpec((1, tn), hi)]
    act = pl.BlockSpec((tm, d), lambda i, j: (i, 0))
    if state is None:
        tail = pl.BlockSpec((1, SUBLANES, tn), lambda i, j: (i, 0, j))
        return pl.pallas_call(
            _ffn_up_seq_kernel,
            grid=(m // tm, nj),
            in_specs=[act, wcat, *small],
            out_specs=[tile, tail, tail],
            out_shape=[jax.ShapeDtypeStruct((m, f), BF16),
                       jax.ShapeDtypeStruct((m // tm, SUBLANES, f), F32),
                       jax.ShapeDtypeStruct((m // tm, SUBLANES, f), F32)],
            scratch_shapes=[pltpu.VMEM((nj, 2, SUBLANES, tn), F32)],
            compiler_params=_params(2),
            name="ffn_up_seq",
        )(h, w, conv_w, conv_w, conv_b, conv_b)
    p0, p1 = state
    once = lambda index_map: pl.BlockSpec((d, tn), index_map, pipeline_mode=pl.Buffered(1))
    return pl.pallas_call(
        _ffn_up_state_kernel,
        grid=(m // tm, nj),
        in_specs=[act, once(lo), once(hi), *small, tile, tile, tile_hi, tile_hi],
        out_specs=[tile, tile, tile, wcat],
        out_shape=[jax.ShapeDtypeStruct((m, f), BF16),
                   jax.ShapeDtypeStruct((m, f), F32),
                   jax.ShapeDtypeStruct((m, f), F32),
                   jax.ShapeDtypeStruct((d, 2 * f), BF16)],
        compiler_params=_params(2),
        name="ffn_up_state",
    )(h, w, w, conv_w, conv_w, conv_b, conv_b, p0, p1, p0, p1)


def _ffn_down_kernel(a_ref, w_ref, x_ref, g2_ref, fg_ref, y_ref, *w_copy):
    kk = pl.program_id(1)
    last = pl.num_programs(1) - 1
    (wb_ref,) = _bf16_weights((w_ref,), w_copy)
    blocks = _row_blocks(a_ref.shape[0])

    @pl.when(kk == 0)
    def _():
        for rows in blocks:
            y_ref[rows, :] = _bdot(a_ref[rows, :], wb_ref[...])

    @pl.when((kk > 0) & (kk < last))
    def _():
        for rows in blocks:
            y_ref[rows, :] += _bdot(a_ref[rows, :], wb_ref[...])

    @pl.when(kk == last)
    def _():
        for rows in blocks:
            acc = y_ref[rows, :] + _bdot(a_ref[rows, :], wb_ref[...])
            x = x_ref[rows, :] + _mod_rows(g2_ref, rows) * acc
            ms = jnp.mean(x * x, axis=-1, keepdims=True)
            y_ref[rows, :] = x * lax.rsqrt(ms + NORM_EPS) * fg_ref[...]


def _ffn_down(act, w_down, x, mod, final_g, tm):
    m, d = x.shape
    f = act.shape[1]
    mb = mod.shape[0]
    tk = _pick(f, (512, 256)) if w_down.dtype == F32 else _pick(f, (1408, 512, 256))
    assert f // tk >= 2, "the accumulate-in-output schedule needs at least two K steps"
    w_in, w_out, w_shapes = _weight_specs([(w_down, 0)], tk, k_tiled=True)
    return pl.pallas_call(
        _ffn_down_kernel,
        grid=(m // tm, f // tk),
        in_specs=[pl.BlockSpec((tm, tk), lambda i, k: (i, k)),
                  *w_in,
                  pl.BlockSpec((tm, d), lambda i, k: (i, 0)),
                  pl.BlockSpec((mb, d), lambda i, k: (0, 5)),
                  pl.BlockSpec((1, d), lambda i, k: (0, 0))],
        out_specs=[pl.BlockSpec((tm, d), lambda i, k: (i, 0)), *w_out],
        out_shape=[jax.ShapeDtypeStruct((m, d), F32), *w_shapes],
        compiler_params=_params(2),
        name="ffn_down",
    )(act, w_down, x, mod, final_g)


def _rope_tables(pos):
    inv_freq = ROPE_THETA ** (-jnp.arange(0, ROT_DIM, 2, dtype=F32) / ROT_DIM)
    ang = pos[:, None] * inv_freq[None, :]
    cos, sin = jnp.cos(ang), jnp.sin(ang)
    rest = jnp.zeros((pos.shape[0], HEAD_DIM - ROT_DIM), F32)
    zero = jnp.zeros_like(sin)
    per_map = jnp.stack([jnp.concatenate([cos, cos, rest + 1.0], axis=1),
                         jnp.concatenate([zero, sin, rest], axis=1),
                         jnp.concatenate([-sin, zero, rest], axis=1)])
    return jnp.tile(per_map, (1, 1, LANES // HEAD_DIM))


def kernel(x_prompt, x_sample, cache_k, cache_v, state_conv, state_ffn, page_table, c_prompt, c_sample, w_ada, b_ada, norm1_g, w_in, lambda_q1, lambda_k1, lambda_q2, lambda_k2, subln_g, w_attn_out, conv_w, w_conv_out, w_o, norm2_g, w_up, ffn_conv_w, ffn_conv_b, w_down, final_g):
    depth = w_ada.shape[0]
    bp, seq, d = x_prompt.shape
    bs, dec_seq, _ = x_sample.shape
    assert bp == 1 and dec_seq == 1, "one prompt sequence and one new token per sample sequence"
    assert depth == 1, "the final norm is fused into the ConvFFN-down kernel of the only layer"
    past_len = page_table.shape[1] * PAGE_SIZE
    qk_w = N_HEADS * QK_DIM
    cw = conv_w.shape[-1]
    f = w_down.shape[1]
    col_conv = 2 * qk_w + ATTN_W
    col_ga = col_conv + 3 * cw
    col_gc = col_ga + d
    tm_p = _pick(seq, (1024, 512, 256))
    tm_s = bs

    xp = x_prompt.reshape(seq, d)
    xs = x_sample.reshape(bs, d)
    pad = (-(bs + bp)) % BF16_ROWS
    c_rows = jnp.concatenate([c_sample, c_prompt, jnp.zeros((pad, d), F32)], axis=0)
    rope_p = _rope_tables(jnp.arange(seq, dtype=F32))
    rope_s = _rope_tables(jnp.full((bs,), past_len, dtype=F32))
    fin_g = final_g.reshape(1, d)
    keep = SUBLANES - (CONV_K - 1)

    outs = [[] for _ in range(8)]
    for li in range(depth):
        lam_init = _lambda_init(li)
        mod = _ada(c_rows, w_ada[li], b_ada[li])
        mod_s, mod_p = mod[:bs], mod[bs:bs + bp]
        lams = tuple(a[li].reshape(1, HEAD_DIM) for a in (lambda_q1, lambda_k1, lambda_q2, lambda_k2))
        g1, g2 = norm1_g[li].reshape(1, d), norm2_g[li].reshape(1, d)
        sub_g = subln_g[li].reshape(1, V_DIM)
        w_in_l, w_up_l = w_in[li], w_up[li]

        sc_prev, sf_prev = state_conv[li], state_ffn[li]
        h, q, k, v, wqkv_b = _qkv(xs, g1, mod_s, w_in_l, rope_s, tm_s)
        u, z, wcb_b, wcc_b, wcx_b = _sconv(
            h, [(w_in_l, col_conv), (w_in_l, col_conv + cw), (w_in_l, col_conv + 2 * cw)],
            conv_w[li], tm_s, state=(sc_prev[:, 0], sc_prev[:, 1]))
        per_head = lambda a: a.reshape(bs, N_HEADS, QK_DIM)
        o = _sample_attention(per_head(q), per_head(k), per_head(v), cache_k, cache_v,
                              li, page_table, lams, sub_g, lam_init).reshape(bs, ATTN_W)
        merged, wga_b, wgc_b, wao_b, wco_b = _merge(
            h, o, u, [(w_in_l, col_ga), (w_in_l, col_gc), (w_attn_out[li], 0), (w_conv_out[li], 0)], tm_s)
        x1, h2, wo_b = _resid_norm(merged, w_o[li], xs, mod_s, g2, tm_s)
        act, ug, uv, wup_b = _ffn_up(h2, w_up_l, ffn_conv_w[li], ffn_conv_b[li], tm_s,
                                     state=(sf_prev[:, 0], sf_prev[:, 1]))
        xs, wdn_b = _ffn_down(act, w_down[li], x1, mod_s, fin_g, tm_s)
        outs[4].append(k.reshape(bs, dec_seq, N_HEADS, QK_DIM))
        outs[5].append(v.reshape(bs, dec_seq, N_HEADS, V_DIM))
        outs[6].append(jnp.stack([sc_prev[:, 1], z], axis=1))
        outs[7].append(jnp.stack([sf_prev[:, 1], jnp.concatenate([ug, uv], axis=-1)], axis=1))

        h, q, k, v = _qkv(xp, g1, mod_p, wqkv_b, rope_p, tm_p)
        u, tail = _sconv(h, [(wcb_b, 0), (wcc_b, 0), (wcx_b, 0)], conv_w[li], tm_p)
        o = _prompt_attention(q, k, v, lams, sub_g, lam_init)
        (merged,) = _merge(h, o, u, [(wga_b, 0), (wgc_b, 0), (wao_b, 0), (wco_b, 0)], tm_p)
        x1, h2 = _resid_norm(merged, wo_b, xp, mod_p, g2, min(tm_p, 512))
        act, tg, tv = _ffn_up(h2, wup_b, ffn_conv_w[li], ffn_conv_b[li], tm_p)
        (xp,) = _ffn_down(act, wdn_b, x1, mod_p, fin_g, tm_p)
        outs[0].append(k.reshape(bp, seq, N_HEADS, QK_DIM))
        outs[1].append(v.reshape(bp, seq, N_HEADS, V_DIM))
        outs[2].append(tail[-1, keep:].reshape(bp, CONV_K - 1, cw))
        outs[3].append(jnp.concatenate([tg[-1, keep:], tv[-1, keep:]], axis=-1).reshape(bp, CONV_K - 1, 2 * f))

    return (xp.reshape(bp, seq, d), xs.reshape(bs, dec_seq, d),
            *(jnp.stack(o) for o in outs))
```
